```python
import math
import jax
import jax.numpy as jnp
from jax import lax
import numpy as np

D_MODEL = 1024
BATCH = 8
SEQ = 2048
DEPTH = 4

GRID_W = 64
CTX_LEN = 256
N_MIXERS = 3
EPS = 1e-6

S5_GROUP = 16
S5_GROUPS = D_MODEL // S5_GROUP
S5_STATE = 64
S5_DT_MIN = 1e-3
S5_DT_MAX = 1e-1

HEAD_DIM = 64
N_Q_HEADS = D_MODEL // HEAD_DIM
N_KV_HEADS = 4
Q_PER_KV = N_Q_HEADS // N_KV_HEADS
KV_COLS = 2 * N_KV_HEADS * HEAD_DIM
ROPE_AXIS_DIM = HEAD_DIM // 2
ROPE_THETA = 10000.0
Q_BLOCK = 128

ML_HEADS = 8
ML_QK_DIM = D_MODEL // 2
ML_V_DIM = D_MODEL
ML_DQK = ML_QK_DIM // ML_HEADS
ML_DV = ML_V_DIM // ML_HEADS
ML_GATE_COLS = 2 * 2 * ML_HEADS
ML_STATE_COLS = ML_QK_DIM + ML_V_DIM + ML_GATE_COLS
ML_IN_COLS = ML_STATE_COLS + ML_QK_DIM + ML_V_DIM
ML_CHUNK = 128

N_EXPERTS = 16
N_GROUPS = 4
EXPERTS_PER_GROUP = N_EXPERTS // N_GROUPS
TOP_K = 2
D_FF_EXPERT = D_MODEL // 2

N_A_LAYERS = (DEPTH + 2) // N_MIXERS
N_B_LAYERS = (DEPTH + 1) // N_MIXERS
N_C_LAYERS = DEPTH // N_MIXERS

kernel_name = 'hybrid_s5_gqa_mlstm_moe_dit'


def rmsnorm(x, g):
    x32 = x.astype(jnp.float32)
    y = x32 * lax.rsqrt(jnp.mean(x32 * x32, axis=-1, keepdims=True) + EPS)
    return (y * g.astype(jnp.float32)).astype(x.dtype)


def modulate(h, shift, scale):
    return h * (1.0 + scale) + shift


def axial_rope_tables(n_tokens):
    rows = n_tokens // GRID_W
    row = jnp.repeat(jnp.arange(rows), GRID_W).astype(jnp.float32)
    col = jnp.tile(jnp.arange(GRID_W), rows).astype(jnp.float32)
    half = ROPE_AXIS_DIM // 2
    inv_freq = ROPE_THETA ** (-jnp.arange(half, dtype=jnp.float32) / half)
    ang_r = row[:, None] * inv_freq
    ang_c = col[:, None] * inv_freq
    return jnp.cos(ang_r), jnp.sin(ang_r), jnp.cos(ang_c), jnp.sin(ang_c)


def _rotate_half(x, cos, sin):
    x1, x2 = jnp.split(x, 2, axis=-1)
    cos, sin = cos[:, None, :], sin[:, None, :]
    return jnp.concatenate([x1 * cos - x2 * sin, x2 * cos + x1 * sin], axis=-1)


def apply_axial_rope(x, tables):
    cos_r, sin_r, cos_c, sin_c = tables
    x32 = x.astype(jnp.float32)
    out = jnp.concatenate([_rotate_half(x32[..., :ROPE_AXIS_DIM], cos_r, sin_r),
                           _rotate_half(x32[..., ROPE_AXIS_DIM:], cos_c, sin_c)], axis=-1)
    return out.astype(x.dtype)


def s5_discretise(lam_re, lam_im, log_step, b_re, b_im):
    f32 = jnp.float32
    lam_re, lam_im = lam_re.astype(f32), lam_im.astype(f32)
    dt = jnp.exp(log_step.astype(f32))[:, None]
    mag = jnp.exp(lam_re * dt)
    a_re, a_im = mag * jnp.cos(lam_im * dt), mag * jnp.sin(lam_im * dt)
    den = lam_re * lam_re + lam_im * lam_im
    n_re, n_im = a_re - 1.0, a_im
    f_re = (n_re * lam_re + n_im * lam_im) / den
    f_im = (n_im * lam_re - n_re * lam_im) / den
    b_re, b_im = b_re.astype(f32), b_im.astype(f32)
    bb_re = f_re[..., None] * b_re - f_im[..., None] * b_im
    bb_im = f_re[..., None] * b_im + f_im[..., None] * b_re
    return a_re, a_im, bb_re, bb_im


def _ssm_combine(left, right):
    a_re1, a_im1, x_re1, x_im1 = left
    a_re2, a_im2, x_re2, x_im2 = right
    return (a_re1 * a_re2 - a_im1 * a_im2,
            a_re1 * a_im2 + a_im1 * a_re2,
            a_re2 * x_re1 - a_im2 * x_im1 + x_re2,
            a_re2 * x_im1 + a_im2 * x_re1 + x_im2)


def s5_scan(a_re, a_im, bu_re, bu_im, reverse):
    n = bu_re.shape[1]
    a_re_seq = jnp.broadcast_to(a_re, (1, n) + a_re.shape)
    a_im_seq = jnp.broadcast_to(a_im, (1, n) + a_im.shape)
    _, _, x_re, x_im = lax.associative_scan(
        _ssm_combine, (a_re_seq, a_im_seq, bu_re, bu_im), reverse=reverse, axis=1)
    return x_re, x_im


def s5_direction(u_c, u_l, lam_re, lam_im, log_step, b_re, b_im, c_re, c_im, reverse, with_ctx):
    a_re, a_im, bb_re, bb_im = s5_discretise(lam_re, lam_im, log_step, b_re, b_im)
    c_re, c_im = c_re.astype(jnp.float32), c_im.astype(jnp.float32)

    def drive(u):
        return (jnp.einsum('blgc,gpc->blgp', u, bb_re), jnp.einsum('blgc,gpc->blgp', u, bb_im))

    def readout(x_re, x_im):
        return (jnp.einsum('blgp,gcp->blgc', x_re, c_re)
                - jnp.einsum('blgp,gcp->blgc', x_im, c_im))

    bc_re, bc_im = drive(u_c)
    xc_re, xc_im = s5_scan(a_re, a_im, bc_re, bc_im, reverse)
    last = 0 if reverse else -1
    s_re, s_im = xc_re[:, last], xc_im[:, last]
    bl_re, bl_im = drive(u_l)
    first = -1 if reverse else 0
    bl_re = bl_re.at[:, first].add(a_re * s_re - a_im * s_im)
    bl_im = bl_im.at[:, first].add(a_re * s_im + a_im * s_re)
    xl_re, xl_im = s5_scan(a_re, a_im, bl_re, bl_im, reverse)
    y_l = readout(xl_re, xl_im)
    y_c = readout(xc_re, xc_im) if with_ctx else None
    return y_c, y_l


def s5_mixer(hc, hl, lam_re, lam_im, log_step, b_re, b_im, c_re, c_im, d_skip, w_glu, b_glu, with_ctx):
    def to_groups(h):
        return h.astype(jnp.float32).reshape(h.shape[0], h.shape[1], S5_GROUPS, S5_GROUP)

    u_c, u_l = to_groups(hc), to_groups(hl)
    y_c, y_l = 0.0, 0.0
    for d in range(2):
        yc_d, yl_d = s5_direction(u_c, u_l, lam_re[d], lam_im[d], log_step[d], b_re[d], b_im[d],
                                  c_re[d], c_im[d], d == 1, with_ctx)
        y_l = y_l + yl_d
        if with_ctx:
            y_c = y_c + yc_d

    def glu_out(y, h):
        y = y.reshape(h.shape).astype(h.dtype) + d_skip * h
        a, g = jnp.split(jax.nn.gelu(y) @ w_glu + b_glu, 2, axis=-1)
        return a * jax.nn.sigmoid(g)

    return (glu_out(y_c, hc) if with_ctx else None), glu_out(y_l, hl)


def block_attention(q, k, v):
    b, lq = q.shape[:2]
    nb = lq // Q_BLOCK
    qb = q.reshape(b, nb, Q_BLOCK, N_KV_HEADS, Q_PER_KV, HEAD_DIM).transpose(1, 0, 2, 3, 4, 5)
    scale = HEAD_DIM ** -0.5

    def one_block(q_blk):
        s = jnp.einsum('bqkgd,bskd->bkgqs', q_blk, k).astype(jnp.float32) * scale
        p = jax.nn.softmax(s, axis=-1).astype(v.dtype)
        return jnp.einsum('bkgqs,bskd->bqkgd', p, v)

    out = lax.map(one_block, qb)
    return out.transpose(1, 0, 2, 3, 4, 5).reshape(b, lq, N_Q_HEADS * HEAD_DIM)


def gqa_project(h, w_kvq, q_gain, k_gain, need_q):
    b, n, _ = h.shape
    w = w_kvq if need_q else w_kvq[:, :KV_COLS]
    proj = h @ w
    k = rmsnorm(proj[..., :KV_COLS // 2].reshape(b, n, N_KV_HEADS, HEAD_DIM), k_gain)
    v = proj[..., KV_COLS // 2:KV_COLS].reshape(b, n, N_KV_HEADS, HEAD_DIM)
    q = rmsnorm(proj[..., KV_COLS:].reshape(b, n, N_Q_HEADS, HEAD_DIM), q_gain) if need_q else None
    return q, k, v


def gqa_mixer(hc, hl, w_kvq, q_gain, k_gain, w_o, with_ctx):
    q_c, k_c, v_c = gqa_project(hc, w_kvq, q_gain, k_gain, with_ctx)
    q_l, k_l, v_l = gqa_project(hl, w_kvq, q_gain, k_gain, True)
    tables = axial_rope_tables(hl.shape[1])
    q_l = apply_axial_rope(q_l, tables)
    k_l = apply_axial_rope(k_l, tables)
    k_all = jnp.concatenate([k_l, k_c], axis=1)
    v_all = jnp.concatenate([v_l, v_c], axis=1)
    y_l = block_attention(q_l, k_all, v_all) @ w_o
    y_c = block_attention(q_c, k_c, v_c) @ w_o if with_ctx else None
    return y_c, y_l


def mlstm_project(h, w_in, b_gate, need_q_o):
    b, n, _ = h.shape
    w = w_in if need_q_o else w_in[:, :ML_STATE_COLS]
    proj = (h @ w).astype(jnp.float32)
    k = proj[..., :ML_QK_DIM].reshape(b, n, ML_HEADS, ML_DQK).transpose(0, 2, 1, 3) * (ML_DQK ** -0.5)
    v = proj[..., ML_QK_DIM:ML_QK_DIM + ML_V_DIM].reshape(b, n, ML_HEADS, ML_DV).transpose(0, 2, 1, 3)
    gates = proj[..., ML_QK_DIM + ML_V_DIM:ML_STATE_COLS] + b_gate.astype(jnp.float32)
    gates = gates.reshape(b, n, 2, 2, ML_HEADS).transpose(2, 3, 0, 4, 1)
    ig, lf = gates[:, 0], jax.nn.log_sigmoid(gates[:, 1])
    q, o = None, None
    if need_q_o:
        q = proj[..., ML_STATE_COLS:ML_STATE_COLS + ML_QK_DIM].reshape(b, n, ML_HEADS, ML_DQK).transpose(0, 2, 1, 3)
        o = proj[..., ML_STATE_COLS + ML_QK_DIM:]
    return k, v, ig, lf, q, o


def mlstm_chunk_states(k, v, ig, lf, state0):
    b, h, n, dk = k.shape
    nc = n // ML_CHUNK
    kc = k.reshape(b, h, nc, ML_CHUNK, dk)
    vc = v.reshape(b, h, nc, ML_CHUNK, v.shape[-1])
    cum = jnp.cumsum(lf.reshape(b, h, nc, ML_CHUNK), axis=-1)
    total = cum[..., -1]
    w = total[..., None] - cum + ig.reshape(b, h, nc, ML_CHUNK)
    m_loc = jnp.max(w, axis=-1)
    e = jnp.exp(w - m_loc[..., None])
    c_loc = jnp.einsum('bhncd,bhnce->bhnde', kc * e[..., None], vc)
    n_loc = jnp.einsum('bhncd,bhnc->bhnd', kc, e)

    def step(carry, xs):
        c_prev, n_prev, m_prev = carry
        g, ml, cl, nl = xs
        m_new = jnp.maximum(g + m_prev, ml)
        a = jnp.exp(g + m_prev - m_new)
        bc = jnp.exp(ml - m_new)
        c_new = a[..., None, None] * c_prev + bc[..., None, None] * cl
        n_new = a[..., None] * n_prev + bc[..., None] * nl
        return (c_new, n_new, m_new), (c_prev, n_prev, m_prev)

    xs = (jnp.moveaxis(total, 2, 0), jnp.moveaxis(m_loc, 2, 0),
          jnp.moveaxis(c_loc, 2, 0), jnp.moveaxis(n_loc, 2, 0))
    final, starts = lax.scan(step, state0, xs)
    starts = tuple(jnp.moveaxis(s, 0, 2) for s in starts)
    return final, starts, cum


def mlstm_chunk_outputs(q, k, v, ig, cum, starts):
    c0, n0, m0 = starts
    b, h, n, dk = q.shape
    dv = v.shape[-1]
    nc = n // ML_CHUNK
    qc = q.reshape(b, h, nc, ML_CHUNK, dk)
    kc = k.reshape(b, h, nc, ML_CHUNK, dk)
    vc = v.reshape(b, h, nc, ML_CHUNK, dv)
    igc = ig.reshape(b, h, nc, ML_CHUNK)
    mask = jnp.tril(jnp.ones((ML_CHUNK, ML_CHUNK), dtype=bool))
    log_d = jnp.where(mask, cum[..., :, None] - cum[..., None, :] + igc[..., None, :], -jnp.inf)
    log_inter = cum + m0[..., None]
    m_row = jnp.maximum(log_inter, jnp.max(log_d, axis=-1))
    s = jnp.einsum('bhntd,bhnsd->bhnts', qc, kc) * jnp.exp(log_d - m_row[..., None])
    a_inter = jnp.exp(log_inter - m_row)
    num = (a_inter[..., None] * jnp.einsum('bhntd,bhnde->bhnte', qc, c0)
           + jnp.einsum('bhnts,bhnse->bhnte', s, vc))
    den = a_inter * jnp.einsum('bhntd,bhnd->bhnt', qc, n0) + jnp.sum(s, axis=-1)
    den = jnp.maximum(jnp.abs(den), jnp.exp(-m_row))
    return (num / den[..., None]).reshape(b, h, n, dv)


def mlstm_mixer(hc, hl, w_in, b_gate, norm_g, w_out, with_ctx):
    k_c, v_c, ig_c, lf_c, q_c, o_c = mlstm_project(hc, w_in, b_gate, with_ctx)
    k_l, v_l, ig_l, lf_l, q_l, o_l = mlstm_project(hl, w_in, b_gate, True)
    b = hl.shape[0]
    zero_state = (jnp.zeros((b, ML_HEADS, ML_DQK, ML_DV), jnp.float32),
                  jnp.zeros((b, ML_HEADS, ML_DQK), jnp.float32),
                  jnp.zeros((b, ML_HEADS), jnp.float32))
    h_c, h_l = 0.0, 0.0
    for d in range(2):
        if d == 1:
            flip_t = lambda a: jnp.flip(a, axis=2)
            flip_g = lambda a: jnp.flip(a, axis=-1)
        else:
            flip_t = lambda a: a
            flip_g = lambda a: a
        kc_d, vc_d, igc_d, lfc_d = flip_t(k_c), flip_t(v_c), flip_g(ig_c[d]), flip_g(lf_c[d])
        final_c, starts_c, cum_c = mlstm_chunk_states(kc_d, vc_d, igc_d, lfc_d, zero_state)
        kl_d, vl_d, igl_d, lfl_d = flip_t(k_l), flip_t(v_l), flip_g(ig_l[d]), flip_g(lf_l[d])
        _, starts_l, cum_l = mlstm_chunk_states(kl_d, vl_d, igl_d, lfl_d, final_c)
        h_l = h_l + flip_t(mlstm_chunk_outputs(flip_t(q_l), kl_d, vl_d, igl_d, cum_l, starts_l))
        if with_ctx:
            h_c = h_c + flip_t(mlstm_chunk_outputs(flip_t(q_c), kc_d, vc_d, igc_d, cum_c, starts_c))

    def finish(hh, o, dtype):
        bb, _, n, _ = hh.shape
        hh = rmsnorm(hh.transpose(0, 2, 1, 3), norm_g.reshape(ML_HEADS, ML_DV))
        return (hh.reshape(bb, n, ML_V_DIM) * jax.nn.sigmoid(o)).astype(dtype) @ w_out

    return (finish(h_c, o_c, hc.dtype) if with_ctx else None), finish(h_l, o_l, hl.dtype)


def moe(h, router_w, router_bias, w_gate, w_up, w_down):
    scores = jax.nn.sigmoid((h @ router_w).astype(jnp.float32))
    sel = scores + router_bias.astype(jnp.float32)
    grouped = sel.reshape(-1, N_GROUPS, EXPERTS_PER_GROUP)
    group_score = jnp.sum(lax.top_k(grouped, TOP_K)[0], axis=-1)
    best_group = jnp.argmax(group_score, axis=-1)
    in_group = (jnp.arange(N_EXPERTS) // EXPERTS_PER_GROUP)[None, :] == best_group[:, None]
    _, idx = lax.top_k(jnp.where(in_group, sel, -jnp.inf), TOP_K)
    wk = jnp.take_along_axis(scores, idx, axis=-1)
    wk = wk / jnp.sum(wk, axis=-1, keepdims=True)
    gates = jnp.sum(jax.nn.one_hot(idx, N_EXPERTS, dtype=jnp.float32) * wk[..., None], axis=1).astype(h.dtype)
    y = jnp.zeros_like(h)
    for e in range(N_EXPERTS):
        hid = jax.nn.silu(h @ w_gate[e]) * (h @ w_up[e])
        y = y + gates[:, e:e + 1] * (hid @ w_down[e])
    return y


def setup_inputs(seed: int = 0) -> dict:
    key = jax.random.key(seed)
    keys = jax.random.split(key, 40)
    f32 = jnp.float32
    D = D_MODEL

    def nrm(i, shape, s):
        return jax.random.normal(keys[i], shape, f32) * s

    lam_im_base = math.pi * jnp.arange(S5_STATE, dtype=f32)
    ml_i_bias = nrm(24, (N_C_LAYERS, 2, 1, ML_HEADS), 0.1)
    ml_f_bias = jnp.linspace(3.0, 6.0, ML_HEADS, dtype=f32) + nrm(25, (N_C_LAYERS, 2, 1, ML_HEADS), 0.1)
    return {
        'x': nrm(0, (BATCH, SEQ, D), 1.0),
        'c': nrm(1, (BATCH, D), 1.0),
        'ctx': nrm(2, (BATCH, CTX_LEN, D), 1.0),
        'c_ctx': nrm(3, (D,), 1.0),
        'ada_w': nrm(4, (DEPTH, D, 6 * D), 0.5 * D ** -0.5),
        'ada_b': nrm(5, (DEPTH, 6 * D), 0.02),
        'norm1_g': 1.0 + nrm(6, (DEPTH, D), 0.02),
        'norm2_g': 1.0 + nrm(7, (DEPTH, D), 0.02),
        'final_g': 1.0 + nrm(8, (D,), 0.02),
        's5_lambda_re': -0.5 + nrm(9, (N_A_LAYERS, 2, S5_GROUPS, S5_STATE), 0.01),
        's5_lambda_im': lam_im_base + nrm(10, (N_A_LAYERS, 2, S5_GROUPS, S5_STATE), 0.01),
        's5_log_step': jax.random.uniform(keys[11], (N_A_LAYERS, 2, S5_GROUPS), f32,
                                          math.log(S5_DT_MIN), math.log(S5_DT_MAX)),
        's5_b_re': nrm(12, (N_A_LAYERS, 2, S5_GROUPS, S5_STATE, S5_GROUP), (2 * S5_GROUP) ** -0.5),
        's5_b_im': nrm(13, (N_A_LAYERS, 2, S5_GROUPS, S5_STATE, S5_GROUP), (2 * S5_GROUP) ** -0.5),
        's5_c_re': nrm(14, (N_A_LAYERS, 2, S5_GROUPS, S5_GROUP, S5_STATE), 0.5),
        's5_c_im': nrm(15, (N_A_LAYERS, 2, S5_GROUPS, S5_GROUP, S5_STATE), 0.5),
        's5_d': nrm(16, (N_A_LAYERS, D), 1.0),
        's5_w_glu': nrm(17, (N_A_LAYERS, D, 2 * D), D ** -0.5),
        's5_b_glu': nrm(18, (N_A_LAYERS, 2 * D), 0.02),
        'attn_w_kvq': nrm(19, (N_B_LAYERS, D, KV_COLS + N_Q_HEADS * HEAD_DIM), D ** -0.5),
        'attn_q_gain': 1.0 + nrm(20, (N_B_LAYERS, HEAD_DIM), 0.02),
        'attn_k_gain': 1.0 + nrm(21, (N_B_LAYERS, HEAD_DIM), 0.02),
        'attn_w_o': nrm(22, (N_B_LAYERS, N_Q_HEADS * HEAD_DIM, D), D ** -0.5),
        'ml_w_in': nrm(23, (N_C_LAYERS, D, ML_IN_COLS), D ** -0.5),
        'ml_b_gate': jnp.concatenate([ml_i_bias, ml_f_bias], axis=2).reshape(N_C_LAYERS, ML_GATE_COLS),
        'ml_norm_g': 1.0 + nrm(26, (N_C_LAYERS, ML_V_DIM), 0.02),
        'ml_w_out': nrm(27, (N_C_LAYERS, ML_V_DIM, D), ML_V_DIM ** -0.5),
        'router_w': nrm(28, (D, N_EXPERTS), D ** -0.5),
        'router_bias': nrm(29, (N_EXPERTS,), 0.01),
        'moe_w_gate': nrm(30, (DEPTH, N_EXPERTS, D, D_FF_EXPERT), D ** -0.5),
        'moe_w_up': nrm(31, (DEPTH, N_EXPERTS, D, D_FF_EXPERT), D ** -0.5),
        'moe_w_down': nrm(32, (DEPTH, N_EXPERTS, D_FF_EXPERT, D), D_FF_EXPERT ** -0.5),
    }


def reference(x, c, ctx, c_ctx, ada_w, ada_b, norm1_g, norm2_g, final_g,
              s5_lambda_re, s5_lambda_im, s5_log_step, s5_b_re, s5_b_im, s5_c_re, s5_c_im,
              s5_d, s5_w_glu, s5_b_glu,
              attn_w_kvq, attn_q_gain, attn_k_gain, attn_w_o,
              ml_w_in, ml_b_gate, ml_norm_g, ml_w_out,
              router_w, router_bias, moe_w_gate, moe_w_up, moe_w_down):
    x_l, x_c = x, ctx
    cond_l = jax.nn.silu(c)
    cond_c = jax.nn.silu(c_ctx)
    for layer in range(DEPTH):
        kind, j = layer % N_MIXERS, layer // N_MIXERS
        with_ctx = layer < DEPTH - 1
        mod_l = (cond_l @ ada_w[layer] + ada_b[layer])[:, None, :]
        mod_c = cond_c @ ada_w[layer] + ada_b[layer]
        sh1, sc1, g1, sh2, sc2, g2 = jnp.split(mod_l, 6, axis=-1)
        csh1, csc1, cg1, csh2, csc2, cg2 = jnp.split(mod_c, 6, axis=-1)

        h_l = modulate(rmsnorm(x_l, norm1_g[layer]), sh1, sc1)
        h_c = modulate(rmsnorm(x_c, norm1_g[layer]), csh1, csc1)
        if kind == 0:
            y_c, y_l = s5_mixer(h_c, h_l, s5_lambda_re[j], s5_lambda_im[j], s5_log_step[j],
                                s5_b_re[j], s5_b_im[j], s5_c_re[j], s5_c_im[j],
                                s5_d[j], s5_w_glu[j], s5_b_glu[j], with_ctx)
        elif kind == 1:
            y_c, y_l = gqa_mixer(h_c, h_l, attn_w_kvq[j], attn_q_gain[j], attn_k_gain[j],
                                 attn_w_o[j], with_ctx)
        else:
            y_c, y_l = mlstm_mixer(h_c, h_l, ml_w_in[j], ml_b_gate[j], ml_norm_g[j],
                                   ml_w_out[j], with_ctx)
        x_l = x_l + g1 * y_l
        h2_l = modulate(rmsnorm(x_l, norm2_g[layer]), sh2, sc2)
        if with_ctx:
            x_c = x_c + cg1 * y_c
            h2_c = modulate(rmsnorm(x_c, norm2_g[layer]), csh2, csc2)
            n_ctx_tok = h2_c.shape[0] * h2_c.shape[1]
            tokens = jnp.concatenate([h2_c.reshape(-1, D_MODEL), h2_l.reshape(-1, D_MODEL)], axis=0)
            y = moe(tokens, router_w, router_bias, moe_w_gate[layer], moe_w_up[layer], moe_w_down[layer])
            x_c = x_c + cg2 * y[:n_ctx_tok].reshape(x_c.shape)
            x_l = x_l + g2 * y[n_ctx_tok:].reshape(x_l.shape)
        else:
            y = moe(h2_l.reshape(-1, D_MODEL), router_w, router_bias,
                    moe_w_gate[layer], moe_w_up[layer], moe_w_down[layer])
            x_l = x_l + g2 * y.reshape(x_l.shape)
    return rmsnorm(x_l, final_g)
```

```python
import functools
import math

import jax
import jax.numpy as jnp
from jax import lax
from jax.experimental import pallas as pl
from jax.experimental.pallas import tpu as pltpu

F32 = jnp.float32
BF16 = jnp.bfloat16

D = 1024
CTX = 256
SEQ = 2048
NTOK = CTX + SEQ
GRID_W = 64
EPS = 1e-6
N_MIXERS = 3

TOK = 256
TILES_PER_B = NTOK // TOK
LIN_TM = 512

S5_GROUP = 16
S5_GROUPS = D // S5_GROUP
S5_STATE = 64
S5_CHUNK = 16
S5_NCH = NTOK // S5_CHUNK
S5_CTX_CH = CTX // S5_CHUNK
S5_GB = 2

HEAD_DIM = 64
N_Q_HEADS = 16
N_KV_HEADS = 4
Q_PER_KV = 4
KV_COLS = 2 * N_KV_HEADS * HEAD_DIM
ROPE_THETA = 10000.0
ATT_TQ = 128

ML_HEADS = 8
ML_DQK = 64
ML_DV = 128
ML_QK = 512
ML_CHUNK = 128
ML_NCH = NTOK // ML_CHUNK
ML_CTX_CH = CTX // ML_CHUNK

N_EXPERTS = 16
N_GROUPS = 4
D_FF = 512
MOE_TM = 1024

VMEM_LIMIT = 56 * 1024 * 1024


def _cparams(sem):
    return pltpu.CompilerParams(dimension_semantics=sem, vmem_limit_bytes=VMEM_LIMIT)


def _nt_dot(a, b, **kw):
    return lax.dot_general(a, b, (((1,), (1,)), ((), ())), preferred_element_type=F32, **kw)


def _dot(a, b):
    return jnp.dot(a, b, preferred_element_type=F32)


def _rms(x):
    return x * lax.rsqrt(jnp.mean(x * x, axis=-1, keepdims=True) + EPS)


def _ada_kernel(c_ref, w_ref, b_ref, o_ref):
    c = c_ref[...]
    cond = (c * jax.nn.sigmoid(c)).astype(BF16)
    o_ref[0] = _dot(cond, w_ref[0].astype(BF16)) + b_ref[0]


def _ada(cond_in, ada_w, ada_b):
    depth, _, n = ada_w.shape
    tn = 1536
    rows = cond_in.shape[0]
    return pl.pallas_call(
        _ada_kernel,
        grid=(depth, n // tn),
        in_specs=[pl.BlockSpec((rows, D), lambda l, j: (0, 0)),
                  pl.BlockSpec((1, D, tn), lambda l, j: (l, 0, j)),
                  pl.BlockSpec((1, 1, tn), lambda l, j: (l, 0, j))],
        out_specs=pl.BlockSpec((1, rows, tn), lambda l, j: (l, 0, j)),
        out_shape=jax.ShapeDtypeStruct((depth, rows, n), F32),
        compiler_params=_cparams(("parallel", "parallel")),
    )(cond_in, ada_w, ada_b.reshape(depth, 1, n))


def _mod_row(nb):
    return lambda i: (jnp.where(i % TILES_PER_B == 0, nb, i // TILES_PER_B), 0, 0)


def _pre_kernel(has_res, *refs):
    if has_res:
        x_ref, y_ref, mprev_ref, mcur_ref, g_ref, xo_ref, h_ref = refs
        x = x_ref[...] + mprev_ref[0, 5:6, :] * y_ref[...]
        xo_ref[...] = x
    else:
        x_ref, mcur_ref, g_ref, h_ref = refs
        x = x_ref[...]
    h = _rms(x) * g_ref[...]
    h = h * (1.0 + mcur_ref[0, 1:2, :]) + mcur_ref[0, 0:1, :]
    h_ref[...] = h.astype(h_ref.dtype)


def _pre(x, y2, mods_prev, mods_cur, g, nb, h_dtype):
    t = x.shape[0]
    tok = pl.BlockSpec((TOK, D), lambda i: (i, 0))
    mod = pl.BlockSpec((1, 6, D), _mod_row(nb))
    vec = pl.BlockSpec((1, D), lambda i: (0, 0))
    if y2 is None:
        return x, pl.pallas_call(
            functools.partial(_pre_kernel, False),
            grid=(t // TOK,), in_specs=[tok, mod, vec], out_specs=tok,
            out_shape=jax.ShapeDtypeStruct((t, D), h_dtype),
            compiler_params=_cparams(("parallel",)),
        )(x, mods_cur, g.reshape(1, D))
    return pl.pallas_call(
        functools.partial(_pre_kernel, True),
        grid=(t // TOK,), in_specs=[tok, tok, mod, mod, vec], out_specs=[tok, tok],
        out_shape=[jax.ShapeDtypeStruct((t, D), F32), jax.ShapeDtypeStruct((t, D), h_dtype)],
        compiler_params=_cparams(("parallel",)),
    )(x, y2, mods_prev, mods_cur, g.reshape(1, D))


def _post_kernel(x_ref, y_ref, m_ref, g_ref, rwt_ref, rb_ref, xo_ref, h_ref, gates_ref):
    x = x_ref[...] + m_ref[0, 2:3, :] * y_ref[...]
    xo_ref[...] = x
    h2 = _rms(x) * g_ref[...]
    h2 = h2 * (1.0 + m_ref[0, 4:5, :]) + m_ref[0, 3:4, :]
    h_ref[...] = h2.astype(BF16)

    logits = _nt_dot(rwt_ref[...], h2, precision=lax.Precision.HIGHEST)
    scores = jax.nn.sigmoid(logits)
    sel = scores + rb_ref[...]
    row = lax.broadcasted_iota(jnp.int32, sel.shape, 0)
    per_group = N_EXPERTS // N_GROUPS
    best_val, best = None, None
    for g in range(N_GROUPS):
        r = [sel[per_group * g + j:per_group * g + j + 1, :] for j in range(per_group)]
        gs = None
        for i in range(per_group):
            for j in range(i + 1, per_group):
                gs = r[i] + r[j] if gs is None else jnp.maximum(gs, r[i] + r[j])
        if g == 0:
            best_val, best = gs, jnp.zeros(gs.shape, jnp.int32)
        else:
            upd = gs > best_val
            best = jnp.where(upd, g, best)
            best_val = jnp.where(upd, gs, best_val)
    masked = jnp.where((row // per_group) == best, sel, -jnp.inf)
    m1 = jnp.max(masked, axis=0, keepdims=True)
    i1 = jnp.min(jnp.where(masked == m1, row, N_EXPERTS), axis=0, keepdims=True)
    masked2 = jnp.where(row == i1, -jnp.inf, masked)
    m2 = jnp.max(masked2, axis=0, keepdims=True)
    i2 = jnp.min(jnp.where(masked2 == m2, row, N_EXPERTS), axis=0, keepdims=True)
    pick = (row == i1) | (row == i2)
    ssum = jnp.sum(jnp.where(pick, scores, 0.0), axis=0, keepdims=True)
    gates_t = jnp.where(pick, scores / ssum, 0.0)
    pad = jnp.zeros((128 - N_EXPERTS, gates_t.shape[1]), F32)
    gates_ref[...] = jnp.concatenate([gates_t, pad], axis=0).T[:, :N_EXPERTS]


def _post(x, y, mods, g, router_w, router_bias, nb):
    t = x.shape[0]
    tok = pl.BlockSpec((TOK, D), lambda i: (i, 0))
    return pl.pallas_call(
        _post_kernel,
        grid=(t // TOK,),
        in_specs=[tok, tok, pl.BlockSpec((1, 6, D), _mod_row(nb)),
                  pl.BlockSpec((1, D), lambda i: (0, 0)),
                  pl.BlockSpec((N_EXPERTS, D), lambda i: (0, 0)),
                  pl.BlockSpec((N_EXPERTS, 1), lambda i: (0, 0))],
        out_specs=[tok, tok, pl.BlockSpec((TOK, N_EXPERTS), lambda i: (i, 0))],
        out_shape=[jax.ShapeDtypeStruct((t, D), F32), jax.ShapeDtypeStruct((t, D), BF16),
                   jax.ShapeDtypeStruct((t, N_EXPERTS), F32)],
        compiler_params=_cparams(("parallel",)),
    )(x, y, mods, g.reshape(1, D), router_w.T, router_bias.reshape(N_EXPERTS, 1))


def _final_kernel(x_ref, y_ref, m_ref, g_ref, o_ref):
    x = x_ref[...] + m_ref[0, 5:6, :] * y_ref[...]
    o_ref[...] = _rms(x) * g_ref[...]


def _final(x, y2, mods, g, nb):
    lat_tiles = SEQ // TOK
    src = lambda j: ((j // lat_tiles) * TILES_PER_B + CTX // TOK + j % lat_tiles, 0)
    tok_in = pl.BlockSpec((TOK, D), src)
    return pl.pallas_call(
        _final_kernel,
        grid=(nb * lat_tiles,),
        in_specs=[tok_in, tok_in, pl.BlockSpec((1, 6, D), lambda j: (j // lat_tiles, 0, 0)),
                  pl.BlockSpec((1, D), lambda j: (0, 0))],
        out_specs=pl.BlockSpec((TOK, D), lambda j: (j, 0)),
        out_shape=jax.ShapeDtypeStruct((nb * SEQ, D), F32),
        compiler_params=_cparams(("parallel",)),
    )(x, y2, mods, g.reshape(1, D))


def _linear_kernel(x_ref, w_ref, o_ref):
    o_ref[...] = _dot(x_ref[...].astype(BF16), w_ref[...]).astype(o_ref.dtype)


def _linear(x, w, out_dtype):
    t, k = x.shape
    n = w.shape[1]
    return pl.pallas_call(
        _linear_kernel,
        grid=(t // LIN_TM,),
        in_specs=[pl.BlockSpec((LIN_TM, k), lambda i: (i, 0)), pl.BlockSpec((k, n), lambda i: (0, 0))],
        out_specs=pl.BlockSpec((LIN_TM, n), lambda i: (i, 0)),
        out_shape=jax.ShapeDtypeStruct((t, n), out_dtype),
        compiler_params=_cparams(("parallel",)),
    )(x, w)


def _s5_operators(lam_re, lam_im, log_step, b_re, b_im, c_re, c_im):
    n = S5_CHUNK
    dt = jnp.exp(log_step)[None, :, :, None]
    k = jnp.arange(n + 1, dtype=F32)[:, None, None, None]
    mag = jnp.exp(k * lam_re[None] * dt)
    ang = k * lam_im[None] * dt
    pw_re, pw_im = mag * jnp.cos(ang), mag * jnp.sin(ang)
    a_re, a_im = pw_re[1], pw_im[1]
    den = lam_re * lam_re + lam_im * lam_im
    n_re, n_im = a_re - 1.0, a_im
    f_re = (n_re * lam_re + n_im * lam_im) / den
    f_im = (n_im * lam_re - n_re * lam_im) / den
    bb_re = f_re[..., None] * b_re - f_im[..., None] * b_im
    bb_im = f_re[..., None] * b_im + f_im[..., None] * b_re

    cp_re = c_re[None] * pw_re[:, :, :, None, :] - c_im[None] * pw_im[:, :, :, None, :]
    cp_im = c_re[None] * pw_im[:, :, :, None, :] + c_im[None] * pw_re[:, :, :, None, :]
    kern = (jnp.einsum('kdgop,dgpc->kdgoc', cp_re[:n], bb_re)
            - jnp.einsum('kdgop,dgpc->kdgoc', cp_im[:n], bb_im))
    s_idx = jnp.arange(n)[:, None]
    t_idx = jnp.arange(n)[None, :]
    lag_f, lag_b = t_idx - s_idx, s_idx - t_idx
    kf = jnp.where((lag_f >= 0)[:, :, None, None, None], kern[jnp.clip(lag_f, 0, n - 1), 0], 0.0)
    kb = jnp.where((lag_b >= 0)[:, :, None, None, None], kern[jnp.clip(lag_b, 0, n - 1), 1], 0.0)
    toep = (kf + kb).transpose(2, 0, 4, 1, 3).reshape(S5_GROUPS, n * S5_GROUP, n * S5_GROUP)

    def state_in(pw_sel, d):
        pr, pi = pw_re[pw_sel, d], pw_im[pw_sel, d]
        e_re = pr[..., None] * bb_re[d][None] - pi[..., None] * bb_im[d][None]
        e_im = pr[..., None] * bb_im[d][None] + pi[..., None] * bb_re[d][None]
        to_w = lambda e: jnp.pad(e.transpose(1, 0, 3, 2).reshape(S5_GROUPS, n * S5_GROUP, S5_STATE),
                                 ((0, 0), (0, 0), (0, 128 - S5_STATE)))
        return to_w(e_re), to_w(e_im)

    def state_out(pw_sel, d):
        o_re, o_im = cp_re[pw_sel, d], cp_im[pw_sel, d]
        to_w = lambda o: jnp.pad(o.transpose(1, 3, 0, 2).reshape(S5_GROUPS, S5_STATE, n * S5_GROUP),
                                 ((0, 0), (0, 128 - S5_STATE), (0, 0)))
        return to_w(o_re), to_w(-o_im)

    ar = jnp.arange(n)
    w1 = jnp.concatenate([toep, *state_in(n - 1 - ar, 0), *state_in(ar, 1)], axis=2)
    w2 = jnp.concatenate([*state_out(ar + 1, 0), *state_out(n - ar, 1)], axis=1)
    pad = lambda a: jnp.pad(a, ((0, 0), (0, 128 - S5_STATE)))
    a16 = jnp.stack([pad(pw_re[n, 0]), pad(pw_im[n, 0]), pad(pw_re[n, 1]), pad(pw_im[n, 1])], axis=1)
    return w1.astype(BF16), w2.astype(BF16), a16


def _s5_kernel(nb, u_ref, w1_ref, w2_ref, a_ref, y_ref, s_scr, x_scr):
    for gi in range(S5_GB):
        s_scr[gi] = _dot(u_ref[gi], w1_ref[gi])

    def step(i, carry):
        cf = i
        cb = jnp.where(i < S5_CTX_CH, S5_CTX_CH - 1 - i, S5_NCH + S5_CTX_CH - 1 - i)
        rows_f = pl.ds(pl.multiple_of(cf * nb, nb), nb)
        rows_b = pl.ds(pl.multiple_of(cb * nb, nb), nb)
        new = []
        for gi in range(S5_GB):
            fr, fi, br, bi = carry[gi]
            x_scr[gi, rows_f, 0:128] = fr
            x_scr[gi, rows_f, 128:256] = fi
            x_scr[gi, rows_b, 256:384] = br
            x_scr[gi, rows_b, 384:512] = bi
            ar, ai = a_ref[gi, 0:1, :], a_ref[gi, 1:2, :]
            nfr = ar * fr - ai * fi + s_scr[gi, rows_f, 256:384]
            nfi = ar * fi + ai * fr + s_scr[gi, rows_f, 384:512]
            ar, ai = a_ref[gi, 2:3, :], a_ref[gi, 3:4, :]
            nbr = ar * br - ai * bi + s_scr[gi, rows_b, 512:640]
            nbi = ar * bi + ai * br + s_scr[gi, rows_b, 640:768]
            new.append((nfr, nfi, nbr, nbi))
        return tuple(new)

    zero = jnp.zeros((nb, 128), F32)
    lax.fori_loop(0, S5_NCH, step, tuple((zero, zero, zero, zero) for _ in range(S5_GB)))
    for gi in range(S5_GB):
        y_ref[gi] = s_scr[gi, :, 0:256] + _dot(x_scr[gi].astype(BF16), w2_ref[gi])


def _s5_scan(u, w1, w2, a16, nb):
    rows = S5_NCH * nb
    kc = S5_CHUNK * S5_GROUP
    blk = lambda shape: pl.BlockSpec((S5_GB,) + shape, lambda g: (g, 0, 0))
    return pl.pallas_call(
        functools.partial(_s5_kernel, nb),
        grid=(S5_GROUPS // S5_GB,),
        in_specs=[blk((rows, kc)), blk((kc, 768)), blk((512, kc)), blk((4, 128))],
        out_specs=blk((rows, kc)),
        out_shape=jax.ShapeDtypeStruct((S5_GROUPS, rows, kc), F32),
        scratch_shapes=[pltpu.VMEM((S5_GB, rows, 768), F32), pltpu.VMEM((S5_GB, rows, 512), F32)],
        compiler_params=_cparams(("parallel",)),
    )(u, w1, w2, a16)


def _gelu(x):
    return 0.5 * x * (1.0 + jnp.tanh(math.sqrt(2.0 / math.pi) * (x + 0.044715 * (x * x * x))))


def _glu_kernel(y_ref, h_ref, d_ref, w_ref, b_ref, o_ref):
    z = _gelu(y_ref[...] + d_ref[...] * h_ref[...]).astype(BF16)
    r = _dot(z, w_ref[...]) + b_ref[...]
    o_ref[...] = r[:, :D] * jax.nn.sigmoid(r[:, D:])


def _glu(y, h, d_skip, w, b):
    t = y.shape[0]
    tok = pl.BlockSpec((LIN_TM, D), lambda i: (i, 0))
    return pl.pallas_call(
        _glu_kernel,
        grid=(t // LIN_TM,),
        in_specs=[tok, tok, pl.BlockSpec((1, D), lambda i: (0, 0)),
                  pl.BlockSpec((D, 2 * D), lambda i: (0, 0)), pl.BlockSpec((1, 2 * D), lambda i: (0, 0))],
        out_specs=tok,
        out_shape=jax.ShapeDtypeStruct((t, D), F32),
        compiler_params=_cparams(("parallel",)),
    )(y, h, d_skip.reshape(1, D), w, b.reshape(1, 2 * D))


def _s5_mixer(h, nb, lam_re, lam_im, log_step, b_re, b_im, c_re, c_im, d_skip, w_glu, b_glu):
    w1, w2, a16 = _s5_operators(lam_re, lam_im, log_step, b_re, b_im, c_re, c_im)
    u = h.astype(BF16).reshape(nb, S5_NCH, S5_CHUNK, S5_GROUPS, S5_GROUP).transpose(3, 1, 0, 2, 4)
    y = _s5_scan(u.reshape(S5_GROUPS, S5_NCH * nb, S5_CHUNK * S5_GROUP), w1, w2, a16, nb)
    y = y.reshape(S5_GROUPS, S5_NCH, nb, S5_CHUNK, S5_GROUP).transpose(2, 1, 3, 0, 4).reshape(nb * NTOK, D)
    return _glu(y, h, d_skip, w_glu.astype(BF16), b_glu)


def _rope_tables():
    rows = SEQ // GRID_W
    row = jnp.repeat(jnp.arange(rows), GRID_W).astype(F32)
    col = jnp.tile(jnp.arange(GRID_W), rows).astype(F32)
    half = HEAD_DIM // 4
    inv_freq = ROPE_THETA ** (-jnp.arange(half, dtype=F32) / half)
    ang_r, ang_c = row[:, None] * inv_freq, col[:, None] * inv_freq
    cos = jnp.concatenate([jnp.cos(ang_r)] * 2 + [jnp.cos(ang_c)] * 2, axis=1)
    sin = jnp.concatenate([-jnp.sin(ang_r), jnp.sin(ang_r), -jnp.sin(ang_c), jnp.sin(ang_c)], axis=1)
    cos = jnp.concatenate([jnp.ones((CTX, HEAD_DIM), F32), cos], axis=0)
    sin = jnp.concatenate([jnp.zeros((CTX, HEAD_DIM), F32), sin], axis=0)
    return jnp.tile(cos, (1, 2)), jnp.tile(sin, (1, 2))


def _qk_prep_kernel(p_ref, cos_ref, sin_ref, qg_ref, kg_ref, q_ref, k_ref, v_ref):
    lane = lax.broadcasted_iota(jnp.int32, (TOK, 128), 1)
    low_head = lane < HEAD_DIM
    first_half = (lane % 32) < 16
    cos, sin = cos_ref[...], sin_ref[...]

    def norm_rope(x, gain):
        sq = x * x
        s_lo = jnp.sum(jnp.where(low_head, sq, 0.0), axis=1, keepdims=True)
        s_hi = jnp.sum(sq, axis=1, keepdims=True) - s_lo
        inv = lax.rsqrt(jnp.where(low_head, s_lo, s_hi) * (1.0 / HEAD_DIM) + EPS)
        xn = x * inv * gain
        partner = jnp.where(first_half, pltpu.roll(xn, 128 - 16, 1), pltpu.roll(xn, 16, 1))
        return xn * cos + partner * sin

    for j in range(KV_COLS // 2 // 128):
        kt = norm_rope(p_ref[:, 128 * j:128 * (j + 1)], kg_ref[...]).astype(BF16)
        vt = p_ref[:, KV_COLS // 2 + 128 * j:KV_COLS // 2 + 128 * (j + 1)].astype(BF16)
        for hh in range(2):
            k_ref[0, 2 * j + hh] = kt[:, HEAD_DIM * hh:HEAD_DIM * (hh + 1)]
            v_ref[0, 2 * j + hh] = vt[:, HEAD_DIM * hh:HEAD_DIM * (hh + 1)]
    for j in range(D // 128):
        q_ref[:, 128 * j:128 * (j + 1)] = norm_rope(
            p_ref[:, KV_COLS + 128 * j:KV_COLS + 128 * (j + 1)], qg_ref[...]).astype(BF16)


def _qk_prep(proj, q_gain, k_gain, nb):
    t, ncol = proj.shape
    cos, sin = _rope_tables()
    tab = pl.BlockSpec((TOK, 128), lambda i: (i % TILES_PER_B, 0))
    gain = pl.BlockSpec((1, 128), lambda i: (0, 0))
    head_major = pl.BlockSpec((1, N_KV_HEADS, TOK, HEAD_DIM), lambda i: (i // TILES_PER_B, 0, i % TILES_PER_B, 0))
    hm_shape = jax.ShapeDtypeStruct((nb, N_KV_HEADS, NTOK, HEAD_DIM), BF16)
    return pl.pallas_call(
        _qk_prep_kernel,
        grid=(t // TOK,),
        in_specs=[pl.BlockSpec((TOK, ncol), lambda i: (i, 0)), tab, tab, gain, gain],
        out_specs=[pl.BlockSpec((TOK, D), lambda i: (i, 0)), head_major, head_major],
        out_shape=[jax.ShapeDtypeStruct((t, D), BF16), hm_shape, hm_shape],
        compiler_params=_cparams(("parallel",)),
    )(proj, cos, sin, jnp.tile(q_gain, 2).reshape(1, 128), jnp.tile(k_gain, 2).reshape(1, 128))


def _attn_kernel(q_ref, k_ref, v_ref, o_ref):
    is_ctx = pl.program_id(1) < CTX // ATT_TQ
    col = lax.broadcasted_iota(jnp.int32, (1, NTOK), 1)
    hide = jnp.logical_and(is_ctx, col >= CTX)
    outs = []
    for kh in range(N_KV_HEADS):
        qs = jnp.concatenate(
            [q_ref[0, :, HEAD_DIM * (Q_PER_KV * kh + g):HEAD_DIM * (Q_PER_KV * kh + g + 1)] for g in range(Q_PER_KV)],
            axis=0)
        s = _nt_dot(qs, k_ref[0, kh]) * (HEAD_DIM ** -0.5)
        s = jnp.where(hide, -jnp.inf, s)
        p = jnp.exp(s - jnp.max(s, axis=1, keepdims=True))
        o = _dot(p.astype(BF16), v_ref[0, kh]) / jnp.sum(p, axis=1, keepdims=True)
        outs += [o[ATT_TQ * g:ATT_TQ * (g + 1)] for g in range(Q_PER_KV)]
    o_ref[0] = jnp.concatenate(outs, axis=1).astype(BF16)


def _attention(q, k, v, nb):
    kv = pl.BlockSpec((1, N_KV_HEADS, NTOK, HEAD_DIM), lambda b, i: (b, 0, 0, 0))
    qo = pl.BlockSpec((1, ATT_TQ, D), lambda b, i: (b, i, 0))
    return pl.pallas_call(
        _attn_kernel,
        grid=(nb, NTOK // ATT_TQ),
        in_specs=[qo, kv, kv], out_specs=qo,
        out_shape=jax.ShapeDtypeStruct((nb, NTOK, D), BF16),
        compiler_params=_cparams(("parallel", "parallel")),
    )(q.reshape(nb, NTOK, D), k, v)


def _gqa_mixer(h, nb, w_kvq, q_gain, k_gain, w_o):
    proj = _linear(h, w_kvq.astype(BF16), F32)
    q, k, v = _qk_prep(proj, q_gain, k_gain, nb)
    o = _attention(q, k, v, nb)
    return _linear(o.reshape(nb * NTOK, D), w_o.astype(BF16), F32)


def _split3(x):
    hi = x.astype(BF16)
    r = x - hi.astype(F32)
    mid = r.astype(BF16)
    lo = (r - mid.astype(F32)).astype(BF16)
    return hi, mid, lo


def _log_sigmoid(x):
    return jnp.minimum(x, 0.0) - jnp.log1p(jnp.exp(-jnp.abs(x)))


def _mlstm_kernel(h_ref, k_ref, v_ref, q_ref, wg_ref, wgt_ref, bc_ref, br_ref, o_ref,
                  col_scr, row_scr, st_scr):
    hb = h_ref[0]
    g_col = _dot(hb, wg_ref[0]) + bc_ref[0]
    g_row = _nt_dot(wgt_ref[0], hb) + br_ref[0]
    ii = lax.broadcasted_iota(jnp.int32, (ML_CHUNK, ML_CHUNK), 0)
    jj = lax.broadcasted_iota(jnp.int32, (ML_CHUNK, ML_CHUNK), 1)
    low = ii >= jj
    upp = ii <= jj
    tri_l, tri_u = low.astype(BF16), upp.astype(BF16)
    lane_c = lax.broadcasted_iota(jnp.int32, (ML_CHUNK, 8), 1)
    sub_r = lax.broadcasted_iota(jnp.int32, (8, ML_CHUNK), 0)
    for c in range(ML_NCH):
        sl = slice(ML_CHUNK * c, ML_CHUNK * (c + 1))
        gc = g_col[sl]
        parts = _split3(jnp.where(lane_c % 2 == 1, _log_sigmoid(gc), 0.0))
        cf = sum(_dot(tri_l, p) for p in parts)
        cb = sum(_dot(tri_u, p) for p in parts)
        cum = jnp.where((lane_c // 2) % 2 == 1, cb, cf)
        col_scr[sl, :] = jnp.where(lane_c % 2 == 1, cum, gc)
        gr = g_row[:, sl]
        parts = _split3(jnp.where(sub_r % 2 == 1, _log_sigmoid(gr), 0.0))
        rf = sum(_dot(p, tri_u) for p in parts)
        rb = sum(_dot(p, tri_l) for p in parts)
        cum = jnp.where((sub_r // 2) % 2 == 1, rb, rf)
        row_scr[c] = jnp.where(sub_r % 2 == 1, cum, gr)

    o_ref[...] = jnp.zeros(o_ref.shape, F32)
    st_scr[...] = jnp.zeros(st_scr.shape, F32)
    ones_col = (lax.broadcasted_iota(jnp.int32, (ML_CHUNK, ML_DV), 1) == 0).astype(BF16)

    def step(i, ms):
        new_ms = []
        for hl in range(2):
            for d in range(2):
                idx = 2 * hl + d
                if d == 0:
                    c = i
                else:
                    c = jnp.where(i < ML_CTX_CH, ML_CTX_CH - 1 - i, ML_NCH + ML_CTX_CH - 1 - i)
                rows = pl.ds(pl.multiple_of(c * ML_CHUNK, ML_CHUNK), ML_CHUNK)
                q = q_ref[0, rows, ML_DQK * hl:ML_DQK * (hl + 1)]
                ks = k_ref[0, rows, ML_DQK * hl:ML_DQK * (hl + 1)] * (ML_DQK ** -0.5)
                v_aug = jnp.concatenate([v_ref[0, rows, ML_DV * hl:ML_DV * (hl + 1)], ones_col], axis=1)
                ig_c = col_scr[rows, 2 * idx:2 * idx + 1]
                cum_c = col_scr[rows, 2 * idx + 1:2 * idx + 2]
                rr = row_scr[c]
                ig_r, cum_r = rr[2 * idx:2 * idx + 1, :], rr[2 * idx + 1:2 * idx + 2, :]
                g = cum_r[:, ML_CHUNK - 1:ML_CHUNK] if d == 0 else cum_r[:, 0:1]
                m0 = ms[idx]
                c_aug = st_scr[idx]

                log_d = jnp.where(low if d == 0 else upp, cum_c - cum_r + ig_r, -jnp.inf)
                log_inter = cum_c + m0
                m_row = jnp.maximum(log_inter, jnp.max(log_d, axis=1, keepdims=True))
                s = _nt_dot(q, ks) * jnp.exp(log_d - m_row)
                a_inter = jnp.exp(log_inter - m_row)
                nd = a_inter * _dot(q, c_aug.astype(BF16)) + _dot(s.astype(BF16), v_aug)
                den = jnp.maximum(jnp.abs(nd[:, ML_DV:ML_DV + 1]), jnp.exp(-m_row))
                o_ref[0, rows, ML_DV * hl:ML_DV * (hl + 1)] += nd[:, :ML_DV] / den

                w = g - cum_c + ig_c
                m_loc = jnp.max(w, axis=0, keepdims=True)
                ke = (ks.astype(F32) * jnp.exp(w - m_loc)).astype(BF16)
                c_loc = lax.dot_general(ke, v_aug, (((0,), (0,)), ((), ())), preferred_element_type=F32)
                m_new = jnp.maximum(g + m0, m_loc)
                st_scr[idx] = jnp.exp(g + m0 - m_new) * c_aug + jnp.exp(m_loc - m_new) * c_loc
                new_ms.append(m_new)
        return tuple(new_ms)

    lax.fori_loop(0, ML_NCH, step, tuple(jnp.zeros((1, 1), F32) for _ in range(4)))


def _mlstm_scan(h, proj, wg, wgt, b_col, b_row, nb):
    pairs = ML_HEADS // 2
    k_blk = pl.BlockSpec((1, NTOK, 2 * ML_DQK), lambda b, p: (b, 0, p))
    v_blk = pl.BlockSpec((1, NTOK, 2 * ML_DV), lambda b, p: (b, 0, ML_QK // (2 * ML_DV) + p))
    q_blk = pl.BlockSpec((1, NTOK, 2 * ML_DQK), lambda b, p: (b, 0, (ML_QK + D) // (2 * ML_DQK) + p))
    per_pair = lambda shape: pl.BlockSpec((1,) + shape, lambda b, p: (p, 0, 0))
    proj3 = proj.reshape(nb, NTOK, proj.shape[-1])
    return pl.pallas_call(
        _mlstm_kernel,
        grid=(nb, pairs),
        in_specs=[pl.BlockSpec((1, NTOK, D), lambda b, p: (b, 0, 0)), k_blk, v_blk, q_blk,
                  per_pair((D, 8)), per_pair((8, D)), per_pair((1, 8)), per_pair((8, 1))],
        out_specs=pl.BlockSpec((1, NTOK, 2 * ML_DV), lambda b, p: (b, 0, p)),
        out_shape=jax.ShapeDtypeStruct((nb, NTOK, D), F32),
        scratch_shapes=[pltpu.VMEM((NTOK, 8), F32), pltpu.VMEM((ML_NCH, 8, ML_CHUNK), F32),
                        pltpu.VMEM((4, ML_DQK, 2 * ML_DV), F32)],
        compiler_params=_cparams(("parallel", "parallel")),
    )(h.reshape(nb, NTOK, D), proj3, proj3, proj3, wg, wgt, b_col, b_row)


def _ml_finish_kernel(hh_ref, o_ref, g_ref, w_ref, y_ref):
    parts = [_rms(hh_ref[:, ML_DV * i:ML_DV * (i + 1)]) for i in range(ML_HEADS)]
    hn = jnp.concatenate(parts, axis=1) * g_ref[...]
    z = (hn * jax.nn.sigmoid(o_ref[...].astype(F32))).astype(BF16)
    y_ref[...] = _dot(z, w_ref[...])


def _ml_finish(hh, proj, norm_g, w_out):
    t = hh.shape[0]
    tok = pl.BlockSpec((LIN_TM, D), lambda i: (i, 0))
    return pl.pallas_call(
        _ml_finish_kernel,
        grid=(t // LIN_TM,),
        in_specs=[tok, pl.BlockSpec((LIN_TM, D), lambda i: (i, (ML_QK + D + ML_QK) // D)),
                  pl.BlockSpec((1, D), lambda i: (0, 0)), pl.BlockSpec((D, D), lambda i: (0, 0))],
        out_specs=tok,
        out_shape=jax.ShapeDtypeStruct((t, D), F32),
        compiler_params=_cparams(("parallel",)),
    )(hh, proj, norm_g.reshape(1, D), w_out)


def _mlstm_mixer(h, nb, w_in, b_gate, norm_g, w_out):
    n_state = ML_QK + D
    n_gate = 4 * ML_HEADS
    w_main = jnp.concatenate([w_in[:, :n_state], w_in[:, n_state + n_gate:]], axis=1).astype(BF16)
    pairs = ML_HEADS // 2
    wg = w_in[:, n_state:n_state + n_gate].reshape(D, 2, 2, pairs, 2).transpose(3, 0, 4, 1, 2)
    wg = wg.reshape(pairs, D, 8).astype(BF16)
    bg = b_gate.reshape(2, 2, pairs, 2).transpose(2, 3, 0, 1).reshape(pairs, 8)
    proj = _linear(h, w_main, BF16)
    hh = _mlstm_scan(h, proj, wg, wg.transpose(0, 2, 1), bg[:, None, :], bg[:, :, None], nb)
    return _ml_finish(hh.reshape(nb * NTOK, D), proj, norm_g, w_out.astype(BF16))


def _moe_kernel(x_ref, g_ref, wg_ref, wu_ref, wd_ref, o_ref):
    e = pl.program_id(1)

    @pl.when(e == 0)
    def _():
        o_ref[...] = jnp.zeros(o_ref.shape, F32)

    x = x_ref[...]
    a = _dot(x, wg_ref[0])
    hid = (a * jax.nn.sigmoid(a) * _dot(x, wu_ref[0])).astype(BF16)
    lane = lax.broadcasted_iota(jnp.int32, g_ref.shape, 1)
    gate = jnp.sum(jnp.where(lane == e, g_ref[...], 0.0), axis=1, keepdims=True)
    o_ref[...] += gate * _dot(hid, wd_ref[0])


def _moe(h2, gates, w_gate, w_up, w_down):
    t = h2.shape[0]
    return pl.pallas_call(
        _moe_kernel,
        grid=(t // MOE_TM, N_EXPERTS),
        in_specs=[pl.BlockSpec((MOE_TM, D), lambda i, e: (i, 0)),
                  pl.BlockSpec((MOE_TM, N_EXPERTS), lambda i, e: (i, 0)),
                  pl.BlockSpec((1, D, D_FF), lambda i, e: (e, 0, 0)),
                  pl.BlockSpec((1, D, D_FF), lambda i, e: (e, 0, 0)),
                  pl.BlockSpec((1, D_FF, D), lambda i, e: (e, 0, 0))],
        out_specs=pl.BlockSpec((MOE_TM, D), lambda i, e: (i, 0)),
        out_shape=jax.ShapeDtypeStruct((t, D), F32),
        compiler_params=_cparams(("parallel", "arbitrary")),
    )(h2, gates, w_gate.astype(BF16), w_up.astype(BF16), w_down.astype(BF16))


def kernel(x, c, ctx, c_ctx, ada_w, ada_b, norm1_g, norm2_g, final_g, s5_lambda_re, s5_lambda_im, s5_log_step, s5_b_re, s5_b_im, s5_c_re, s5_c_im, s5_d, s5_w_glu, s5_b_glu, attn_w_kvq, attn_q_gain, attn_k_gain, attn_w_o, ml_w_in, ml_b_gate, ml_norm_g, ml_w_out, router_w, router_bias, moe_w_gate, moe_w_up, moe_w_down):
    nb = x.shape[0]
    depth = ada_w.shape[0]
    assert x.shape[1:] == (SEQ, D) and ctx.shape[1:] == (CTX, D) and nb + 1 <= 16
    xs = jnp.concatenate([ctx, x], axis=1).reshape(nb * NTOK, D)
    cond_in = jnp.zeros((16, D), F32).at[:nb].set(c).at[nb].set(c_ctx)
    mods = _ada(cond_in, ada_w, ada_b).reshape(depth, 16, 6, D)

    y2 = None
    for layer in range(depth):
        kind, j = layer % N_MIXERS, layer // N_MIXERS
        xs, h = _pre(xs, y2, mods[layer - 1] if layer else None, mods[layer], norm1_g[layer], nb,
                     F32 if kind == 0 else BF16)
        if kind == 0:
            y = _s5_mixer(h, nb, s5_lambda_re[j], s5_lambda_im[j], s5_log_step[j], s5_b_re[j], s5_b_im[j],
                          s5_c_re[j], s5_c_im[j], s5_d[j], s5_w_glu[j], s5_b_glu[j])
        elif kind == 1:
            y = _gqa_mixer(h, nb, attn_w_kvq[j], attn_q_gain[j], attn_k_gain[j], attn_w_o[j])
        else:
            y = _mlstm_mixer(h, nb, ml_w_in[j], ml_b_gate[j], ml_norm_g[j], ml_w_out[j])
        xs, h2, gates = _post(xs, y, mods[layer], norm2_g[layer], router_w, router_bias, nb)
        y2 = _moe(h2, gates, moe_w_gate[layer], moe_w_up[layer], moe_w_down[layer])
    out = _final(xs, y2, mods[depth - 1], final_g, nb)
    return out.reshape(nb, SEQ, D)
```

```python
import functools
import math

import jax
import jax.numpy as jnp
from jax import lax
from jax.experimental import pallas as pl
from jax.experimental.pallas import tpu as pltpu

F32 = jnp.float32
BF16 = jnp.bfloat16

D = 1024
CTX = 256
SEQ = 2048
NTOK = CTX + SEQ
GRID_W = 64
EPS = 1e-6
N_MIXERS = 3

TOK = 256
TILES_PER_B = NTOK // TOK
LIN_TM = 512

S5_GROUP = 16
S5_GROUPS = D // S5_GROUP
S5_STATE = 64
S5_CHUNK = 16
S5_NCH = NTOK // S5_CHUNK
S5_CTX_CH = CTX // S5_CHUNK
S5_GB = 2

HEAD_DIM = 64
N_Q_HEADS = 16
N_KV_HEADS = 4
Q_PER_KV = 4
KV_COLS = 2 * N_KV_HEADS * HEAD_DIM
ROPE_THETA = 10000.0
ATT_TQ = 128

ML_HEADS = 8
ML_DQK = 64
ML_DV = 128
ML_QK = 512
ML_CHUNK = 128
ML_NCH = NTOK // ML_CHUNK
ML_CTX_CH = CTX // ML_CHUNK

N_EXPERTS = 16
N_GROUPS = 4
D_FF = 512
MOE_TM = 256

VMEM_LIMIT = 56 * 1024 * 1024


def _cparams(sem):
    return pltpu.CompilerParams(dimension_semantics=sem, vmem_limit_bytes=VMEM_LIMIT)


def _nt_dot(a, b, **kw):
    return lax.dot_general(a, b, (((1,), (1,)), ((), ())), preferred_element_type=F32, **kw)


def _dot(a, b):
    return jnp.dot(a, b, preferred_element_type=F32)


def _rms(x):
    return x * lax.rsqrt(jnp.mean(x * x, axis=-1, keepdims=True) + EPS)


def _ada_kernel(c_ref, w_ref, b_ref, o_ref):
    c = c_ref[...]
    cond = (c * jax.nn.sigmoid(c)).astype(BF16)
    o_ref[0] = _dot(cond, w_ref[0].astype(BF16)) + b_ref[0]


def _ada(cond_in, ada_w, ada_b):
    depth, _, n = ada_w.shape
    tn = 1536
    rows = cond_in.shape[0]
    return pl.pallas_call(
        _ada_kernel,
        grid=(depth, n // tn),
        in_specs=[pl.BlockSpec((rows, D), lambda l, j: (0, 0)),
                  pl.BlockSpec((1, D, tn), lambda l, j: (l, 0, j)),
                  pl.BlockSpec((1, 1, tn), lambda l, j: (l, 0, j))],
        out_specs=pl.BlockSpec((1, rows, tn), lambda l, j: (l, 0, j)),
        out_shape=jax.ShapeDtypeStruct((depth, rows, n), F32),
        compiler_params=_cparams(("parallel", "parallel")),
    )(cond_in, ada_w, ada_b.reshape(depth, 1, n))


def _mod_row(nb):
    return lambda i: (jnp.where(i % TILES_PER_B == 0, nb, i // TILES_PER_B), 0, 0)


def _pre_kernel(has_res, *refs):
    if has_res:
        x_ref, y_ref, mprev_ref, mcur_ref, g_ref, xo_ref, h_ref = refs
        x = x_ref[...] + mprev_ref[0, 5:6, :] * y_ref[...]
        xo_ref[...] = x
    else:
        x_ref, mcur_ref, g_ref, h_ref = refs
        x = x_ref[...]
    h = _rms(x) * g_ref[...]
    h = h * (1.0 + mcur_ref[0, 1:2, :]) + mcur_ref[0, 0:1, :]
    h_ref[...] = h.astype(h_ref.dtype)


def _pre(x, y2, mods_prev, mods_cur, g, nb, h_dtype):
    t = x.shape[0]
    tok = pl.BlockSpec((TOK, D), lambda i: (i, 0))
    mod = pl.BlockSpec((1, 6, D), _mod_row(nb))
    vec = pl.BlockSpec((1, D), lambda i: (0, 0))
    if y2 is None:
        return x, pl.pallas_call(
            functools.partial(_pre_kernel, False),
            grid=(t // TOK,), in_specs=[tok, mod, vec], out_specs=tok,
            out_shape=jax.ShapeDtypeStruct((t, D), h_dtype),
            compiler_params=_cparams(("parallel",)),
        )(x, mods_cur, g.reshape(1, D))
    return pl.pallas_call(
        functools.partial(_pre_kernel, True),
        grid=(t // TOK,), in_specs=[tok, tok, mod, mod, vec], out_specs=[tok, tok],
        out_shape=[jax.ShapeDtypeStruct((t, D), F32), jax.ShapeDtypeStruct((t, D), h_dtype)],
        compiler_params=_cparams(("parallel",)),
    )(x, y2, mods_prev, mods_cur, g.reshape(1, D))


def _post_kernel(x_ref, y_ref, m_ref, g_ref, rwt_ref, rb_ref, xo_ref, h_ref, ridx_ref, rw_ref):
    x = x_ref[...] + m_ref[0, 2:3, :] * y_ref[...]
    xo_ref[...] = x
    h2 = _rms(x) * g_ref[...]
    h2 = h2 * (1.0 + m_ref[0, 4:5, :]) + m_ref[0, 3:4, :]
    h_ref[...] = h2

    logits = _nt_dot(rwt_ref[...], h2, precision=lax.Precision.HIGHEST)
    scores = jax.nn.sigmoid(logits)
    sel = scores + rb_ref[...]
    row = lax.broadcasted_iota(jnp.int32, sel.shape, 0)
    per_group = N_EXPERTS // N_GROUPS
    best_val, best = None, None
    for g in range(N_GROUPS):
        r = [sel[per_group * g + j:per_group * g + j + 1, :] for j in range(per_group)]
        gs = None
        for i in range(per_group):
            for j in range(i + 1, per_group):
                gs = r[i] + r[j] if gs is None else jnp.maximum(gs, r[i] + r[j])
        if g == 0:
            best_val, best = gs, jnp.zeros(gs.shape, jnp.int32)
        else:
            upd = gs > best_val
            best = jnp.where(upd, g, best)
            best_val = jnp.where(upd, gs, best_val)
    masked = jnp.where((row // per_group) == best, sel, -jnp.inf)
    m1 = jnp.max(masked, axis=0, keepdims=True)
    i1 = jnp.min(jnp.where(masked == m1, row, N_EXPERTS), axis=0, keepdims=True)
    masked2 = jnp.where(row == i1, -jnp.inf, masked)
    m2 = jnp.max(masked2, axis=0, keepdims=True)
    i2 = jnp.min(jnp.where(masked2 == m2, row, N_EXPERTS), axis=0, keepdims=True)
    lo, hi = jnp.minimum(i1, i2), jnp.maximum(i1, i2)
    s_lo = jnp.sum(jnp.where(row == lo, scores, 0.0), axis=0, keepdims=True)
    s_hi = jnp.sum(jnp.where(row == hi, scores, 0.0), axis=0, keepdims=True)
    ridx_ref[...] = jnp.concatenate([lo, hi], axis=0)
    rw_ref[...] = jnp.concatenate([s_lo, s_hi], axis=0) / (s_lo + s_hi)


def _post(x, y, mods, g, router_w, router_bias, nb):
    t = x.shape[0]
    tok = pl.BlockSpec((TOK, D), lambda i: (i, 0))
    route = pl.BlockSpec((2, TOK), lambda i: (0, i))
    return pl.pallas_call(
        _post_kernel,
        grid=(t // TOK,),
        in_specs=[tok, tok, pl.BlockSpec((1, 6, D), _mod_row(nb)),
                  pl.BlockSpec((1, D), lambda i: (0, 0)),
                  pl.BlockSpec((N_EXPERTS, D), lambda i: (0, 0)),
                  pl.BlockSpec((N_EXPERTS, 1), lambda i: (0, 0))],
        out_specs=[tok, tok, route, route],
        out_shape=[jax.ShapeDtypeStruct((t, D), F32), jax.ShapeDtypeStruct((t, D), F32),
                   jax.ShapeDtypeStruct((2, t), jnp.int32), jax.ShapeDtypeStruct((2, t), F32)],
        compiler_params=_cparams(("parallel",)),
    )(x, y, mods, g.reshape(1, D), router_w.T, router_bias.reshape(N_EXPERTS, 1))


def _final_kernel(x_ref, y_ref, m_ref, g_ref, o_ref):
    x = x_ref[...] + m_ref[0, 5:6, :] * y_ref[...]
    o_ref[...] = _rms(x) * g_ref[...]


def _final(x, y2, mods, g, nb):
    lat_tiles = SEQ // TOK
    src = lambda j: ((j // lat_tiles) * TILES_PER_B + CTX // TOK + j % lat_tiles, 0)
    tok_in = pl.BlockSpec((TOK, D), src)
    return pl.pallas_call(
        _final_kernel,
        grid=(nb * lat_tiles,),
        in_specs=[tok_in, tok_in, pl.BlockSpec((1, 6, D), lambda j: (j // lat_tiles, 0, 0)),
                  pl.BlockSpec((1, D), lambda j: (0, 0))],
        out_specs=pl.BlockSpec((TOK, D), lambda j: (j, 0)),
        out_shape=jax.ShapeDtypeStruct((nb * SEQ, D), F32),
        compiler_params=_cparams(("parallel",)),
    )(x, y2, mods, g.reshape(1, D))


def _linear_kernel(x_ref, w_ref, o_ref):
    o_ref[...] = _dot(x_ref[...].astype(BF16), w_ref[...]).astype(o_ref.dtype)


def _linear(x, w, out_dtype):
    t, k = x.shape
    n = w.shape[1]
    return pl.pallas_call(
        _linear_kernel,
        grid=(t // LIN_TM,),
        in_specs=[pl.BlockSpec((LIN_TM, k), lambda i: (i, 0)), pl.BlockSpec((k, n), lambda i: (0, 0))],
        out_specs=pl.BlockSpec((LIN_TM, n), lambda i: (i, 0)),
        out_shape=jax.ShapeDtypeStruct((t, n), out_dtype),
        compiler_params=_cparams(("parallel",)),
    )(x, w)


def _s5_operators(lam_re, lam_im, log_step, b_re, b_im, c_re, c_im):
    n = S5_CHUNK
    dt = jnp.exp(log_step)[None, :, :, None]
    k = jnp.arange(n + 1, dtype=F32)[:, None, None, None]
    mag = jnp.exp(k * lam_re[None] * dt)
    ang = k * lam_im[None] * dt
    pw_re, pw_im = mag * jnp.cos(ang), mag * jnp.sin(ang)
    a_re, a_im = pw_re[1], pw_im[1]
    den = lam_re * lam_re + lam_im * lam_im
    n_re, n_im = a_re - 1.0, a_im
    f_re = (n_re * lam_re + n_im * lam_im) / den
    f_im = (n_im * lam_re - n_re * lam_im) / den
    bb_re = f_re[..., None] * b_re - f_im[..., None] * b_im
    bb_im = f_re[..., None] * b_im + f_im[..., None] * b_re

    cp_re = c_re[None] * pw_re[:, :, :, None, :] - c_im[None] * pw_im[:, :, :, None, :]
    cp_im = c_re[None] * pw_im[:, :, :, None, :] + c_im[None] * pw_re[:, :, :, None, :]
    kern = (jnp.einsum('kdgop,dgpc->kdgoc', cp_re[:n], bb_re)
            - jnp.einsum('kdgop,dgpc->kdgoc', cp_im[:n], bb_im))
    s_idx = jnp.arange(n)[:, None]
    t_idx = jnp.arange(n)[None, :]
    lag_f, lag_b = t_idx - s_idx, s_idx - t_idx
    kf = jnp.where((lag_f >= 0)[:, :, None, None, None], kern[jnp.clip(lag_f, 0, n - 1), 0], 0.0)
    kb = jnp.where((lag_b >= 0)[:, :, None, None, None], kern[jnp.clip(lag_b, 0, n - 1), 1], 0.0)
    toep = (kf + kb).transpose(2, 0, 4, 1, 3).reshape(S5_GROUPS, n * S5_GROUP, n * S5_GROUP)

    def state_in(pw_sel, d):
        pr, pi = pw_re[pw_sel, d], pw_im[pw_sel, d]
        e_re = pr[..., None] * bb_re[d][None] - pi[..., None] * bb_im[d][None]
        e_im = pr[..., None] * bb_im[d][None] + pi[..., None] * bb_re[d][None]
        to_w = lambda e: jnp.pad(e.transpose(1, 0, 3, 2).reshape(S5_GROUPS, n * S5_GROUP, S5_STATE),
                                 ((0, 0), (0, 0), (0, 128 - S5_STATE)))
        return to_w(e_re), to_w(e_im)

    def state_out(pw_sel, d):
        o_re, o_im = cp_re[pw_sel, d], cp_im[pw_sel, d]
        to_w = lambda o: jnp.pad(o.transpose(1, 3, 0, 2).reshape(S5_GROUPS, S5_STATE, n * S5_GROUP),
                                 ((0, 0), (0, 128 - S5_STATE), (0, 0)))
        return to_w(o_re), to_w(-o_im)

    ar = jnp.arange(n)
    w1 = jnp.concatenate([toep, *state_in(n - 1 - ar, 0), *state_in(ar, 1)], axis=2)
    w2 = jnp.concatenate([*state_out(ar + 1, 0), *state_out(n - ar, 1)], axis=1)
    pad = lambda a: jnp.pad(a, ((0, 0), (0, 128 - S5_STATE)))
    a16 = jnp.stack([pad(pw_re[n, 0]), pad(pw_im[n, 0]), pad(pw_re[n, 1]), pad(pw_im[n, 1])], axis=1)
    return w1.astype(BF16), w2.astype(BF16), a16


def _s5_kernel(nb, u_ref, w1_ref, w2_ref, a_ref, y_ref, s_scr, x_scr):
    for gi in range(S5_GB):
        s_scr[gi] = _dot(u_ref[gi], w1_ref[gi])

    def step(i, carry):
        cf = i
        cb = jnp.where(i < S5_CTX_CH, S5_CTX_CH - 1 - i, S5_NCH + S5_CTX_CH - 1 - i)
        rows_f = pl.ds(pl.multiple_of(cf * nb, nb), nb)
        rows_b = pl.ds(pl.multiple_of(cb * nb, nb), nb)
        new = []
        for gi in range(S5_GB):
            fr, fi, br, bi = carry[gi]
            x_scr[gi, rows_f, 0:128] = fr
            x_scr[gi, rows_f, 128:256] = fi
            x_scr[gi, rows_b, 256:384] = br
            x_scr[gi, rows_b, 384:512] = bi
            ar, ai = a_ref[gi, 0:1, :], a_ref[gi, 1:2, :]
            nfr = ar * fr - ai * fi + s_scr[gi, rows_f, 256:384]
            nfi = ar * fi + ai * fr + s_scr[gi, rows_f, 384:512]
            ar, ai = a_ref[gi, 2:3, :], a_ref[gi, 3:4, :]
            nbr = ar * br - ai * bi + s_scr[gi, rows_b, 512:640]
            nbi = ar * bi + ai * br + s_scr[gi, rows_b, 640:768]
            new.append((nfr, nfi, nbr, nbi))
        return tuple(new)

    zero = jnp.zeros((nb, 128), F32)
    lax.fori_loop(0, S5_NCH, step, tuple((zero, zero, zero, zero) for _ in range(S5_GB)))
    for gi in range(S5_GB):
        y_ref[gi] = s_scr[gi, :, 0:256] + _dot(x_scr[gi].astype(BF16), w2_ref[gi])


def _s5_scan(u, w1, w2, a16, nb):
    rows = S5_NCH * nb
    kc = S5_CHUNK * S5_GROUP
    blk = lambda shape: pl.BlockSpec((S5_GB,) + shape, lambda g: (g, 0, 0))
    return pl.pallas_call(
        functools.partial(_s5_kernel, nb),
        grid=(S5_GROUPS // S5_GB,),
        in_specs=[blk((rows, kc)), blk((kc, 768)), blk((512, kc)), blk((4, 128))],
        out_specs=blk((rows, kc)),
        out_shape=jax.ShapeDtypeStruct((S5_GROUPS, rows, kc), F32),
        scratch_shapes=[pltpu.VMEM((S5_GB, rows, 768), F32), pltpu.VMEM((S5_GB, rows, 512), F32)],
        compiler_params=_cparams(("parallel",)),
    )(u, w1, w2, a16)


def _gelu(x):
    return 0.5 * x * (1.0 + jnp.tanh(math.sqrt(2.0 / math.pi) * (x + 0.044715 * (x * x * x))))


def _glu_kernel(y_ref, h_ref, d_ref, w_ref, b_ref, o_ref):
    z = _gelu(y_ref[...] + d_ref[...] * h_ref[...]).astype(BF16)
    r = _dot(z, w_ref[...]) + b_ref[...]
    o_ref[...] = r[:, :D] * jax.nn.sigmoid(r[:, D:])


def _glu(y, h, d_skip, w, b):
    t = y.shape[0]
    tok = pl.BlockSpec((LIN_TM, D), lambda i: (i, 0))
    return pl.pallas_call(
        _glu_kernel,
        grid=(t // LIN_TM,),
        in_specs=[tok, tok, pl.BlockSpec((1, D), lambda i: (0, 0)),
                  pl.BlockSpec((D, 2 * D), lambda i: (0, 0)), pl.BlockSpec((1, 2 * D), lambda i: (0, 0))],
        out_specs=tok,
        out_shape=jax.ShapeDtypeStruct((t, D), F32),
        compiler_params=_cparams(("parallel",)),
    )(y, h, d_skip.reshape(1, D), w, b.reshape(1, 2 * D))


def _s5_mixer(h, nb, lam_re, lam_im, log_step, b_re, b_im, c_re, c_im, d_skip, w_glu, b_glu):
    w1, w2, a16 = _s5_operators(lam_re, lam_im, log_step, b_re, b_im, c_re, c_im)
    u = h.astype(BF16).reshape(nb, S5_NCH, S5_CHUNK, S5_GROUPS, S5_GROUP).transpose(3, 1, 0, 2, 4)
    y = _s5_scan(u.reshape(S5_GROUPS, S5_NCH * nb, S5_CHUNK * S5_GROUP), w1, w2, a16, nb)
    y = y.reshape(S5_GROUPS, S5_NCH, nb, S5_CHUNK, S5_GROUP).transpose(2, 1, 3, 0, 4).reshape(nb * NTOK, D)
    return _glu(y, h, d_skip, w_glu.astype(BF16), b_glu)


def _rope_tables():
    rows = SEQ // GRID_W
    row = jnp.repeat(jnp.arange(rows), GRID_W).astype(F32)
    col = jnp.tile(jnp.arange(GRID_W), rows).astype(F32)
    half = HEAD_DIM // 4
    inv_freq = ROPE_THETA ** (-jnp.arange(half, dtype=F32) / half)
    ang_r, ang_c = row[:, None] * inv_freq, col[:, None] * inv_freq
    cos = jnp.concatenate([jnp.cos(ang_r)] * 2 + [jnp.cos(ang_c)] * 2, axis=1)
    sin = jnp.concatenate([-jnp.sin(ang_r), jnp.sin(ang_r), -jnp.sin(ang_c), jnp.sin(ang_c)], axis=1)
    cos = jnp.concatenate([jnp.ones((CTX, HEAD_DIM), F32), cos], axis=0)
    sin = jnp.concatenate([jnp.zeros((CTX, HEAD_DIM), F32), sin], axis=0)
    return jnp.tile(cos, (1, 2)), jnp.tile(sin, (1, 2))


def _qk_prep_kernel(p_ref, cos_ref, sin_ref, qg_ref, kg_ref, q_ref, k_ref, v_ref):
    lane = lax.broadcasted_iota(jnp.int32, (TOK, 128), 1)
    low_head = lane < HEAD_DIM
    first_half = (lane % 32) < 16
    cos, sin = cos_ref[...], sin_ref[...]

    def norm_rope(x, gain):
        sq = x * x
        s_lo = jnp.sum(jnp.where(low_head, sq, 0.0), axis=1, keepdims=True)
        s_hi = jnp.sum(sq, axis=1, keepdims=True) - s_lo
        inv = lax.rsqrt(jnp.where(low_head, s_lo, s_hi) * (1.0 / HEAD_DIM) + EPS)
        xn = x * inv * gain
        partner = jnp.where(first_half, pltpu.roll(xn, 128 - 16, 1), pltpu.roll(xn, 16, 1))
        return xn * cos + partner * sin

    ones_col = (lax.broadcasted_iota(jnp.int32, (TOK, HEAD_DIM), 1) == 0).astype(BF16)
    for j in range(KV_COLS // 2 // 128):
        kt = norm_rope(p_ref[:, 128 * j:128 * (j + 1)], kg_ref[...]).astype(BF16)
        vt = p_ref[:, KV_COLS // 2 + 128 * j:KV_COLS // 2 + 128 * (j + 1)].astype(BF16)
        for hh in range(2):
            k_ref[0, 2 * j + hh] = kt[:, HEAD_DIM * hh:HEAD_DIM * (hh + 1)]
            v_ref[0, 2 * j + hh] = jnp.concatenate([vt[:, HEAD_DIM * hh:HEAD_DIM * (hh + 1)], ones_col], axis=1)
    for j in range(D // 128):
        qt = norm_rope(p_ref[:, KV_COLS + 128 * j:KV_COLS + 128 * (j + 1)], qg_ref[...])
        q_ref[:, 128 * j:128 * (j + 1)] = (qt * (HEAD_DIM ** -0.5 * math.log2(math.e))).astype(BF16)


def _qk_prep(proj, q_gain, k_gain, nb):
    t, ncol = proj.shape
    cos, sin = _rope_tables()
    tab = pl.BlockSpec((TOK, 128), lambda i: (i % TILES_PER_B, 0))
    gain = pl.BlockSpec((1, 128), lambda i: (0, 0))
    head_major = lambda width: pl.BlockSpec((1, N_KV_HEADS, TOK, width),
                                            lambda i: (i // TILES_PER_B, 0, i % TILES_PER_B, 0))
    hm_shape = lambda width: jax.ShapeDtypeStruct((nb, N_KV_HEADS, NTOK, width), BF16)
    return pl.pallas_call(
        _qk_prep_kernel,
        grid=(t // TOK,),
        in_specs=[pl.BlockSpec((TOK, ncol), lambda i: (i, 0)), tab, tab, gain, gain],
        out_specs=[pl.BlockSpec((TOK, D), lambda i: (i, 0)), head_major(HEAD_DIM), head_major(2 * HEAD_DIM)],
        out_shape=[jax.ShapeDtypeStruct((t, D), BF16), hm_shape(HEAD_DIM), hm_shape(2 * HEAD_DIM)],
        compiler_params=_cparams(("parallel",)),
    )(proj, cos, sin, jnp.tile(q_gain, 2).reshape(1, 128), jnp.tile(k_gain, 2).reshape(1, 128))


def _attn_kernel(q_ref, k_ref, v_ref, o_ref):
    def attend(n_keys):
        outs = []
        for kh in range(N_KV_HEADS):
            qs = jnp.concatenate(
                [q_ref[0, :, HEAD_DIM * (Q_PER_KV * kh + g):HEAD_DIM * (Q_PER_KV * kh + g + 1)]
                 for g in range(Q_PER_KV)], axis=0)
            s = _nt_dot(qs, k_ref[0, kh, :n_keys, :])
            p = jnp.exp2(s - jnp.max(s, axis=1, keepdims=True))
            ov = _dot(p.astype(BF16), v_ref[0, kh, :n_keys, :])
            o = ov[:, :HEAD_DIM] / ov[:, HEAD_DIM:HEAD_DIM + 1]
            outs += [o[ATT_TQ * g:ATT_TQ * (g + 1)] for g in range(Q_PER_KV)]
        o_ref[0] = jnp.concatenate(outs, axis=1).astype(BF16)

    is_ctx = pl.program_id(1) < CTX // ATT_TQ

    @pl.when(is_ctx)
    def _():
        attend(CTX)

    @pl.when(jnp.logical_not(is_ctx))
    def _():
        attend(NTOK)


def _attention(q, k, v, nb):
    kv = lambda width: pl.BlockSpec((1, N_KV_HEADS, NTOK, width), lambda b, i: (b, 0, 0, 0))
    qo = pl.BlockSpec((1, ATT_TQ, D), lambda b, i: (b, i, 0))
    return pl.pallas_call(
        _attn_kernel,
        grid=(nb, NTOK // ATT_TQ),
        in_specs=[qo, kv(HEAD_DIM), kv(2 * HEAD_DIM)], out_specs=qo,
        out_shape=jax.ShapeDtypeStruct((nb, NTOK, D), BF16),
        compiler_params=_cparams(("parallel", "parallel")),
    )(q.reshape(nb, NTOK, D), k, v)


def _gqa_mixer(h, nb, w_kvq, q_gain, k_gain, w_o):
    proj = _linear(h, w_kvq.astype(BF16), F32)
    q, k, v = _qk_prep(proj, q_gain, k_gain, nb)
    o = _attention(q, k, v, nb)
    return _linear(o.reshape(nb * NTOK, D), w_o.astype(BF16), F32)


def _split3(x):
    hi = x.astype(BF16)
    r = x - hi.astype(F32)
    mid = r.astype(BF16)
    lo = (r - mid.astype(F32)).astype(BF16)
    return hi, mid, lo


def _log_sigmoid(x):
    return jnp.minimum(x, 0.0) - jnp.log1p(jnp.exp(-jnp.abs(x)))


def _mlstm_kernel(h_ref, k_ref, v_ref, q_ref, wg_ref, wgt_ref, bc_ref, br_ref, o_ref,
                  col_scr, row_scr, st_scr):
    hb = h_ref[0]
    g_col = _dot(hb, wg_ref[0]) + bc_ref[0]
    g_row = _nt_dot(wgt_ref[0], hb) + br_ref[0]
    ii = lax.broadcasted_iota(jnp.int32, (ML_CHUNK, ML_CHUNK), 0)
    jj = lax.broadcasted_iota(jnp.int32, (ML_CHUNK, ML_CHUNK), 1)
    low = ii >= jj
    upp = ii <= jj
    tri_l, tri_u = low.astype(BF16), upp.astype(BF16)
    lane_c = lax.broadcasted_iota(jnp.int32, (ML_CHUNK, 8), 1)
    sub_r = lax.broadcasted_iota(jnp.int32, (8, ML_CHUNK), 0)
    for c in range(ML_NCH):
        sl = slice(ML_CHUNK * c, ML_CHUNK * (c + 1))
        gc = g_col[sl]
        parts = _split3(jnp.where(lane_c % 2 == 1, _log_sigmoid(gc), 0.0))
        cf = sum(_dot(tri_l, p) for p in parts)
        cb = sum(_dot(tri_u, p) for p in parts)
        cum = jnp.where((lane_c // 2) % 2 == 1, cb, cf)
        col_scr[sl, :] = jnp.where(lane_c % 2 == 1, cum, gc)
        gr = g_row[:, sl]
        parts = _split3(jnp.where(sub_r % 2 == 1, _log_sigmoid(gr), 0.0))
        rf = sum(_dot(p, tri_u) for p in parts)
        rb = sum(_dot(p, tri_l) for p in parts)
        cum = jnp.where((sub_r // 2) % 2 == 1, rb, rf)
        row_scr[c] = jnp.where(sub_r % 2 == 1, cum, gr)

    o_ref[...] = jnp.zeros(o_ref.shape, F32)
    st_scr[...] = jnp.zeros(st_scr.shape, F32)
    ones_col = (lax.broadcasted_iota(jnp.int32, (ML_CHUNK, ML_DV), 1) == 0).astype(BF16)

    def step(i, ms):
        new_ms = []
        for hl in range(2):
            for d in range(2):
                idx = 2 * hl + d
                if d == 0:
                    c = i
                else:
                    c = jnp.where(i < ML_CTX_CH, ML_CTX_CH - 1 - i, ML_NCH + ML_CTX_CH - 1 - i)
                rows = pl.ds(pl.multiple_of(c * ML_CHUNK, ML_CHUNK), ML_CHUNK)
                q = q_ref[0, rows, ML_DQK * hl:ML_DQK * (hl + 1)]
                ks = k_ref[0, rows, ML_DQK * hl:ML_DQK * (hl + 1)] * (ML_DQK ** -0.5)
                v_aug = jnp.concatenate([v_ref[0, rows, ML_DV * hl:ML_DV * (hl + 1)], ones_col], axis=1)
                ig_c = col_scr[rows, 2 * idx:2 * idx + 1]
                cum_c = col_scr[rows, 2 * idx + 1:2 * idx + 2]
                rr = row_scr[c]
                ig_r, cum_r = rr[2 * idx:2 * idx + 1, :], rr[2 * idx + 1:2 * idx + 2, :]
                g = cum_r[:, ML_CHUNK - 1:ML_CHUNK] if d == 0 else cum_r[:, 0:1]
                m0 = ms[idx]
                c_aug = st_scr[idx]

                log_d = jnp.where(low if d == 0 else upp, cum_c - cum_r + ig_r, -jnp.inf)
                log_inter = cum_c + m0
                m_row = jnp.maximum(log_inter, jnp.max(log_d, axis=1, keepdims=True))
                s = _nt_dot(q, ks) * jnp.exp(log_d - m_row)
                a_inter = jnp.exp(log_inter - m_row)
                nd = a_inter * _dot(q, c_aug.astype(BF16)) + _dot(s.astype(BF16), v_aug)
                den = jnp.maximum(jnp.abs(nd[:, ML_DV:ML_DV + 1]), jnp.exp(-m_row))
                o_ref[0, rows, ML_DV * hl:ML_DV * (hl + 1)] += nd[:, :ML_DV] / den

                w = g - cum_c + ig_c
                m_loc = jnp.max(w, axis=0, keepdims=True)
                ke = (ks.astype(F32) * jnp.exp(w - m_loc)).astype(BF16)
                c_loc = lax.dot_general(ke, v_aug, (((0,), (0,)), ((), ())), preferred_element_type=F32)
                m_new = jnp.maximum(g + m0, m_loc)
                st_scr[idx] = jnp.exp(g + m0 - m_new) * c_aug + jnp.exp(m_loc - m_new) * c_loc
                new_ms.append(m_new)
        return tuple(new_ms)

    lax.fori_loop(0, ML_NCH, step, tuple(jnp.zeros((1, 1), F32) for _ in range(4)))


def _mlstm_scan(h, proj, wg, wgt, b_col, b_row, nb):
    pairs = ML_HEADS // 2
    k_blk = pl.BlockSpec((1, NTOK, 2 * ML_DQK), lambda b, p: (b, 0, p))
    v_blk = pl.BlockSpec((1, NTOK, 2 * ML_DV), lambda b, p: (b, 0, ML_QK // (2 * ML_DV) + p))
    q_blk = pl.BlockSpec((1, NTOK, 2 * ML_DQK), lambda b, p: (b, 0, (ML_QK + D) // (2 * ML_DQK) + p))
    per_pair = lambda shape: pl.BlockSpec((1,) + shape, lambda b, p: (p, 0, 0))
    proj3 = proj.reshape(nb, NTOK, proj.shape[-1])
    return pl.pallas_call(
        _mlstm_kernel,
        grid=(nb, pairs),
        in_specs=[pl.BlockSpec((1, NTOK, D), lambda b, p: (b, 0, 0)), k_blk, v_blk, q_blk,
                  per_pair((D, 8)), per_pair((8, D)), per_pair((1, 8)), per_pair((8, 1))],
        out_specs=pl.BlockSpec((1, NTOK, 2 * ML_DV), lambda b, p: (b, 0, p)),
        out_shape=jax.ShapeDtypeStruct((nb, NTOK, D), F32),
        scratch_shapes=[pltpu.VMEM((NTOK, 8), F32), pltpu.VMEM((ML_NCH, 8, ML_CHUNK), F32),
                        pltpu.VMEM((4, ML_DQK, 2 * ML_DV), F32)],
        compiler_params=_cparams(("parallel", "parallel")),
    )(h.reshape(nb, NTOK, D), proj3, proj3, proj3, wg, wgt, b_col, b_row)


def _ml_finish_kernel(hh_ref, o_ref, g_ref, w_ref, y_ref):
    parts = [_rms(hh_ref[:, ML_DV * i:ML_DV * (i + 1)]) for i in range(ML_HEADS)]
    hn = jnp.concatenate(parts, axis=1) * g_ref[...]
    z = (hn * jax.nn.sigmoid(o_ref[...].astype(F32))).astype(BF16)
    y_ref[...] = _dot(z, w_ref[...])


def _ml_finish(hh, proj, norm_g, w_out):
    t = hh.shape[0]
    tok = pl.BlockSpec((LIN_TM, D), lambda i: (i, 0))
    return pl.pallas_call(
        _ml_finish_kernel,
        grid=(t // LIN_TM,),
        in_specs=[tok, pl.BlockSpec((LIN_TM, D), lambda i: (i, (ML_QK + D + ML_QK) // D)),
                  pl.BlockSpec((1, D), lambda i: (0, 0)), pl.BlockSpec((D, D), lambda i: (0, 0))],
        out_specs=tok,
        out_shape=jax.ShapeDtypeStruct((t, D), F32),
        compiler_params=_cparams(("parallel",)),
    )(hh, proj, norm_g.reshape(1, D), w_out)


def _mlstm_mixer(h, nb, w_in, b_gate, norm_g, w_out):
    n_state = ML_QK + D
    n_gate = 4 * ML_HEADS
    w_main = jnp.concatenate([w_in[:, :n_state], w_in[:, n_state + n_gate:]], axis=1).astype(BF16)
    pairs = ML_HEADS // 2
    wg = w_in[:, n_state:n_state + n_gate].reshape(D, 2, 2, pairs, 2).transpose(3, 0, 4, 1, 2)
    wg = wg.reshape(pairs, D, 8).astype(BF16)
    bg = b_gate.reshape(2, 2, pairs, 2).transpose(2, 3, 0, 1).reshape(pairs, 8)
    proj = _linear(h, w_main, BF16)
    hh = _mlstm_scan(h, proj, wg, wg.transpose(0, 2, 1), bg[:, None, :], bg[:, :, None], nb)
    return _ml_finish(hh.reshape(nb * NTOK, D), proj, norm_g, w_out.astype(BF16))


PAIRS = tuple((a, b) for a in range(4) for b in range(a + 1, 4))
N_CAT = N_GROUPS * len(PAIRS)


def _route_tables(ridx, rw, t):
    n_tiles = t // MOE_TM + N_CAT
    n_rows = n_tiles * MOE_TM
    lo, hi = ridx[0], ridx[1]
    a, b = lo % 4, hi % 4
    cat = (lo // 4) * len(PAIRS) + (a * (7 - a)) // 2 + (b - a - 1)
    onehot = (cat[:, None] == jnp.arange(N_CAT)[None, :]).astype(jnp.int32)
    rank = jnp.take_along_axis(jnp.cumsum(onehot, axis=0) - onehot, cat[:, None], axis=1)[:, 0]
    counts = jnp.sum(onehot, axis=0)
    padded = ((counts + MOE_TM - 1) // MOE_TM) * MOE_TM
    ends = jnp.cumsum(padded)
    pos = (ends - padded)[cat] + rank
    tok_ids = jnp.arange(t, dtype=jnp.int32)
    src = jnp.zeros((n_rows,), jnp.int32).at[pos].set(tok_ids)
    w = jnp.zeros((n_rows, 2), F32).at[pos].set(rw.T)
    n_used = ends[-1] // MOE_TM
    tile_start = jnp.minimum(jnp.arange(n_tiles), n_used - 1) * MOE_TM
    tile_cat = jnp.sum((ends[None, :] <= tile_start[:, None]).astype(jnp.int32), axis=1)
    n_valid = jnp.clip((ends - padded + counts)[tile_cat] - tile_start, 0, MOE_TM)
    pa = jnp.array([p[0] for p in PAIRS], jnp.int32)[tile_cat % len(PAIRS)]
    pb = jnp.array([p[1] for p in PAIRS], jnp.int32)[tile_cat % len(PAIRS)]
    base = (tile_cat // len(PAIRS)) * 4
    i32 = lambda a: a.astype(jnp.int32)
    return i32(base + pa), i32(base + pb), i32(n_used).reshape(1), i32(n_valid), src, w


def _moe_kernel(ea_ref, eb_ref, nu_ref, nv_ref, tok_ref,
                h_hbm, w_ref, wga_ref, wua_ref, wda_ref, wgb_ref, wub_ref, wdb_ref, y_hbm,
                xbuf, ybuf, bin_scr, gsem, ssem):
    i = pl.program_id(0)
    n_used = nu_ref[0]
    slot = i % 2

    def issue_gather(tile, sl):
        def body(r, carry):
            tok = tok_ref[tile * MOE_TM + r]
            pltpu.make_async_copy(h_hbm.at[pl.ds(tok, 1)], xbuf.at[sl, pl.ds(r, 1)], gsem.at[sl]).start()
            return carry
        lax.fori_loop(0, MOE_TM, body, 0)

    def wait_gather(sl):
        pltpu.make_async_copy(h_hbm.at[pl.ds(0, MOE_TM)], xbuf.at[sl], gsem.at[sl]).wait()

    def issue_scatter(tile, sl):
        def to_hbm(r, carry):
            tok = tok_ref[tile * MOE_TM + r]
            pltpu.make_async_copy(ybuf.at[sl, pl.ds(r, 1)], y_hbm.at[pl.ds(tok, 1)], ssem.at[sl]).start()
            return carry

        def to_bin(r, carry):
            pltpu.make_async_copy(ybuf.at[sl, pl.ds(r, 1)], bin_scr.at[sl, pl.ds(r, 1)], ssem.at[sl]).start()
            return carry
        lax.fori_loop(0, nv_ref[tile], to_hbm, 0)
        lax.fori_loop(nv_ref[tile], MOE_TM, to_bin, 0)

    def wait_scatter(sl):
        pltpu.make_async_copy(ybuf.at[sl], y_hbm.at[pl.ds(0, MOE_TM)], ssem.at[sl]).wait()

    @pl.when(i == 0)
    def _():
        issue_gather(0, 0)

    @pl.when(i + 1 < n_used)
    def _():
        issue_gather(i + 1, 1 - slot)

    @pl.when(i < n_used)
    def _():
        wait_gather(slot)

        @pl.when(i >= 2)
        def _():
            wait_scatter(slot)

        x = xbuf[slot].astype(BF16)
        w = w_ref[...]

        def expert(wg_ref, wu_ref, wd_ref):
            a = _dot(x, wg_ref[0])
            hid = (a * jax.nn.sigmoid(a) * _dot(x, wu_ref[0])).astype(BF16)
            return _dot(hid, wd_ref[0])

        ybuf[slot] = (w[:, 0:1] * expert(wga_ref, wua_ref, wda_ref)
                      + w[:, 1:2] * expert(wgb_ref, wub_ref, wdb_ref))
        issue_scatter(i, slot)

    @pl.when(i == pl.num_programs(0) - 1)
    def _():
        wait_scatter(0)
        wait_scatter(1)


def _moe(h2, ridx, rw, w_gate, w_up, w_down):
    t = h2.shape[0]
    ea, eb, n_used, n_valid, src, w = _route_tables(ridx, rw, t)
    n_tiles = ea.shape[0]
    assert t // MOE_TM >= 2
    wspec = lambda which, shape: pl.BlockSpec(
        (1,) + shape, (lambda i, ea, eb, nu, nv, tok: (ea[i], 0, 0)) if which == 0
        else (lambda i, ea, eb, nu, nv, tok: (eb[i], 0, 0)))
    wg, wu, wd = w_gate.astype(BF16), w_up.astype(BF16), w_down.astype(BF16)
    return pl.pallas_call(
        _moe_kernel,
        grid_spec=pltpu.PrefetchScalarGridSpec(
            num_scalar_prefetch=5,
            grid=(n_tiles,),
            in_specs=[pl.BlockSpec(memory_space=pl.ANY),
                      pl.BlockSpec((MOE_TM, 2), lambda i, *_: (i, 0)),
                      wspec(0, (D, D_FF)), wspec(0, (D, D_FF)), wspec(0, (D_FF, D)),
                      wspec(1, (D, D_FF)), wspec(1, (D, D_FF)), wspec(1, (D_FF, D))],
            out_specs=pl.BlockSpec(memory_space=pl.ANY),
            scratch_shapes=[pltpu.VMEM((2, MOE_TM, D), F32), pltpu.VMEM((2, MOE_TM, D), F32),
                            pltpu.VMEM((2, MOE_TM, D), F32),
                            pltpu.SemaphoreType.DMA((2,)), pltpu.SemaphoreType.DMA((2,))]),
        out_shape=jax.ShapeDtypeStruct((t, D), F32),
        compiler_params=_cparams(("arbitrary",)),
    )(ea, eb, n_used, n_valid, src, h2, w, wg, wu, wd, wg, wu, wd)


def kernel(x, c, ctx, c_ctx, ada_w, ada_b, norm1_g, norm2_g, final_g, s5_lambda_re, s5_lambda_im, s5_log_step, s5_b_re, s5_b_im, s5_c_re, s5_c_im, s5_d, s5_w_glu, s5_b_glu, attn_w_kvq, attn_q_gain, attn_k_gain, attn_w_o, ml_w_in, ml_b_gate, ml_norm_g, ml_w_out, router_w, router_bias, moe_w_gate, moe_w_up, moe_w_down):
    nb = x.shape[0]
    depth = ada_w.shape[0]
    assert x.shape[1:] == (SEQ, D) and ctx.shape[1:] == (CTX, D) and nb + 1 <= 16
    xs = jnp.concatenate([ctx, x], axis=1).reshape(nb * NTOK, D)
    cond_in = jnp.zeros((16, D), F32).at[:nb].set(c).at[nb].set(c_ctx)
    mods = _ada(cond_in, ada_w, ada_b).reshape(depth, 16, 6, D)

    y2 = None
    for layer in range(depth):
        kind, j = layer % N_MIXERS, layer // N_MIXERS
        xs, h = _pre(xs, y2, mods[layer - 1] if layer else None, mods[layer], norm1_g[layer], nb,
                     F32 if kind == 0 else BF16)
        if kind == 0:
            y = _s5_mixer(h, nb, s5_lambda_re[j], s5_lambda_im[j], s5_log_step[j], s5_b_re[j], s5_b_im[j],
                          s5_c_re[j], s5_c_im[j], s5_d[j], s5_w_glu[j], s5_b_glu[j])
        elif kind == 1:
            y = _gqa_mixer(h, nb, attn_w_kvq[j], attn_q_gain[j], attn_k_gain[j], attn_w_o[j])
        else:
            y = _mlstm_mixer(h, nb, ml_w_in[j], ml_b_gate[j], ml_norm_g[j], ml_w_out[j])
        xs, h2, ridx, rw = _post(xs, y, mods[layer], norm2_g[layer], router_w, router_bias, nb)
        y2 = _moe(h2, ridx, rw, moe_w_gate[layer], moe_w_up[layer], moe_w_down[layer])
    out = _final(xs, y2, mods[depth - 1], final_g, nb)
    return out.reshape(nb, SEQ, D)
```

```python
import functools
import math

import jax
import jax.numpy as jnp
import numpy as np
from jax import lax
from jax.experimental import pallas as pl
from jax.experimental.pallas import tpu as pltpu

F32 = jnp.float32
BF16 = jnp.bfloat16

D = 1024
CTX = 256
SEQ = 2048
NTOK = CTX + SEQ
GRID_W = 64
EPS = 1e-6
N_MIXERS = 3

TOK = 256
TILES_PER_B = NTOK // TOK
LIN_TM = 512

S5_GROUP = 16
S5_GROUPS = D // S5_GROUP
S5_STATE = 64
S5_CHUNK = 16
S5_NCH = NTOK // S5_CHUNK
S5_CTX_CH = CTX // S5_CHUNK
S5_SLAB_GROUPS = 128 // S5_GROUP

HEAD_DIM = 64
N_Q_HEADS = 16
N_KV_HEADS = 4
Q_PER_KV = 4
KV_COLS = 2 * N_KV_HEADS * HEAD_DIM
ROPE_THETA = 10000.0
ATT_TQ = 128

ML_HEADS = 8
ML_DQK = 64
ML_DV = 128
ML_QK = 512
ML_CHUNK = 128
ML_NCH = NTOK // ML_CHUNK
ML_CTX_CH = CTX // ML_CHUNK

N_EXPERTS = 16
N_GROUPS = 4
D_FF = 512
MOE_TM = 256

VMEM_LIMIT = 56 * 1024 * 1024


def _cparams(sem):
    return pltpu.CompilerParams(dimension_semantics=sem, vmem_limit_bytes=VMEM_LIMIT)


def _nt_dot(a, b, **kw):
    return lax.dot_general(a, b, (((1,), (1,)), ((), ())), preferred_element_type=F32, **kw)


def _dot(a, b):
    return jnp.dot(a, b, preferred_element_type=F32)


def _rms(x):
    return x * lax.rsqrt(jnp.mean(x * x, axis=-1, keepdims=True) + EPS)


def _ada_kernel(c_ref, w_ref, b_ref, o_ref):
    c = c_ref[...]
    cond = (c * jax.nn.sigmoid(c)).astype(BF16)
    o_ref[0] = _dot(cond, w_ref[0].astype(BF16)) + b_ref[0]


def _ada(cond_in, ada_w, ada_b):
    depth, _, n = ada_w.shape
    tn = 1536
    rows = cond_in.shape[0]
    return pl.pallas_call(
        _ada_kernel,
        grid=(depth, n // tn),
        in_specs=[pl.BlockSpec((rows, D), lambda l, j: (0, 0)),
                  pl.BlockSpec((1, D, tn), lambda l, j: (l, 0, j)),
                  pl.BlockSpec((1, 1, tn), lambda l, j: (l, 0, j))],
        out_specs=pl.BlockSpec((1, rows, tn), lambda l, j: (l, 0, j)),
        out_shape=jax.ShapeDtypeStruct((depth, rows, n), F32),
        compiler_params=_cparams(("parallel", "parallel")),
    )(cond_in, ada_w, ada_b.reshape(depth, 1, n))


def _mod_row(nb):
    return lambda i: (jnp.where(i % TILES_PER_B == 0, nb, i // TILES_PER_B), 0, 0)


def _pre_kernel(has_res, *refs):
    if has_res:
        x_ref, y_ref, mprev_ref, mcur_ref, g_ref, xo_ref, h_ref = refs
        x = x_ref[...] + mprev_ref[0, 5:6, :] * y_ref[...]
        xo_ref[...] = x
    else:
        x_ref, mcur_ref, g_ref, h_ref = refs
        x = x_ref[...]
    h = _rms(x) * g_ref[...]
    h = h * (1.0 + mcur_ref[0, 1:2, :]) + mcur_ref[0, 0:1, :]
    h_ref[...] = h.astype(h_ref.dtype).reshape(h_ref.shape)


def _chunk_major(nb):
    shape = (S5_NCH, nb, S5_CHUNK, D)
    spec = pl.BlockSpec((TOK // S5_CHUNK, 1, S5_CHUNK, D), lambda i: (i % TILES_PER_B, i // TILES_PER_B, 0, 0))
    return shape, spec


def _pre(x, y2, mods_prev, mods_cur, g, nb, h_dtype, chunk_major=False):
    t = x.shape[0]
    tok = pl.BlockSpec((TOK, D), lambda i: (i, 0))
    mod = pl.BlockSpec((1, 6, D), _mod_row(nb))
    vec = pl.BlockSpec((1, D), lambda i: (0, 0))
    h_shape, h_spec = _chunk_major(nb) if chunk_major else ((t, D), tok)
    if y2 is None:
        return x, pl.pallas_call(
            functools.partial(_pre_kernel, False),
            grid=(t // TOK,), in_specs=[tok, mod, vec], out_specs=h_spec,
            out_shape=jax.ShapeDtypeStruct(h_shape, h_dtype),
            compiler_params=_cparams(("parallel",)),
        )(x, mods_cur, g.reshape(1, D))
    return pl.pallas_call(
        functools.partial(_pre_kernel, True),
        grid=(t // TOK,), in_specs=[tok, tok, mod, mod, vec], out_specs=[tok, h_spec],
        out_shape=[jax.ShapeDtypeStruct((t, D), F32), jax.ShapeDtypeStruct(h_shape, h_dtype)],
        compiler_params=_cparams(("parallel",)),
    )(x, y2, mods_prev, mods_cur, g.reshape(1, D))


def _post_kernel(x_ref, y_ref, m_ref, g_ref, rwt_ref, rb_ref, xo_ref, h_ref, ridx_ref, rw_ref):
    x = x_ref[...] + m_ref[0, 2:3, :] * y_ref[...].reshape(x_ref.shape)
    xo_ref[...] = x
    h2 = _rms(x) * g_ref[...]
    h2 = h2 * (1.0 + m_ref[0, 4:5, :]) + m_ref[0, 3:4, :]
    h_ref[...] = h2

    logits = _nt_dot(rwt_ref[...], h2, precision=lax.Precision.HIGHEST)
    scores = jax.nn.sigmoid(logits)
    sel = scores + rb_ref[...]
    row = lax.broadcasted_iota(jnp.int32, sel.shape, 0)
    per_group = N_EXPERTS // N_GROUPS
    best_val, best = None, None
    for g in range(N_GROUPS):
        r = [sel[per_group * g + j:per_group * g + j + 1, :] for j in range(per_group)]
        gs = None
        for i in range(per_group):
            for j in range(i + 1, per_group):
                gs = r[i] + r[j] if gs is None else jnp.maximum(gs, r[i] + r[j])
        if g == 0:
            best_val, best = gs, jnp.zeros(gs.shape, jnp.int32)
        else:
            upd = gs > best_val
            best = jnp.where(upd, g, best)
            best_val = jnp.where(upd, gs, best_val)
    masked = jnp.where((row // per_group) == best, sel, -jnp.inf)
    m1 = jnp.max(masked, axis=0, keepdims=True)
    i1 = jnp.min(jnp.where(masked == m1, row, N_EXPERTS), axis=0, keepdims=True)
    masked2 = jnp.where(row == i1, -jnp.inf, masked)
    m2 = jnp.max(masked2, axis=0, keepdims=True)
    i2 = jnp.min(jnp.where(masked2 == m2, row, N_EXPERTS), axis=0, keepdims=True)
    lo, hi = jnp.minimum(i1, i2), jnp.maximum(i1, i2)
    s_lo = jnp.sum(jnp.where(row == lo, scores, 0.0), axis=0, keepdims=True)
    s_hi = jnp.sum(jnp.where(row == hi, scores, 0.0), axis=0, keepdims=True)
    ridx_ref[...] = jnp.concatenate([lo, hi], axis=0)
    rw_ref[...] = jnp.concatenate([s_lo, s_hi], axis=0) / (s_lo + s_hi)


def _post(x, y, mods, g, router_w, router_bias, nb):
    t = x.shape[0]
    tok = pl.BlockSpec((TOK, D), lambda i: (i, 0))
    route = pl.BlockSpec((2, TOK), lambda i: (0, i))
    return pl.pallas_call(
        _post_kernel,
        grid=(t // TOK,),
        in_specs=[tok, _chunk_major(nb)[1] if y.ndim == 4 else tok, pl.BlockSpec((1, 6, D), _mod_row(nb)),
                  pl.BlockSpec((1, D), lambda i: (0, 0)),
                  pl.BlockSpec((N_EXPERTS, D), lambda i: (0, 0)),
                  pl.BlockSpec((N_EXPERTS, 1), lambda i: (0, 0))],
        out_specs=[tok, tok, route, route],
        out_shape=[jax.ShapeDtypeStruct((t, D), F32), jax.ShapeDtypeStruct((t, D), F32),
                   jax.ShapeDtypeStruct((2, t), jnp.int32), jax.ShapeDtypeStruct((2, t), F32)],
        compiler_params=_cparams(("parallel",)),
    )(x, y, mods, g.reshape(1, D), router_w.T, router_bias.reshape(N_EXPERTS, 1))


def _final_kernel(x_ref, y_ref, m_ref, g_ref, o_ref):
    x = x_ref[...] + m_ref[0, 5:6, :] * y_ref[...]
    o_ref[...] = _rms(x) * g_ref[...]


def _final(x, y2, mods, g, nb):
    lat_tiles = SEQ // TOK
    src = lambda j: ((j // lat_tiles) * TILES_PER_B + CTX // TOK + j % lat_tiles, 0)
    tok_in = pl.BlockSpec((TOK, D), src)
    return pl.pallas_call(
        _final_kernel,
        grid=(nb * lat_tiles,),
        in_specs=[tok_in, tok_in, pl.BlockSpec((1, 6, D), lambda j: (j // lat_tiles, 0, 0)),
                  pl.BlockSpec((1, D), lambda j: (0, 0))],
        out_specs=pl.BlockSpec((TOK, D), lambda j: (j, 0)),
        out_shape=jax.ShapeDtypeStruct((nb * SEQ, D), F32),
        compiler_params=_cparams(("parallel",)),
    )(x, y2, mods, g.reshape(1, D))


def _linear_kernel(x_ref, w_ref, o_ref):
    o_ref[...] = _dot(x_ref[...].astype(BF16), w_ref[...]).astype(o_ref.dtype)


def _linear(x, w, out_dtype):
    t, k = x.shape
    n = w.shape[1]
    return pl.pallas_call(
        _linear_kernel,
        grid=(t // LIN_TM,),
        in_specs=[pl.BlockSpec((LIN_TM, k), lambda i: (i, 0)), pl.BlockSpec((k, n), lambda i: (0, 0))],
        out_specs=pl.BlockSpec((LIN_TM, n), lambda i: (i, 0)),
        out_shape=jax.ShapeDtypeStruct((t, n), out_dtype),
        compiler_params=_cparams(("parallel",)),
    )(x, w)


def _s5_operators(lam_re, lam_im, log_step, b_re, b_im, c_re, c_im):
    n = S5_CHUNK
    dt = jnp.exp(log_step)[None, :, :, None]
    k = jnp.arange(n + 1, dtype=F32)[:, None, None, None]
    mag = jnp.exp(k * lam_re[None] * dt)
    ang = k * lam_im[None] * dt
    pw_re, pw_im = mag * jnp.cos(ang), mag * jnp.sin(ang)
    a_re, a_im = pw_re[1], pw_im[1]
    den = lam_re * lam_re + lam_im * lam_im
    n_re, n_im = a_re - 1.0, a_im
    f_re = (n_re * lam_re + n_im * lam_im) / den
    f_im = (n_im * lam_re - n_re * lam_im) / den
    bb_re = f_re[..., None] * b_re - f_im[..., None] * b_im
    bb_im = f_re[..., None] * b_im + f_im[..., None] * b_re

    cp_re = c_re[None] * pw_re[:, :, :, None, :] - c_im[None] * pw_im[:, :, :, None, :]
    cp_im = c_re[None] * pw_im[:, :, :, None, :] + c_im[None] * pw_re[:, :, :, None, :]
    kern = (jnp.einsum('kdgop,dgpc->kdgoc', cp_re[:n], bb_re)
            - jnp.einsum('kdgop,dgpc->kdgoc', cp_im[:n], bb_im))
    s_idx = jnp.arange(n)[:, None]
    t_idx = jnp.arange(n)[None, :]
    lag_f, lag_b = t_idx - s_idx, s_idx - t_idx
    kf = jnp.where((lag_f >= 0)[:, :, None, None, None], kern[jnp.clip(lag_f, 0, n - 1), 0], 0.0)
    kb = jnp.where((lag_b >= 0)[:, :, None, None, None], kern[jnp.clip(lag_b, 0, n - 1), 1], 0.0)
    toep = (kf + kb).transpose(2, 0, 4, 1, 3).reshape(S5_GROUPS, n * S5_GROUP, n * S5_GROUP)

    def state_in(pw_sel, d):
        pr, pi = pw_re[pw_sel, d], pw_im[pw_sel, d]
        e_re = pr[..., None] * bb_re[d][None] - pi[..., None] * bb_im[d][None]
        e_im = pr[..., None] * bb_im[d][None] + pi[..., None] * bb_re[d][None]
        to_w = lambda e: jnp.pad(e.transpose(1, 0, 3, 2).reshape(S5_GROUPS, n * S5_GROUP, S5_STATE),
                                 ((0, 0), (0, 0), (0, 128 - S5_STATE)))
        return to_w(e_re), to_w(e_im)

    def state_out(pw_sel, d):
        o_re, o_im = cp_re[pw_sel, d], cp_im[pw_sel, d]
        to_w = lambda o: jnp.pad(o.transpose(1, 3, 0, 2).reshape(S5_GROUPS, S5_STATE, n * S5_GROUP),
                                 ((0, 0), (0, 128 - S5_STATE), (0, 0)))
        return to_w(o_re), to_w(-o_im)

    ar = jnp.arange(n)
    w1 = jnp.concatenate([toep, *state_in(n - 1 - ar, 0), *state_in(ar, 1)], axis=2)
    w2 = jnp.concatenate([*state_out(ar + 1, 0), *state_out(n - ar, 1)], axis=1)
    pad = lambda a: jnp.pad(a, ((0, 0), (0, 128 - S5_STATE)))
    a16 = jnp.stack([pad(pw_re[n, 0]), pad(pw_im[n, 0]), pad(pw_re[n, 1]), pad(pw_im[n, 1])], axis=1)
    return w1.astype(BF16), w2.astype(BF16), a16


def _s5_select():
    sel = np.zeros((S5_SLAB_GROUPS, S5_CHUNK * 128, S5_CHUNK * S5_GROUP), np.float32)
    s, c = np.meshgrid(np.arange(S5_CHUNK), np.arange(S5_GROUP), indexing="ij")
    for j in range(S5_SLAB_GROUPS):
        sel[j, s * 128 + S5_GROUP * j + c, s * S5_GROUP + c] = 1.0
    return jnp.asarray(sel, BF16)


def _s5_kernel(nb, h_ref, sel_ref, w1_ref, w2_ref, a_ref, y_ref, x2_scr, s_scr, x_scr):
    rows = S5_NCH * nb

    @pl.when(pl.program_id(1) == 0)
    def _():
        for s in range(S5_CHUNK):
            x2_scr[:, 128 * s:128 * (s + 1)] = h_ref[pl.ds(s, rows, stride=S5_CHUNK), :].astype(BF16)
        y_ref[...] = jnp.zeros(y_ref.shape, F32)

    u = _dot(x2_scr[...], sel_ref[0]).astype(BF16)
    s_scr[...] = _dot(u, w1_ref[0])

    def step(i, carry):
        cf = i
        cb = jnp.where(i < S5_CTX_CH, S5_CTX_CH - 1 - i, S5_NCH + S5_CTX_CH - 1 - i)
        rows_f = pl.ds(pl.multiple_of(cf * nb, nb), nb)
        rows_b = pl.ds(pl.multiple_of(cb * nb, nb), nb)
        fr, fi, br, bi = carry
        x_scr[rows_f, 0:128] = fr
        x_scr[rows_f, 128:256] = fi
        x_scr[rows_b, 256:384] = br
        x_scr[rows_b, 384:512] = bi
        ar, ai = a_ref[0, 0:1, :], a_ref[0, 1:2, :]
        nfr = ar * fr - ai * fi + s_scr[rows_f, 256:384]
        nfi = ar * fi + ai * fr + s_scr[rows_f, 384:512]
        ar, ai = a_ref[0, 2:3, :], a_ref[0, 3:4, :]
        nbr = ar * br - ai * bi + s_scr[rows_b, 512:640]
        nbi = ar * bi + ai * br + s_scr[rows_b, 640:768]
        return nfr, nfi, nbr, nbi

    zero = jnp.zeros((nb, 128), F32)
    lax.fori_loop(0, S5_NCH, step, (zero, zero, zero, zero))
    yg = s_scr[:, 0:256] + _dot(x_scr[...].astype(BF16), w2_ref[0])
    y_ref[0] += _nt_dot(yg.astype(BF16), sel_ref[0])


def _s5_scan(h_cm, w1, w2, a16, nb):
    rows = S5_NCH * nb
    kc = S5_CHUNK * S5_GROUP
    n_slabs = D // 128
    per_group = lambda shape: pl.BlockSpec((1,) + shape, lambda q, j: (q * S5_SLAB_GROUPS + j, 0, 0))
    return pl.pallas_call(
        functools.partial(_s5_kernel, nb),
        grid=(n_slabs, S5_SLAB_GROUPS),
        in_specs=[pl.BlockSpec((nb * NTOK, 128), lambda q, j: (0, q)),
                  pl.BlockSpec((1, S5_CHUNK * 128, kc), lambda q, j: (j, 0, 0)),
                  per_group((kc, 768)), per_group((512, kc)), per_group((4, 128))],
        out_specs=pl.BlockSpec((1, rows, S5_CHUNK * 128), lambda q, j: (q, 0, 0)),
        out_shape=jax.ShapeDtypeStruct((n_slabs, rows, S5_CHUNK * 128), F32),
        scratch_shapes=[pltpu.VMEM((rows, S5_CHUNK * 128), BF16), pltpu.VMEM((rows, 768), F32),
                        pltpu.VMEM((rows, 512), F32)],
        compiler_params=_cparams(("parallel", "arbitrary")),
    )(h_cm.reshape(nb * NTOK, D), _s5_select(), w1, w2, a16)


def _gelu(x):
    return 0.5 * x * (1.0 + jnp.tanh(math.sqrt(2.0 / math.pi) * (x + 0.044715 * (x * x * x))))


def _glu_kernel(y_ref, h_ref, d_ref, w_ref, b_ref, o_ref, ys_scr):
    rows = TOK // S5_CHUNK
    for q in range(D // 128):
        for s in range(S5_CHUNK):
            ys_scr[q, pl.ds(s, rows, stride=S5_CHUNK), :] = y_ref[q, :, 128 * s:128 * (s + 1)]
    y = jnp.concatenate([ys_scr[q] for q in range(D // 128)], axis=1)
    z = _gelu(y + d_ref[...] * h_ref[...]).astype(BF16)
    r = _dot(z, w_ref[...]) + b_ref[...]
    o_ref[...] = r[:, :D] * jax.nn.sigmoid(r[:, D:])


def _glu(y, h, d_skip, w, b):
    t = h.shape[0]
    tok = pl.BlockSpec((TOK, D), lambda i: (i, 0))
    return pl.pallas_call(
        _glu_kernel,
        grid=(t // TOK,),
        in_specs=[pl.BlockSpec((D // 128, TOK // S5_CHUNK, S5_CHUNK * 128), lambda i: (0, i, 0)), tok,
                  pl.BlockSpec((1, D), lambda i: (0, 0)),
                  pl.BlockSpec((D, 2 * D), lambda i: (0, 0)), pl.BlockSpec((1, 2 * D), lambda i: (0, 0))],
        out_specs=tok,
        out_shape=jax.ShapeDtypeStruct((t, D), F32),
        scratch_shapes=[pltpu.VMEM((D // 128, TOK, 128), F32)],
        compiler_params=_cparams(("parallel",)),
    )(y, h, d_skip.reshape(1, D), w, b.reshape(1, 2 * D))


def _s5_mixer(h_cm, nb, lam_re, lam_im, log_step, b_re, b_im, c_re, c_im, d_skip, w_glu, b_glu):
    w1, w2, a16 = _s5_operators(lam_re, lam_im, log_step, b_re, b_im, c_re, c_im)
    y = _s5_scan(h_cm, w1, w2, a16, nb)
    y = _glu(y, h_cm.reshape(nb * NTOK, D), d_skip, w_glu.astype(BF16), b_glu)
    return y.reshape(h_cm.shape)


def _rope_tables():
    rows = SEQ // GRID_W
    row = jnp.repeat(jnp.arange(rows), GRID_W).astype(F32)
    col = jnp.tile(jnp.arange(GRID_W), rows).astype(F32)
    half = HEAD_DIM // 4
    inv_freq = ROPE_THETA ** (-jnp.arange(half, dtype=F32) / half)
    ang_r, ang_c = row[:, None] * inv_freq, col[:, None] * inv_freq
    cos = jnp.concatenate([jnp.cos(ang_r)] * 2 + [jnp.cos(ang_c)] * 2, axis=1)
    sin = jnp.concatenate([-jnp.sin(ang_r), jnp.sin(ang_r), -jnp.sin(ang_c), jnp.sin(ang_c)], axis=1)
    cos = jnp.concatenate([jnp.ones((CTX, HEAD_DIM), F32), cos], axis=0)
    sin = jnp.concatenate([jnp.zeros((CTX, HEAD_DIM), F32), sin], axis=0)
    return jnp.tile(cos, (1, 2)), jnp.tile(sin, (1, 2))


def _qk_prep_kernel(p_ref, cos_ref, sin_ref, qg_ref, kg_ref, q_ref, k_ref, v_ref):
    lane = lax.broadcasted_iota(jnp.int32, (TOK, 128), 1)
    low_head = lane < HEAD_DIM
    first_half = (lane % 32) < 16
    cos, sin = cos_ref[...], sin_ref[...]

    def norm_rope(x, gain):
        sq = x * x
        s_lo = jnp.sum(jnp.where(low_head, sq, 0.0), axis=1, keepdims=True)
        s_hi = jnp.sum(sq, axis=1, keepdims=True) - s_lo
        inv = lax.rsqrt(jnp.where(low_head, s_lo, s_hi) * (1.0 / HEAD_DIM) + EPS)
        xn = x * inv * gain
        partner = jnp.where(first_half, pltpu.roll(xn, 128 - 16, 1), pltpu.roll(xn, 16, 1))
        return xn * cos + partner * sin

    ones_col = (lax.broadcasted_iota(jnp.int32, (TOK, HEAD_DIM), 1) == 0).astype(BF16)
    for j in range(KV_COLS // 2 // 128):
        kt = norm_rope(p_ref[:, 128 * j:128 * (j + 1)], kg_ref[...]).astype(BF16)
        vt = p_ref[:, KV_COLS // 2 + 128 * j:KV_COLS // 2 + 128 * (j + 1)].astype(BF16)
        for hh in range(2):
            k_ref[0, 2 * j + hh] = kt[:, HEAD_DIM * hh:HEAD_DIM * (hh + 1)]
            v_ref[0, 2 * j + hh] = jnp.concatenate([vt[:, HEAD_DIM * hh:HEAD_DIM * (hh + 1)], ones_col], axis=1)
    for j in range(D // 128):
        qt = norm_rope(p_ref[:, KV_COLS + 128 * j:KV_COLS + 128 * (j + 1)], qg_ref[...])
        q_ref[:, 128 * j:128 * (j + 1)] = (qt * (HEAD_DIM ** -0.5 * math.log2(math.e))).astype(BF16)


def _qk_prep(proj, q_gain, k_gain, nb):
    t, ncol = proj.shape
    cos, sin = _rope_tables()
    tab = pl.BlockSpec((TOK, 128), lambda i: (i % TILES_PER_B, 0))
    gain = pl.BlockSpec((1, 128), lambda i: (0, 0))
    head_major = lambda width: pl.BlockSpec((1, N_KV_HEADS, TOK, width),
                                            lambda i: (i // TILES_PER_B, 0, i % TILES_PER_B, 0))
    hm_shape = lambda width: jax.ShapeDtypeStruct((nb, N_KV_HEADS, NTOK, width), BF16)
    return pl.pallas_call(
        _qk_prep_kernel,
        grid=(t // TOK,),
        in_specs=[pl.BlockSpec((TOK, ncol), lambda i: (i, 0)), tab, tab, gain, gain],
        out_specs=[pl.BlockSpec((TOK, D), lambda i: (i, 0)), head_major(HEAD_DIM), head_major(2 * HEAD_DIM)],
        out_shape=[jax.ShapeDtypeStruct((t, D), BF16), hm_shape(HEAD_DIM), hm_shape(2 * HEAD_DIM)],
        compiler_params=_cparams(("parallel",)),
    )(proj, cos, sin, jnp.tile(q_gain, 2).reshape(1, 128), jnp.tile(k_gain, 2).reshape(1, 128))


def _attn_kernel(q_ref, k_ref, v_ref, o_ref):
    def attend(n_keys):
        outs = []
        for kh in range(N_KV_HEADS):
            qs = jnp.concatenate(
                [q_ref[0, :, HEAD_DIM * (Q_PER_KV * kh + g):HEAD_DIM * (Q_PER_KV * kh + g + 1)]
                 for g in range(Q_PER_KV)], axis=0)
            s = _nt_dot(qs, k_ref[0, kh, :n_keys, :])
            p = jnp.exp2(s - jnp.max(s, axis=1, keepdims=True))
            ov = _dot(p.astype(BF16), v_ref[0, kh, :n_keys, :])
            o = ov[:, :HEAD_DIM] / ov[:, HEAD_DIM:HEAD_DIM + 1]
            outs += [o[ATT_TQ * g:ATT_TQ * (g + 1)] for g in range(Q_PER_KV)]
        o_ref[0] = jnp.concatenate(outs, axis=1).astype(BF16)

    is_ctx = pl.program_id(1) < CTX // ATT_TQ

    @pl.when(is_ctx)
    def _():
        attend(CTX)

    @pl.when(jnp.logical_not(is_ctx))
    def _():
        attend(NTOK)


def _attention(q, k, v, nb):
    kv = lambda width: pl.BlockSpec((1, N_KV_HEADS, NTOK, width), lambda b, i: (b, 0, 0, 0))
    qo = pl.BlockSpec((1, ATT_TQ, D), lambda b, i: (b, i, 0))
    return pl.pallas_call(
        _attn_kernel,
        grid=(nb, NTOK // ATT_TQ),
        in_specs=[qo, kv(HEAD_DIM), kv(2 * HEAD_DIM)], out_specs=qo,
        out_shape=jax.ShapeDtypeStruct((nb, NTOK, D), BF16),
        compiler_params=_cparams(("parallel", "parallel")),
    )(q.reshape(nb, NTOK, D), k, v)


def _gqa_mixer(h, nb, w_kvq, q_gain, k_gain, w_o):
    proj = _linear(h, w_kvq.astype(BF16), F32)
    q, k, v = _qk_prep(proj, q_gain, k_gain, nb)
    o = _attention(q, k, v, nb)
    return _linear(o.reshape(nb * NTOK, D), w_o.astype(BF16), F32)


def _split3(x):
    hi = x.astype(BF16)
    r = x - hi.astype(F32)
    mid = r.astype(BF16)
    lo = (r - mid.astype(F32)).astype(BF16)
    return hi, mid, lo


def _log_sigmoid(x):
    return jnp.minimum(x, 0.0) - jnp.log1p(jnp.exp(-jnp.abs(x)))


def _mlstm_kernel(h_ref, k_ref, v_ref, q_ref, wg_ref, wgt_ref, bc_ref, br_ref, o_ref,
                  col_scr, row_scr, st_scr):
    hb = h_ref[0]
    g_col = _dot(hb, wg_ref[0]) + bc_ref[0]
    g_row = _nt_dot(wgt_ref[0], hb) + br_ref[0]
    ii = lax.broadcasted_iota(jnp.int32, (ML_CHUNK, ML_CHUNK), 0)
    jj = lax.broadcasted_iota(jnp.int32, (ML_CHUNK, ML_CHUNK), 1)
    low = ii >= jj
    upp = ii <= jj
    tri_l, tri_u = low.astype(BF16), upp.astype(BF16)
    lane_c = lax.broadcasted_iota(jnp.int32, (ML_CHUNK, 8), 1)
    sub_r = lax.broadcasted_iota(jnp.int32, (8, ML_CHUNK), 0)
    for c in range(ML_NCH):
        sl = slice(ML_CHUNK * c, ML_CHUNK * (c + 1))
        gc = g_col[sl]
        parts = _split3(jnp.where(lane_c % 2 == 1, _log_sigmoid(gc), 0.0))
        cf = sum(_dot(tri_l, p) for p in parts)
        cb = sum(_dot(tri_u, p) for p in parts)
        cum = jnp.where((lane_c // 2) % 2 == 1, cb, cf)
        col_scr[sl, :] = jnp.where(lane_c % 2 == 1, cum, gc)
        gr = g_row[:, sl]
        parts = _split3(jnp.where(sub_r % 2 == 1, _log_sigmoid(gr), 0.0))
        rf = sum(_dot(p, tri_u) for p in parts)
        rb = sum(_dot(p, tri_l) for p in parts)
        cum = jnp.where((sub_r // 2) % 2 == 1, rb, rf)
        row_scr[c] = jnp.where(sub_r % 2 == 1, cum, gr)

    o_ref[...] = jnp.zeros(o_ref.shape, F32)
    st_scr[...] = jnp.zeros(st_scr.shape, F32)
    ones_col = (lax.broadcasted_iota(jnp.int32, (ML_CHUNK, ML_DV), 1) == 0).astype(BF16)

    def step(i, ms):
        new_ms = []
        for hl in range(2):
            for d in range(2):
                idx = 2 * hl + d
                if d == 0:
                    c = i
                else:
                    c = jnp.where(i < ML_CTX_CH, ML_CTX_CH - 1 - i, ML_NCH + ML_CTX_CH - 1 - i)
                rows = pl.ds(pl.multiple_of(c * ML_CHUNK, ML_CHUNK), ML_CHUNK)
                q = q_ref[0, rows, ML_DQK * hl:ML_DQK * (hl + 1)]
                ks = k_ref[0, rows, ML_DQK * hl:ML_DQK * (hl + 1)] * (ML_DQK ** -0.5)
                v_aug = jnp.concatenate([v_ref[0, rows, ML_DV * hl:ML_DV * (hl + 1)], ones_col], axis=1)
                ig_c = col_scr[rows, 2 * idx:2 * idx + 1]
                cum_c = col_scr[rows, 2 * idx + 1:2 * idx + 2]
                rr = row_scr[c]
                ig_r, cum_r = rr[2 * idx:2 * idx + 1, :], rr[2 * idx + 1:2 * idx + 2, :]
                g = cum_r[:, ML_CHUNK - 1:ML_CHUNK] if d == 0 else cum_r[:, 0:1]
                m0 = ms[idx]
                c_aug = st_scr[idx]

                log_d = jnp.where(low if d == 0 else upp, cum_c - cum_r + ig_r, -jnp.inf)
                log_inter = cum_c + m0
                m_row = jnp.maximum(log_inter, jnp.max(log_d, axis=1, keepdims=True))
                s = _nt_dot(q, ks) * jnp.exp(log_d - m_row)
                a_inter = jnp.exp(log_inter - m_row)
                nd = a_inter * _dot(q, c_aug.astype(BF16)) + _dot(s.astype(BF16), v_aug)
                den = jnp.maximum(jnp.abs(nd[:, ML_DV:ML_DV + 1]), jnp.exp(-m_row))
                o_ref[0, rows, ML_DV * hl:ML_DV * (hl + 1)] += nd[:, :ML_DV] / den

                w = g - cum_c + ig_c
                m_loc = jnp.max(w, axis=0, keepdims=True)
                ke = (ks.astype(F32) * jnp.exp(w - m_loc)).astype(BF16)
                c_loc = lax.dot_general(ke, v_aug, (((0,), (0,)), ((), ())), preferred_element_type=F32)
                m_new = jnp.maximum(g + m0, m_loc)
                st_scr[idx] = jnp.exp(g + m0 - m_new) * c_aug + jnp.exp(m_loc - m_new) * c_loc
                new_ms.append(m_new)
        return tuple(new_ms)

    lax.fori_loop(0, ML_NCH, step, tuple(jnp.zeros((1, 1), F32) for _ in range(4)))


def _mlstm_scan(h, proj, wg, wgt, b_col, b_row, nb):
    pairs = ML_HEADS // 2
    k_blk = pl.BlockSpec((1, NTOK, 2 * ML_DQK), lambda b, p: (b, 0, p))
    v_blk = pl.BlockSpec((1, NTOK, 2 * ML_DV), lambda b, p: (b, 0, ML_QK // (2 * ML_DV) + p))
    q_blk = pl.BlockSpec((1, NTOK, 2 * ML_DQK), lambda b, p: (b, 0, (ML_QK + D) // (2 * ML_DQK) + p))
    per_pair = lambda shape: pl.BlockSpec((1,) + shape, lambda b, p: (p, 0, 0))
    proj3 = proj.reshape(nb, NTOK, proj.shape[-1])
    return pl.pallas_call(
        _mlstm_kernel,
        grid=(nb, pairs),
        in_specs=[pl.BlockSpec((1, NTOK, D), lambda b, p: (b, 0, 0)), k_blk, v_blk, q_blk,
                  per_pair((D, 8)), per_pair((8, D)), per_pair((1, 8)), per_pair((8, 1))],
        out_specs=pl.BlockSpec((1, NTOK, 2 * ML_DV), lambda b, p: (b, 0, p)),
        out_shape=jax.ShapeDtypeStruct((nb, NTOK, D), F32),
        scratch_shapes=[pltpu.VMEM((NTOK, 8), F32), pltpu.VMEM((ML_NCH, 8, ML_CHUNK), F32),
                        pltpu.VMEM((4, ML_DQK, 2 * ML_DV), F32)],
        compiler_params=_cparams(("parallel", "parallel")),
    )(h.reshape(nb, NTOK, D), proj3, proj3, proj3, wg, wgt, b_col, b_row)


def _ml_finish_kernel(hh_ref, o_ref, g_ref, w_ref, y_ref):
    parts = [_rms(hh_ref[:, ML_DV * i:ML_DV * (i + 1)]) for i in range(ML_HEADS)]
    hn = jnp.concatenate(parts, axis=1) * g_ref[...]
    z = (hn * jax.nn.sigmoid(o_ref[...].astype(F32))).astype(BF16)
    y_ref[...] = _dot(z, w_ref[...])


def _ml_finish(hh, proj, norm_g, w_out):
    t = hh.shape[0]
    tok = pl.BlockSpec((LIN_TM, D), lambda i: (i, 0))
    return pl.pallas_call(
        _ml_finish_kernel,
        grid=(t // LIN_TM,),
        in_specs=[tok, pl.BlockSpec((LIN_TM, D), lambda i: (i, (ML_QK + D + ML_QK) // D)),
                  pl.BlockSpec((1, D), lambda i: (0, 0)), pl.BlockSpec((D, D), lambda i: (0, 0))],
        out_specs=tok,
        out_shape=jax.ShapeDtypeStruct((t, D), F32),
        compiler_params=_cparams(("parallel",)),
    )(hh, proj, norm_g.reshape(1, D), w_out)


def _mlstm_mixer(h, nb, w_in, b_gate, norm_g, w_out):
    n_state = ML_QK + D
    n_gate = 4 * ML_HEADS
    w_main = jnp.concatenate([w_in[:, :n_state], w_in[:, n_state + n_gate:]], axis=1).astype(BF16)
    pairs = ML_HEADS // 2
    wg = w_in[:, n_state:n_state + n_gate].reshape(D, 2, 2, pairs, 2).transpose(3, 0, 4, 1, 2)
    wg = wg.reshape(pairs, D, 8).astype(BF16)
    bg = b_gate.reshape(2, 2, pairs, 2).transpose(2, 3, 0, 1).reshape(pairs, 8)
    proj = _linear(h, w_main, BF16)
    hh = _mlstm_scan(h, proj, wg, wg.transpose(0, 2, 1), bg[:, None, :], bg[:, :, None], nb)
    return _ml_finish(hh.reshape(nb * NTOK, D), proj, norm_g, w_out.astype(BF16))


PAIRS = tuple((a, b) for a in range(4) for b in range(a + 1, 4))
N_CAT = N_GROUPS * len(PAIRS)


def _route_tables(ridx, rw, t):
    n_tiles = t // MOE_TM + N_CAT
    n_rows = n_tiles * MOE_TM
    lo, hi = ridx[0], ridx[1]
    a, b = lo % 4, hi % 4
    cat = (lo // 4) * len(PAIRS) + (a * (7 - a)) // 2 + (b - a - 1)
    onehot = (cat[:, None] == jnp.arange(N_CAT)[None, :]).astype(jnp.int32)
    rank = jnp.take_along_axis(jnp.cumsum(onehot, axis=0) - onehot, cat[:, None], axis=1)[:, 0]
    counts = jnp.sum(onehot, axis=0)
    padded = ((counts + MOE_TM - 1) // MOE_TM) * MOE_TM
    ends = jnp.cumsum(padded)
    pos = (ends - padded)[cat] + rank
    tok_ids = jnp.arange(t, dtype=jnp.int32)
    src = jnp.zeros((n_rows,), jnp.int32).at[pos].set(tok_ids)
    w = jnp.zeros((n_rows, 2), F32).at[pos].set(rw.T)
    n_used = ends[-1] // MOE_TM
    tile_start = jnp.minimum(jnp.arange(n_tiles), n_used - 1) * MOE_TM
    tile_cat = jnp.sum((ends[None, :] <= tile_start[:, None]).astype(jnp.int32), axis=1)
    n_valid = jnp.clip((ends - padded + counts)[tile_cat] - tile_start, 0, MOE_TM)
    pa = jnp.array([p[0] for p in PAIRS], jnp.int32)[tile_cat % len(PAIRS)]
    pb = jnp.array([p[1] for p in PAIRS], jnp.int32)[tile_cat % len(PAIRS)]
    base = (tile_cat // len(PAIRS)) * 4
    i32 = lambda a: a.astype(jnp.int32)
    return i32(base + pa), i32(base + pb), i32(n_used).reshape(1), i32(n_valid), src, w


def _moe_kernel(ea_ref, eb_ref, nu_ref, nv_ref, tok_ref,
                h_hbm, w_ref, wga_ref, wua_ref, wda_ref, wgb_ref, wub_ref, wdb_ref, y_hbm,
                xbuf, ybuf, bin_scr, gsem, ssem):
    i = pl.program_id(0)
    n_used = nu_ref[0]
    slot = i % 2

    def by_eights(row_fn):
        def body(c, carry):
            for u in range(8):
                row_fn(c * 8 + u, carry)
            return carry
        return body

    def issue_gather(tile, sl):
        def row(r, carry):
            tok = tok_ref[tile * MOE_TM + r]
            pltpu.make_async_copy(h_hbm.at[pl.ds(tok, 1)], xbuf.at[sl, pl.ds(r, 1)], gsem.at[sl]).start()
            return carry
        lax.fori_loop(0, MOE_TM // 8, by_eights(row), 0)

    def wait_gather(sl):
        pltpu.make_async_copy(h_hbm.at[pl.ds(0, MOE_TM)], xbuf.at[sl], gsem.at[sl]).wait()

    def issue_scatter(tile, sl):
        def to_hbm(r, carry):
            tok = tok_ref[tile * MOE_TM + r]
            pltpu.make_async_copy(ybuf.at[sl, pl.ds(r, 1)], y_hbm.at[pl.ds(tok, 1)], ssem.at[sl]).start()
            return carry

        def to_bin(r, carry):
            pltpu.make_async_copy(ybuf.at[sl, pl.ds(r, 1)], bin_scr.at[sl, pl.ds(r, 1)], ssem.at[sl]).start()
            return carry
        nv = nv_ref[tile]
        nv_down = lax.shift_right_logical(nv, 3)
        nv_up = lax.shift_right_logical(nv + 7, 3)
        lax.fori_loop(0, nv_down, by_eights(to_hbm), 0)
        lax.fori_loop(nv_down * 8, nv, to_hbm, 0)
        lax.fori_loop(nv, nv_up * 8, to_bin, 0)
        lax.fori_loop(nv_up, MOE_TM // 8, by_eights(to_bin), 0)

    def wait_scatter(sl):
        pltpu.make_async_copy(ybuf.at[sl], y_hbm.at[pl.ds(0, MOE_TM)], ssem.at[sl]).wait()

    @pl.when(i == 0)
    def _():
        issue_gather(0, 0)

    @pl.when(i + 1 < n_used)
    def _():
        issue_gather(i + 1, 1 - slot)

    @pl.when(i < n_used)
    def _():
        wait_gather(slot)

        @pl.when(i >= 2)
        def _():
            wait_scatter(slot)

        x = xbuf[slot].astype(BF16)
        w = w_ref[...]

        def expert(wg_ref, wu_ref, wd_ref):
            a = _dot(x, wg_ref[0])
            hid = (a * jax.nn.sigmoid(a) * _dot(x, wu_ref[0])).astype(BF16)
            return _dot(hid, wd_ref[0])

        ybuf[slot] = (w[:, 0:1] * expert(wga_ref, wua_ref, wda_ref)
                      + w[:, 1:2] * expert(wgb_ref, wub_ref, wdb_ref))
        issue_scatter(i, slot)

    @pl.when(i == pl.num_programs(0) - 1)
    def _():
        wait_scatter(0)
        wait_scatter(1)


def _moe(h2, ridx, rw, w_gate, w_up, w_down):
    t = h2.shape[0]
    ea, eb, n_used, n_valid, src, w = _route_tables(ridx, rw, t)
    n_tiles = ea.shape[0]
    assert t // MOE_TM >= 2
    wspec = lambda which, shape: pl.BlockSpec(
        (1,) + shape, (lambda i, ea, eb, nu, nv, tok: (ea[i], 0, 0)) if which == 0
        else (lambda i, ea, eb, nu, nv, tok: (eb[i], 0, 0)))
    wg, wu, wd = w_gate.astype(BF16), w_up.astype(BF16), w_down.astype(BF16)
    return pl.pallas_call(
        _moe_kernel,
        grid_spec=pltpu.PrefetchScalarGridSpec(
            num_scalar_prefetch=5,
            grid=(n_tiles,),
            in_specs=[pl.BlockSpec(memory_space=pl.ANY),
                      pl.BlockSpec((MOE_TM, 2), lambda i, *_: (i, 0)),
                      wspec(0, (D, D_FF)), wspec(0, (D, D_FF)), wspec(0, (D_FF, D)),
                      wspec(1, (D, D_FF)), wspec(1, (D, D_FF)), wspec(1, (D_FF, D))],
            out_specs=pl.BlockSpec(memory_space=pl.ANY),
            scratch_shapes=[pltpu.VMEM((2, MOE_TM, D), F32), pltpu.VMEM((2, MOE_TM, D), F32),
                            pltpu.VMEM((2, MOE_TM, D), F32),
                            pltpu.SemaphoreType.DMA((2,)), pltpu.SemaphoreType.DMA((2,))]),
        out_shape=jax.ShapeDtypeStruct((t, D), F32),
        compiler_params=_cparams(("arbitrary",)),
    )(ea, eb, n_used, n_valid, src, h2, w, wg, wu, wd, wg, wu, wd)


def kernel(x, c, ctx, c_ctx, ada_w, ada_b, norm1_g, norm2_g, final_g, s5_lambda_re, s5_lambda_im, s5_log_step, s5_b_re, s5_b_im, s5_c_re, s5_c_im, s5_d, s5_w_glu, s5_b_glu, attn_w_kvq, attn_q_gain, attn_k_gain, attn_w_o, ml_w_in, ml_b_gate, ml_norm_g, ml_w_out, router_w, router_bias, moe_w_gate, moe_w_up, moe_w_down):
    nb = x.shape[0]
    depth = ada_w.shape[0]
    assert x.shape[1:] == (SEQ, D) and ctx.shape[1:] == (CTX, D) and nb + 1 <= 16
    xs = jnp.concatenate([ctx, x], axis=1).reshape(nb * NTOK, D)
    cond_in = jnp.zeros((16, D), F32).at[:nb].set(c).at[nb].set(c_ctx)
    mods = _ada(cond_in, ada_w, ada_b).reshape(depth, 16, 6, D)

    y2 = None
    for layer in range(depth):
        kind, j = layer % N_MIXERS, layer // N_MIXERS
        xs, h = _pre(xs, y2, mods[layer - 1] if layer else None, mods[layer], norm1_g[layer], nb,
                     F32 if kind == 0 else BF16, chunk_major=kind == 0)
        if kind == 0:
            y = _s5_mixer(h, nb, s5_lambda_re[j], s5_lambda_im[j], s5_log_step[j], s5_b_re[j], s5_b_im[j],
                          s5_c_re[j], s5_c_im[j], s5_d[j], s5_w_glu[j], s5_b_glu[j])
        elif kind == 1:
            y = _gqa_mixer(h, nb, attn_w_kvq[j], attn_q_gain[j], attn_k_gain[j], attn_w_o[j])
        else:
            y = _mlstm_mixer(h, nb, ml_w_in[j], ml_b_gate[j], ml_norm_g[j], ml_w_out[j])
        xs, h2, ridx, rw = _post(xs, y, mods[layer], norm2_g[layer], router_w, router_bias, nb)
        y2 = _moe(h2, ridx, rw, moe_w_gate[layer], moe_w_up[layer], moe_w_down[layer])
    out = _final(xs, y2, mods[depth - 1], final_g, nb)
    return out.reshape(nb, SEQ, D)
```

```python
import functools
import math

import jax
import jax.numpy as jnp
import numpy as np
from jax import lax
from jax.experimental import pallas as pl
from jax.experimental.pallas import tpu as pltpu

F32 = jnp.float32
BF16 = jnp.bfloat16

D = 1024
CTX = 256
SEQ = 2048
NTOK = CTX + SEQ
GRID_W = 64
EPS = 1e-6
N_MIXERS = 3

TOK = 256
TILES_PER_B = NTOK // TOK
LIN_TM = 512

S5_GROUP = 16
S5_GROUPS = D // S5_GROUP
S5_STATE = 64
S5_CHUNK = 16
S5_NCH = NTOK // S5_CHUNK
S5_CTX_CH = CTX // S5_CHUNK
S5_SLAB_GROUPS = 128 // S5_GROUP

HEAD_DIM = 64
N_Q_HEADS = 16
N_KV_HEADS = 4
Q_PER_KV = 4
KV_COLS = 2 * N_KV_HEADS * HEAD_DIM
ROPE_THETA = 10000.0
ATT_TQ = 128

ML_HEADS = 8
ML_DQK = 64
ML_DV = 128
ML_QK = 512
ML_CHUNK = 128
ML_NCH = NTOK // ML_CHUNK
ML_CTX_CH = CTX // ML_CHUNK

N_EXPERTS = 16
N_GROUPS = 4
D_FF = 512
MOE_TM = 256
MOE_ROW = D + 128

VMEM_LIMIT = 56 * 1024 * 1024


def _cparams(sem):
    return pltpu.CompilerParams(dimension_semantics=sem, vmem_limit_bytes=VMEM_LIMIT)


def _nt_dot(a, b, **kw):
    return lax.dot_general(a, b, (((1,), (1,)), ((), ())), preferred_element_type=F32, **kw)


def _dot(a, b):
    return jnp.dot(a, b, preferred_element_type=F32)


def _rms(x):
    return x * lax.rsqrt(jnp.mean(x * x, axis=-1, keepdims=True) + EPS)


def _ada_kernel(c_ref, w_ref, b_ref, o_ref):
    c = c_ref[...]
    cond = (c * jax.nn.sigmoid(c)).astype(BF16)
    o_ref[0] = _dot(cond, w_ref[0].astype(BF16)) + b_ref[0]


def _ada(cond_in, ada_w, ada_b):
    depth, _, n = ada_w.shape
    tn = 1536
    rows = cond_in.shape[0]
    return pl.pallas_call(
        _ada_kernel,
        grid=(depth, n // tn),
        in_specs=[pl.BlockSpec((rows, D), lambda l, j: (0, 0)),
                  pl.BlockSpec((1, D, tn), lambda l, j: (l, 0, j)),
                  pl.BlockSpec((1, 1, tn), lambda l, j: (l, 0, j))],
        out_specs=pl.BlockSpec((1, rows, tn), lambda l, j: (l, 0, j)),
        out_shape=jax.ShapeDtypeStruct((depth, rows, n), F32),
        compiler_params=_cparams(("parallel", "parallel")),
    )(cond_in, ada_w, ada_b.reshape(depth, 1, n))


def _mod_row(nb):
    return lambda i: (jnp.where(i % TILES_PER_B == 0, nb, i // TILES_PER_B), 0, 0)


def _pre_kernel(has_res, *refs):
    if has_res:
        x_ref, y_ref, mprev_ref, mcur_ref, g_ref, xo_ref, h_ref = refs
        x = x_ref[...] + mprev_ref[0, 5:6, :] * y_ref[...]
        xo_ref[...] = x
    else:
        x_ref, mcur_ref, g_ref, h_ref = refs
        x = x_ref[...]
    h = _rms(x) * g_ref[...]
    h = h * (1.0 + mcur_ref[0, 1:2, :]) + mcur_ref[0, 0:1, :]
    h_ref[...] = h.astype(h_ref.dtype).reshape(h_ref.shape)


def _chunk_major(nb):
    shape = (S5_NCH, nb, S5_CHUNK, D)
    spec = pl.BlockSpec((TOK // S5_CHUNK, 1, S5_CHUNK, D), lambda i: (i % TILES_PER_B, i // TILES_PER_B, 0, 0))
    return shape, spec


def _pre(x, y2, mods_prev, mods_cur, g, nb, h_dtype, chunk_major=False):
    t = x.shape[0]
    tok = pl.BlockSpec((TOK, D), lambda i: (i, 0))
    mod = pl.BlockSpec((1, 6, D), _mod_row(nb))
    vec = pl.BlockSpec((1, D), lambda i: (0, 0))
    h_shape, h_spec = _chunk_major(nb) if chunk_major else ((t, D), tok)
    if y2 is None:
        return x, pl.pallas_call(
            functools.partial(_pre_kernel, False),
            grid=(t // TOK,), in_specs=[tok, mod, vec], out_specs=h_spec,
            out_shape=jax.ShapeDtypeStruct(h_shape, h_dtype),
            compiler_params=_cparams(("parallel",)),
        )(x, mods_cur, g.reshape(1, D))
    return pl.pallas_call(
        functools.partial(_pre_kernel, True),
        grid=(t // TOK,), in_specs=[tok, tok, mod, mod, vec], out_specs=[tok, h_spec],
        out_shape=[jax.ShapeDtypeStruct((t, D), F32), jax.ShapeDtypeStruct(h_shape, h_dtype)],
        compiler_params=_cparams(("parallel",)),
    )(x, y2, mods_prev, mods_cur, g.reshape(1, D))


def _post_kernel(x_ref, y_ref, m_ref, g_ref, rwt_ref, rb_ref, xo_ref, h_ref, ridx_ref):
    x = x_ref[...] + m_ref[0, 2:3, :] * y_ref[...].reshape(x_ref.shape)
    xo_ref[...] = x
    h2 = _rms(x) * g_ref[...]
    h2 = h2 * (1.0 + m_ref[0, 4:5, :]) + m_ref[0, 3:4, :]
    h_ref[:, :D] = h2

    logits = _nt_dot(rwt_ref[...], h2, precision=lax.Precision.HIGHEST)
    scores = jax.nn.sigmoid(logits)
    sel = scores + rb_ref[...]
    row = lax.broadcasted_iota(jnp.int32, sel.shape, 0)
    per_group = N_EXPERTS // N_GROUPS
    best_val, best = None, None
    for g in range(N_GROUPS):
        r = [sel[per_group * g + j:per_group * g + j + 1, :] for j in range(per_group)]
        gs = None
        for i in range(per_group):
            for j in range(i + 1, per_group):
                gs = r[i] + r[j] if gs is None else jnp.maximum(gs, r[i] + r[j])
        if g == 0:
            best_val, best = gs, jnp.zeros(gs.shape, jnp.int32)
        else:
            upd = gs > best_val
            best = jnp.where(upd, g, best)
            best_val = jnp.where(upd, gs, best_val)
    masked = jnp.where((row // per_group) == best, sel, -jnp.inf)
    m1 = jnp.max(masked, axis=0, keepdims=True)
    i1 = jnp.min(jnp.where(masked == m1, row, N_EXPERTS), axis=0, keepdims=True)
    masked2 = jnp.where(row == i1, -jnp.inf, masked)
    m2 = jnp.max(masked2, axis=0, keepdims=True)
    i2 = jnp.min(jnp.where(masked2 == m2, row, N_EXPERTS), axis=0, keepdims=True)
    lo, hi = jnp.minimum(i1, i2), jnp.maximum(i1, i2)
    s_lo = jnp.sum(jnp.where(row == lo, scores, 0.0), axis=0, keepdims=True)
    s_hi = jnp.sum(jnp.where(row == hi, scores, 0.0), axis=0, keepdims=True)
    ridx_ref[...] = jnp.concatenate([lo, hi], axis=0)
    wts = jnp.concatenate([s_lo, s_hi, jnp.zeros((126, TOK), F32)], axis=0) / (s_lo + s_hi)
    h_ref[:, D:] = wts.T


def _post(x, y, mods, g, router_w, router_bias, nb):
    t = x.shape[0]
    tok = pl.BlockSpec((TOK, D), lambda i: (i, 0))
    return pl.pallas_call(
        _post_kernel,
        grid=(t // TOK,),
        in_specs=[tok, _chunk_major(nb)[1] if y.ndim == 4 else tok, pl.BlockSpec((1, 6, D), _mod_row(nb)),
                  pl.BlockSpec((1, D), lambda i: (0, 0)),
                  pl.BlockSpec((N_EXPERTS, D), lambda i: (0, 0)),
                  pl.BlockSpec((N_EXPERTS, 1), lambda i: (0, 0))],
        out_specs=[tok, pl.BlockSpec((TOK, MOE_ROW), lambda i: (i, 0)), pl.BlockSpec((2, TOK), lambda i: (0, i))],
        out_shape=[jax.ShapeDtypeStruct((t, D), F32), jax.ShapeDtypeStruct((t, MOE_ROW), F32),
                   jax.ShapeDtypeStruct((2, t), jnp.int32)],
        compiler_params=_cparams(("parallel",)),
    )(x, y, mods, g.reshape(1, D), router_w.T, router_bias.reshape(N_EXPERTS, 1))


def _final_kernel(x_ref, y_ref, m_ref, g_ref, o_ref):
    x = x_ref[...] + m_ref[0, 5:6, :] * y_ref[...]
    o_ref[...] = _rms(x) * g_ref[...]


def _final(x, y2, mods, g, nb):
    lat_tiles = SEQ // TOK
    src = lambda j: ((j // lat_tiles) * TILES_PER_B + CTX // TOK + j % lat_tiles, 0)
    tok_in = pl.BlockSpec((TOK, D), src)
    return pl.pallas_call(
        _final_kernel,
        grid=(nb * lat_tiles,),
        in_specs=[tok_in, tok_in, pl.BlockSpec((1, 6, D), lambda j: (j // lat_tiles, 0, 0)),
                  pl.BlockSpec((1, D), lambda j: (0, 0))],
        out_specs=pl.BlockSpec((TOK, D), lambda j: (j, 0)),
        out_shape=jax.ShapeDtypeStruct((nb * SEQ, D), F32),
        compiler_params=_cparams(("parallel",)),
    )(x, y2, mods, g.reshape(1, D))


def _linear_kernel(x_ref, w_ref, o_ref):
    o_ref[...] = _dot(x_ref[...].astype(BF16), w_ref[...]).astype(o_ref.dtype)


def _linear(x, w, out_dtype):
    t, k = x.shape
    n = w.shape[1]
    return pl.pallas_call(
        _linear_kernel,
        grid=(t // LIN_TM,),
        in_specs=[pl.BlockSpec((LIN_TM, k), lambda i: (i, 0)), pl.BlockSpec((k, n), lambda i: (0, 0))],
        out_specs=pl.BlockSpec((LIN_TM, n), lambda i: (i, 0)),
        out_shape=jax.ShapeDtypeStruct((t, n), out_dtype),
        compiler_params=_cparams(("parallel",)),
    )(x, w)


def _s5_operators(lam_re, lam_im, log_step, b_re, b_im, c_re, c_im):
    n = S5_CHUNK
    dt = jnp.exp(log_step)[None, :, :, None]
    k = jnp.arange(n + 1, dtype=F32)[:, None, None, None]
    mag = jnp.exp(k * lam_re[None] * dt)
    ang = k * lam_im[None] * dt
    pw_re, pw_im = mag * jnp.cos(ang), mag * jnp.sin(ang)
    a_re, a_im = pw_re[1], pw_im[1]
    den = lam_re * lam_re + lam_im * lam_im
    n_re, n_im = a_re - 1.0, a_im
    f_re = (n_re * lam_re + n_im * lam_im) / den
    f_im = (n_im * lam_re - n_re * lam_im) / den
    bb_re = f_re[..., None] * b_re - f_im[..., None] * b_im
    bb_im = f_re[..., None] * b_im + f_im[..., None] * b_re

    cp_re = c_re[None] * pw_re[:, :, :, None, :] - c_im[None] * pw_im[:, :, :, None, :]
    cp_im = c_re[None] * pw_im[:, :, :, None, :] + c_im[None] * pw_re[:, :, :, None, :]
    kern = (jnp.einsum('kdgop,dgpc->kdgoc', cp_re[:n], bb_re)
            - jnp.einsum('kdgop,dgpc->kdgoc', cp_im[:n], bb_im))
    s_idx = jnp.arange(n)[:, None]
    t_idx = jnp.arange(n)[None, :]
    lag_f, lag_b = t_idx - s_idx, s_idx - t_idx
    kf = jnp.where((lag_f >= 0)[:, :, None, None, None], kern[jnp.clip(lag_f, 0, n - 1), 0], 0.0)
    kb = jnp.where((lag_b >= 0)[:, :, None, None, None], kern[jnp.clip(lag_b, 0, n - 1), 1], 0.0)
    toep = (kf + kb).transpose(2, 0, 4, 1, 3).reshape(S5_GROUPS, n * S5_GROUP, n * S5_GROUP)

    def state_in(pw_sel, d):
        pr, pi = pw_re[pw_sel, d], pw_im[pw_sel, d]
        e_re = pr[..., None] * bb_re[d][None] - pi[..., None] * bb_im[d][None]
        e_im = pr[..., None] * bb_im[d][None] + pi[..., None] * bb_re[d][None]
        to_w = lambda e: jnp.pad(e.transpose(1, 0, 3, 2).reshape(S5_GROUPS, n * S5_GROUP, S5_STATE),
                                 ((0, 0), (0, 0), (0, 128 - S5_STATE)))
        return to_w(e_re), to_w(e_im)

    def state_out(pw_sel, d):
        o_re, o_im = cp_re[pw_sel, d], cp_im[pw_sel, d]
        to_w = lambda o: jnp.pad(o.transpose(1, 3, 0, 2).reshape(S5_GROUPS, S5_STATE, n * S5_GROUP),
                                 ((0, 0), (0, 128 - S5_STATE), (0, 0)))
        return to_w(o_re), to_w(-o_im)

    ar = jnp.arange(n)
    w1 = jnp.concatenate([toep, *state_in(n - 1 - ar, 0), *state_in(ar, 1)], axis=2)
    w2 = jnp.concatenate([*state_out(ar + 1, 0), *state_out(n - ar, 1)], axis=1)
    pad = lambda a: jnp.pad(a, ((0, 0), (0, 128 - S5_STATE)))
    a16 = jnp.stack([pad(pw_re[n, 0]), pad(pw_im[n, 0]), pad(pw_re[n, 1]), pad(pw_im[n, 1])], axis=1)
    return w1.astype(BF16), w2.astype(BF16), a16


def _s5_select():
    sel = np.zeros((S5_SLAB_GROUPS, S5_CHUNK * 128, S5_CHUNK * S5_GROUP), np.float32)
    s, c = np.meshgrid(np.arange(S5_CHUNK), np.arange(S5_GROUP), indexing="ij")
    for j in range(S5_SLAB_GROUPS):
        sel[j, s * 128 + S5_GROUP * j + c, s * S5_GROUP + c] = 1.0
    return jnp.asarray(sel, BF16)


def _s5_kernel(nb, h_ref, sel_ref, w1_ref, w2_ref, a_ref, y_ref, x2_scr, s_scr, x_scr):
    rows = S5_NCH * nb

    @pl.when(pl.program_id(1) == 0)
    def _():
        for s in range(S5_CHUNK):
            x2_scr[:, 128 * s:128 * (s + 1)] = h_ref[pl.ds(s, rows, stride=S5_CHUNK), :].astype(BF16)
        y_ref[...] = jnp.zeros(y_ref.shape, F32)

    u = _dot(x2_scr[...], sel_ref[0]).astype(BF16)
    s_scr[...] = _dot(u, w1_ref[0])

    def step(i, carry):
        cf = i
        cb = jnp.where(i < S5_CTX_CH, S5_CTX_CH - 1 - i, S5_NCH + S5_CTX_CH - 1 - i)
        rows_f = pl.ds(pl.multiple_of(cf * nb, nb), nb)
        rows_b = pl.ds(pl.multiple_of(cb * nb, nb), nb)
        fr, fi, br, bi = carry
        x_scr[rows_f, 0:128] = fr
        x_scr[rows_f, 128:256] = fi
        x_scr[rows_b, 256:384] = br
        x_scr[rows_b, 384:512] = bi
        ar, ai = a_ref[0, 0:1, :], a_ref[0, 1:2, :]
        nfr = ar * fr - ai * fi + s_scr[rows_f, 256:384]
        nfi = ar * fi + ai * fr + s_scr[rows_f, 384:512]
        ar, ai = a_ref[0, 2:3, :], a_ref[0, 3:4, :]
        nbr = ar * br - ai * bi + s_scr[rows_b, 512:640]
        nbi = ar * bi + ai * br + s_scr[rows_b, 640:768]
        return nfr, nfi, nbr, nbi

    zero = jnp.zeros((nb, 128), F32)
    lax.fori_loop(0, S5_NCH, step, (zero, zero, zero, zero))
    yg = s_scr[:, 0:256] + _dot(x_scr[...].astype(BF16), w2_ref[0])
    y_ref[0] += _nt_dot(yg.astype(BF16), sel_ref[0])


def _s5_scan(h_cm, w1, w2, a16, nb):
    rows = S5_NCH * nb
    kc = S5_CHUNK * S5_GROUP
    n_slabs = D // 128
    per_group = lambda shape: pl.BlockSpec((1,) + shape, lambda q, j: (q * S5_SLAB_GROUPS + j, 0, 0))
    return pl.pallas_call(
        functools.partial(_s5_kernel, nb),
        grid=(n_slabs, S5_SLAB_GROUPS),
        in_specs=[pl.BlockSpec((nb * NTOK, 128), lambda q, j: (0, q)),
                  pl.BlockSpec((1, S5_CHUNK * 128, kc), lambda q, j: (j, 0, 0)),
                  per_group((kc, 768)), per_group((512, kc)), per_group((4, 128))],
        out_specs=pl.BlockSpec((1, rows, S5_CHUNK * 128), lambda q, j: (q, 0, 0)),
        out_shape=jax.ShapeDtypeStruct((n_slabs, rows, S5_CHUNK * 128), F32),
        scratch_shapes=[pltpu.VMEM((rows, S5_CHUNK * 128), BF16), pltpu.VMEM((rows, 768), F32),
                        pltpu.VMEM((rows, 512), F32)],
        compiler_params=_cparams(("parallel", "arbitrary")),
    )(h_cm.reshape(nb * NTOK, D), _s5_select(), w1, w2, a16)


def _gelu(x):
    return 0.5 * x * (1.0 + jnp.tanh(math.sqrt(2.0 / math.pi) * (x + 0.044715 * (x * x * x))))


def _glu_kernel(y_ref, h_ref, d_ref, w_ref, b_ref, o_ref, ys_scr):
    rows = TOK // S5_CHUNK
    for q in range(D // 128):
        for s in range(S5_CHUNK):
            ys_scr[q, pl.ds(s, rows, stride=S5_CHUNK), :] = y_ref[q, :, 128 * s:128 * (s + 1)]
    y = jnp.concatenate([ys_scr[q] for q in range(D // 128)], axis=1)
    z = _gelu(y + d_ref[...] * h_ref[...]).astype(BF16)
    r = _dot(z, w_ref[...]) + b_ref[...]
    o_ref[...] = r[:, :D] * jax.nn.sigmoid(r[:, D:])


def _glu(y, h, d_skip, w, b):
    t = h.shape[0]
    tok = pl.BlockSpec((TOK, D), lambda i: (i, 0))
    return pl.pallas_call(
        _glu_kernel,
        grid=(t // TOK,),
        in_specs=[pl.BlockSpec((D // 128, TOK // S5_CHUNK, S5_CHUNK * 128), lambda i: (0, i, 0)), tok,
                  pl.BlockSpec((1, D), lambda i: (0, 0)),
                  pl.BlockSpec((D, 2 * D), lambda i: (0, 0)), pl.BlockSpec((1, 2 * D), lambda i: (0, 0))],
        out_specs=tok,
        out_shape=jax.ShapeDtypeStruct((t, D), F32),
        scratch_shapes=[pltpu.VMEM((D // 128, TOK, 128), F32)],
        compiler_params=_cparams(("parallel",)),
    )(y, h, d_skip.reshape(1, D), w, b.reshape(1, 2 * D))


def _s5_mixer(h_cm, nb, lam_re, lam_im, log_step, b_re, b_im, c_re, c_im, d_skip, w_glu, b_glu):
    w1, w2, a16 = _s5_operators(lam_re, lam_im, log_step, b_re, b_im, c_re, c_im)
    y = _s5_scan(h_cm, w1, w2, a16, nb)
    y = _glu(y, h_cm.reshape(nb * NTOK, D), d_skip, w_glu.astype(BF16), b_glu)
    return y.reshape(h_cm.shape)


def _rope_tables():
    rows = SEQ // GRID_W
    row = jnp.repeat(jnp.arange(rows), GRID_W).astype(F32)
    col = jnp.tile(jnp.arange(GRID_W), rows).astype(F32)
    half = HEAD_DIM // 4
    inv_freq = ROPE_THETA ** (-jnp.arange(half, dtype=F32) / half)
    ang_r, ang_c = row[:, None] * inv_freq, col[:, None] * inv_freq
    cos = jnp.concatenate([jnp.cos(ang_r)] * 2 + [jnp.cos(ang_c)] * 2, axis=1)
    sin = jnp.concatenate([-jnp.sin(ang_r), jnp.sin(ang_r), -jnp.sin(ang_c), jnp.sin(ang_c)], axis=1)
    cos = jnp.concatenate([jnp.ones((CTX, HEAD_DIM), F32), cos], axis=0)
    sin = jnp.concatenate([jnp.zeros((CTX, HEAD_DIM), F32), sin], axis=0)
    return jnp.tile(cos, (1, 2)), jnp.tile(sin, (1, 2))


def _qk_prep_kernel(p_ref, cos_ref, sin_ref, qg_ref, kg_ref, q_ref, k_ref, v_ref):
    lane = lax.broadcasted_iota(jnp.int32, (TOK, 128), 1)
    low_head = lane < HEAD_DIM
    first_half = (lane % 32) < 16
    cos, sin = cos_ref[...], sin_ref[...]

    def norm_rope(x, gain):
        sq = x * x
        s_lo = jnp.sum(jnp.where(low_head, sq, 0.0), axis=1, keepdims=True)
        s_hi = jnp.sum(sq, axis=1, keepdims=True) - s_lo
        inv = lax.rsqrt(jnp.where(low_head, s_lo, s_hi) * (1.0 / HEAD_DIM) + EPS)
        xn = x * inv * gain
        partner = jnp.where(first_half, pltpu.roll(xn, 128 - 16, 1), pltpu.roll(xn, 16, 1))
        return xn * cos + partner * sin

    ones_col = (lax.broadcasted_iota(jnp.int32, (TOK, HEAD_DIM), 1) == 0).astype(BF16)
    for j in range(KV_COLS // 2 // 128):
        kt = norm_rope(p_ref[:, 128 * j:128 * (j + 1)], kg_ref[...]).astype(BF16)
        vt = p_ref[:, KV_COLS // 2 + 128 * j:KV_COLS // 2 + 128 * (j + 1)].astype(BF16)
        for hh in range(2):
            k_ref[0, 2 * j + hh] = kt[:, HEAD_DIM * hh:HEAD_DIM * (hh + 1)]
            v_ref[0, 2 * j + hh] = jnp.concatenate([vt[:, HEAD_DIM * hh:HEAD_DIM * (hh + 1)], ones_col], axis=1)
    for j in range(D // 128):
        qt = norm_rope(p_ref[:, KV_COLS + 128 * j:KV_COLS + 128 * (j + 1)], qg_ref[...])
        q_ref[:, 128 * j:128 * (j + 1)] = (qt * (HEAD_DIM ** -0.5 * math.log2(math.e))).astype(BF16)


def _qk_prep(proj, q_gain, k_gain, nb):
    t, ncol = proj.shape
    cos, sin = _rope_tables()
    tab = pl.BlockSpec((TOK, 128), lambda i: (i % TILES_PER_B, 0))
    gain = pl.BlockSpec((1, 128), lambda i: (0, 0))
    head_major = lambda width: pl.BlockSpec((1, N_KV_HEADS, TOK, width),
                                            lambda i: (i // TILES_PER_B, 0, i % TILES_PER_B, 0))
    hm_shape = lambda width: jax.ShapeDtypeStruct((nb, N_KV_HEADS, NTOK, width), BF16)
    return pl.pallas_call(
        _qk_prep_kernel,
        grid=(t // TOK,),
        in_specs=[pl.BlockSpec((TOK, ncol), lambda i: (i, 0)), tab, tab, gain, gain],
        out_specs=[pl.BlockSpec((TOK, D), lambda i: (i, 0)), head_major(HEAD_DIM), head_major(2 * HEAD_DIM)],
        out_shape=[jax.ShapeDtypeStruct((t, D), BF16), hm_shape(HEAD_DIM), hm_shape(2 * HEAD_DIM)],
        compiler_params=_cparams(("parallel",)),
    )(proj, cos, sin, jnp.tile(q_gain, 2).reshape(1, 128), jnp.tile(k_gain, 2).reshape(1, 128))


def _attn_kernel(q_ref, k_ref, v_ref, o_ref):
    def attend(n_keys):
        outs = []
        for kh in range(N_KV_HEADS):
            qs = jnp.concatenate(
                [q_ref[0, :, HEAD_DIM * (Q_PER_KV * kh + g):HEAD_DIM * (Q_PER_KV * kh + g + 1)]
                 for g in range(Q_PER_KV)], axis=0)
            s = _nt_dot(qs, k_ref[0, kh, :n_keys, :])
            p = jnp.exp2(s - jnp.max(s, axis=1, keepdims=True))
            ov = _dot(p.astype(BF16), v_ref[0, kh, :n_keys, :])
            o = ov[:, :HEAD_DIM] / ov[:, HEAD_DIM:HEAD_DIM + 1]
            outs += [o[ATT_TQ * g:ATT_TQ * (g + 1)] for g in range(Q_PER_KV)]
        o_ref[0] = jnp.concatenate(outs, axis=1).astype(BF16)

    is_ctx = pl.program_id(1) < CTX // ATT_TQ

    @pl.when(is_ctx)
    def _():
        attend(CTX)

    @pl.when(jnp.logical_not(is_ctx))
    def _():
        attend(NTOK)


def _attention(q, k, v, nb):
    kv = lambda width: pl.BlockSpec((1, N_KV_HEADS, NTOK, width), lambda b, i: (b, 0, 0, 0))
    qo = pl.BlockSpec((1, ATT_TQ, D), lambda b, i: (b, i, 0))
    return pl.pallas_call(
        _attn_kernel,
        grid=(nb, NTOK // ATT_TQ),
        in_specs=[qo, kv(HEAD_DIM), kv(2 * HEAD_DIM)], out_specs=qo,
        out_shape=jax.ShapeDtypeStruct((nb, NTOK, D), BF16),
        compiler_params=_cparams(("parallel", "parallel")),
    )(q.reshape(nb, NTOK, D), k, v)


def _gqa_mixer(h, nb, w_kvq, q_gain, k_gain, w_o):
    proj = _linear(h, w_kvq.astype(BF16), F32)
    q, k, v = _qk_prep(proj, q_gain, k_gain, nb)
    o = _attention(q, k, v, nb)
    return _linear(o.reshape(nb * NTOK, D), w_o.astype(BF16), F32)


def _split3(x):
    hi = x.astype(BF16).astype(F32)
    r = x - hi
    mid = r.astype(BF16).astype(F32)
    lo = (r - mid).astype(BF16).astype(F32)
    return hi, mid, lo


def _log_sigmoid(x):
    return jnp.minimum(x, 0.0) - jnp.log1p(jnp.exp(-jnp.abs(x)))


def _mlstm_kernel(h_ref, k_ref, v_ref, q_ref, wg_ref, wgt_ref, bc_ref, br_ref, o_ref,
                  col_scr, row_scr, st_scr, cl_scr, c0_scr, mloc_scr, m0_scr):
    hb = h_ref[0]
    g_col = _dot(hb, wg_ref[0]) + bc_ref[0]
    g_row = _nt_dot(wgt_ref[0], hb) + br_ref[0]
    ii = lax.broadcasted_iota(jnp.int32, (ML_CHUNK, ML_CHUNK), 0)
    jj = lax.broadcasted_iota(jnp.int32, (ML_CHUNK, ML_CHUNK), 1)
    low = ii >= jj
    upp = ii <= jj
    tri_l, tri_u = low.astype(BF16), upp.astype(BF16)
    tri_lu_rows = jnp.concatenate([tri_l, tri_u], axis=0)
    tri_ul_cols = jnp.concatenate([tri_u, tri_l], axis=1)
    lane_c = lax.broadcasted_iota(jnp.int32, (ML_CHUNK, 8), 1)
    sub_r = lax.broadcasted_iota(jnp.int32, (8, ML_CHUNK), 0)
    for c in range(ML_NCH):
        sl = slice(ML_CHUNK * c, ML_CHUNK * (c + 1))
        gc = g_col[sl]
        parts = _split3(jnp.where(lane_c % 2 == 1, _log_sigmoid(gc), 0.0))
        cs = _dot(tri_lu_rows, jnp.concatenate(parts, axis=1).astype(BF16))
        cs = cs[:, 0:8] + cs[:, 8:16] + cs[:, 16:24]
        cum = jnp.where((lane_c // 2) % 2 == 1, cs[ML_CHUNK:], cs[:ML_CHUNK])
        col_scr[sl, :] = jnp.where(lane_c % 2 == 1, cum, gc)
        gr = g_row[:, sl]
        parts = _split3(jnp.where(sub_r % 2 == 1, _log_sigmoid(gr), 0.0))
        rs = _dot(jnp.concatenate(parts, axis=0).astype(BF16), tri_ul_cols)
        rs = rs[0:8] + rs[8:16] + rs[16:24]
        cum = jnp.where((sub_r // 2) % 2 == 1, rs[:, ML_CHUNK:], rs[:, :ML_CHUNK])
        row_scr[c] = jnp.where(sub_r % 2 == 1, cum, gr)

    ones_col = (lax.broadcasted_iota(jnp.int32, (ML_CHUNK, ML_DV), 1) == 0).astype(BF16)

    def chunk_rows(c):
        return pl.ds(pl.multiple_of(c * ML_CHUNK, ML_CHUNK), ML_CHUNK)

    def gates(c, idx):
        rows, rr = chunk_rows(c), row_scr[c]
        ig_c, cum_c = col_scr[rows, 2 * idx:2 * idx + 1], col_scr[rows, 2 * idx + 1:2 * idx + 2]
        ig_r, cum_r = rr[2 * idx:2 * idx + 1, :], rr[2 * idx + 1:2 * idx + 2, :]
        g = cum_r[:, ML_CHUNK - 1:ML_CHUNK] if idx % 2 == 0 else cum_r[:, 0:1]
        return ig_c, cum_c, ig_r, cum_r, g

    def keys_values(c, hl):
        rows = chunk_rows(c)
        ks = k_ref[0, rows, ML_DQK * hl:ML_DQK * (hl + 1)] * (ML_DQK ** -0.5)
        v_aug = jnp.concatenate([v_ref[0, rows, ML_DV * hl:ML_DV * (hl + 1)], ones_col], axis=1)
        return ks, v_aug

    scalar_tile = lambda m: jnp.broadcast_to(m, (8, 128))

    def local_state(c, carry):
        for hl in range(2):
            ks, v_aug = keys_values(c, hl)
            ke = []
            for d in range(2):
                ig_c, cum_c, _, _, g = gates(c, 2 * hl + d)
                w = g - cum_c + ig_c
                m_loc = jnp.max(w, axis=0, keepdims=True)
                mloc_scr[c, 2 * hl + d] = scalar_tile(m_loc)
                ke.append(ks.astype(F32) * jnp.exp(w - m_loc))
            ke = jnp.concatenate(ke, axis=1).astype(BF16)
            cl_scr[hl, c] = lax.dot_general(ke, v_aug, (((0,), (0,)), ((), ())), preferred_element_type=F32)
        return carry

    lax.fori_loop(0, ML_NCH, local_state, 0)

    st_scr[...] = jnp.zeros(st_scr.shape, F32)

    def recur(i, ms):
        new_ms = []
        for hl in range(2):
            for d in range(2):
                idx = 2 * hl + d
                if d == 0:
                    c = i
                else:
                    c = jnp.where(i < ML_CTX_CH, ML_CTX_CH - 1 - i, ML_NCH + ML_CTX_CH - 1 - i)
                g = gates(c, idx)[4]
                m0, c_aug = ms[idx], st_scr[idx]
                c0_scr[idx, c] = c_aug.astype(BF16)
                m0_scr[c, idx] = scalar_tile(m0)
                m_loc = mloc_scr[c, idx, 0:1, 0:1]
                m_new = jnp.maximum(g + m0, m_loc)
                st_scr[idx] = (jnp.exp(g + m0 - m_new) * c_aug
                               + jnp.exp(m_loc - m_new) * cl_scr[hl, c, ML_DQK * d:ML_DQK * (d + 1), :])
                new_ms.append(m_new)
        return tuple(new_ms)

    lax.fori_loop(0, ML_NCH, recur, tuple(jnp.zeros((1, 1), F32) for _ in range(4)))

    def outputs(c, carry):
        rows = chunk_rows(c)
        for hl in range(2):
            ks, v_aug = keys_values(c, hl)
            q = q_ref[0, rows, ML_DQK * hl:ML_DQK * (hl + 1)]
            raw = _nt_dot(q, ks)
            qc = _dot(q, jnp.concatenate([c0_scr[2 * hl, c], c0_scr[2 * hl + 1, c]], axis=1))
            s, a_inter, m_row = [], [], []
            for d in range(2):
                idx = 2 * hl + d
                ig_c, cum_c, ig_r, cum_r, _ = gates(c, idx)
                log_d = jnp.where(low if d == 0 else upp, cum_c - cum_r + ig_r, -jnp.inf)
                log_inter = cum_c + m0_scr[c, idx, 0:1, 0:1]
                m_row.append(jnp.maximum(log_inter, jnp.max(log_d, axis=1, keepdims=True)))
                s.append(raw * jnp.exp(log_d - m_row[d]))
                a_inter.append(jnp.exp(log_inter - m_row[d]))
            sv = _dot(jnp.concatenate(s, axis=0).astype(BF16), v_aug)
            h = None
            for d in range(2):
                nd = a_inter[d] * qc[:, 2 * ML_DV * d:2 * ML_DV * (d + 1)] + sv[ML_CHUNK * d:ML_CHUNK * (d + 1)]
                den = jnp.maximum(jnp.abs(nd[:, ML_DV:ML_DV + 1]), jnp.exp(-m_row[d]))
                h = nd[:, :ML_DV] / den if h is None else h + nd[:, :ML_DV] / den
            o_ref[0, rows, ML_DV * hl:ML_DV * (hl + 1)] = h
        return carry

    lax.fori_loop(0, ML_NCH, outputs, 0)


def _mlstm_scan(h, proj, wg, wgt, b_col, b_row, nb):
    pairs = ML_HEADS // 2
    k_blk = pl.BlockSpec((1, NTOK, 2 * ML_DQK), lambda b, p: (b, 0, p))
    v_blk = pl.BlockSpec((1, NTOK, 2 * ML_DV), lambda b, p: (b, 0, ML_QK // (2 * ML_DV) + p))
    q_blk = pl.BlockSpec((1, NTOK, 2 * ML_DQK), lambda b, p: (b, 0, (ML_QK + D) // (2 * ML_DQK) + p))
    per_pair = lambda shape: pl.BlockSpec((1,) + shape, lambda b, p: (p, 0, 0))
    proj3 = proj.reshape(nb, NTOK, proj.shape[-1])
    return pl.pallas_call(
        _mlstm_kernel,
        grid=(nb, pairs),
        in_specs=[pl.BlockSpec((1, NTOK, D), lambda b, p: (b, 0, 0)), k_blk, v_blk, q_blk,
                  per_pair((D, 8)), per_pair((8, D)), per_pair((1, 8)), per_pair((8, 1))],
        out_specs=pl.BlockSpec((1, NTOK, 2 * ML_DV), lambda b, p: (b, 0, p)),
        out_shape=jax.ShapeDtypeStruct((nb, NTOK, D), F32),
        scratch_shapes=[pltpu.VMEM((NTOK, 8), F32), pltpu.VMEM((ML_NCH, 8, ML_CHUNK), F32),
                        pltpu.VMEM((4, ML_DQK, 2 * ML_DV), F32),
                        pltpu.VMEM((2, ML_NCH, 2 * ML_DQK, 2 * ML_DV), F32),
                        pltpu.VMEM((4, ML_NCH, ML_DQK, 2 * ML_DV), BF16),
                        pltpu.VMEM((ML_NCH, 4, 8, 128), F32), pltpu.VMEM((ML_NCH, 4, 8, 128), F32)],
        compiler_params=_cparams(("parallel", "parallel")),
    )(h.reshape(nb, NTOK, D), proj3, proj3, proj3, wg, wgt, b_col, b_row)


def _ml_finish_kernel(hh_ref, o_ref, g_ref, w_ref, y_ref):
    parts = [_rms(hh_ref[:, ML_DV * i:ML_DV * (i + 1)]) for i in range(ML_HEADS)]
    hn = jnp.concatenate(parts, axis=1) * g_ref[...]
    z = (hn * jax.nn.sigmoid(o_ref[...].astype(F32))).astype(BF16)
    y_ref[...] = _dot(z, w_ref[...])


def _ml_finish(hh, proj, norm_g, w_out):
    t = hh.shape[0]
    tok = pl.BlockSpec((LIN_TM, D), lambda i: (i, 0))
    return pl.pallas_call(
        _ml_finish_kernel,
        grid=(t // LIN_TM,),
        in_specs=[tok, pl.BlockSpec((LIN_TM, D), lambda i: (i, (ML_QK + D + ML_QK) // D)),
                  pl.BlockSpec((1, D), lambda i: (0, 0)), pl.BlockSpec((D, D), lambda i: (0, 0))],
        out_specs=tok,
        out_shape=jax.ShapeDtypeStruct((t, D), F32),
        compiler_params=_cparams(("parallel",)),
    )(hh, proj, norm_g.reshape(1, D), w_out)


def _mlstm_mixer(h, nb, w_in, b_gate, norm_g, w_out):
    n_state = ML_QK + D
    n_gate = 4 * ML_HEADS
    w_main = jnp.concatenate([w_in[:, :n_state], w_in[:, n_state + n_gate:]], axis=1).astype(BF16)
    pairs = ML_HEADS // 2
    wg = w_in[:, n_state:n_state + n_gate].reshape(D, 2, 2, pairs, 2).transpose(3, 0, 4, 1, 2)
    wg = wg.reshape(pairs, D, 8).astype(BF16)
    bg = b_gate.reshape(2, 2, pairs, 2).transpose(2, 3, 0, 1).reshape(pairs, 8)
    proj = _linear(h, w_main, BF16)
    hh = _mlstm_scan(h, proj, wg, wg.transpose(0, 2, 1), bg[:, None, :], bg[:, :, None], nb)
    return _ml_finish(hh.reshape(nb * NTOK, D), proj, norm_g, w_out.astype(BF16))


PAIRS = tuple((a, b) for a in range(4) for b in range(a + 1, 4))
N_CAT = N_GROUPS * len(PAIRS)


def _route_tables(ridx, t):
    n_tiles = t // MOE_TM + N_CAT
    n_rows = n_tiles * MOE_TM
    lo, hi = ridx[0], ridx[1]
    a, b = lo % 4, hi % 4
    cat = (lo // 4) * len(PAIRS) + (a * (7 - a)) // 2 + (b - a - 1)
    onehot = (cat[:, None] == jnp.arange(N_CAT)[None, :]).astype(jnp.int32)
    rank = jnp.take_along_axis(jnp.cumsum(onehot, axis=0) - onehot, cat[:, None], axis=1)[:, 0]
    counts = jnp.sum(onehot, axis=0)
    padded = ((counts + MOE_TM - 1) // MOE_TM) * MOE_TM
    ends = jnp.cumsum(padded)
    pos = (ends - padded)[cat] + rank
    tok_ids = jnp.arange(t, dtype=jnp.int32)
    src = jnp.zeros((n_rows,), jnp.int32).at[pos].set(tok_ids)
    n_used = ends[-1] // MOE_TM
    tile_start = jnp.minimum(jnp.arange(n_tiles), n_used - 1) * MOE_TM
    tile_cat = jnp.sum((ends[None, :] <= tile_start[:, None]).astype(jnp.int32), axis=1)
    n_valid = jnp.clip((ends - padded + counts)[tile_cat] - tile_start, 0, MOE_TM)
    pa = jnp.array([p[0] for p in PAIRS], jnp.int32)[tile_cat % len(PAIRS)]
    pb = jnp.array([p[1] for p in PAIRS], jnp.int32)[tile_cat % len(PAIRS)]
    base = (tile_cat // len(PAIRS)) * 4
    i32 = lambda a: a.astype(jnp.int32)
    return i32(base + pa), i32(base + pb), i32(n_used).reshape(1), i32(n_valid), src


def _moe_kernel(ea_ref, eb_ref, nu_ref, nv_ref, tok_ref,
                h_hbm, wga_ref, wua_ref, wda_ref, wgb_ref, wub_ref, wdb_ref, y_hbm,
                xbuf, ybuf, bin_scr, gsem, ssem):
    i = pl.program_id(0)
    n_used = nu_ref[0]
    slot = i % 2

    def by_eights(row_fn):
        def body(c, carry):
            for u in range(8):
                row_fn(c * 8 + u, carry)
            return carry
        return body

    def issue_gather(tile, sl):
        def row(r, carry):
            tok = tok_ref[tile * MOE_TM + r]
            pltpu.make_async_copy(h_hbm.at[pl.ds(tok, 1)], xbuf.at[sl, pl.ds(r, 1)], gsem.at[sl]).start()
            return carry
        lax.fori_loop(0, MOE_TM // 8, by_eights(row), 0)

    def wait_gather(sl):
        pltpu.make_async_copy(h_hbm.at[pl.ds(0, MOE_TM)], xbuf.at[sl], gsem.at[sl]).wait()

    def issue_scatter(tile, sl):
        def to_hbm(r, carry):
            tok = tok_ref[tile * MOE_TM + r]
            pltpu.make_async_copy(ybuf.at[sl, pl.ds(r, 1)], y_hbm.at[pl.ds(tok, 1)], ssem.at[sl]).start()
            return carry

        def to_bin(r, carry):
            pltpu.make_async_copy(ybuf.at[sl, pl.ds(r, 1)], bin_scr.at[sl, pl.ds(r, 1)], ssem.at[sl]).start()
            return carry
        nv = nv_ref[tile]
        nv_down = lax.shift_right_logical(nv, 3)
        nv_up = lax.shift_right_logical(nv + 7, 3)
        lax.fori_loop(0, nv_down, by_eights(to_hbm), 0)
        lax.fori_loop(nv_down * 8, nv, to_hbm, 0)
        lax.fori_loop(nv, nv_up * 8, to_bin, 0)
        lax.fori_loop(nv_up, MOE_TM // 8, by_eights(to_bin), 0)

    def wait_scatter(sl):
        pltpu.make_async_copy(ybuf.at[sl], y_hbm.at[pl.ds(0, MOE_TM)], ssem.at[sl]).wait()

    @pl.when(i == 0)
    def _():
        issue_gather(0, 0)

    @pl.when(i + 1 < n_used)
    def _():
        issue_gather(i + 1, 1 - slot)

    @pl.when(i < n_used)
    def _():
        wait_gather(slot)

        @pl.when(i >= 2)
        def _():
            wait_scatter(slot)

        x = xbuf[slot, :, :D].astype(BF16)
        w = xbuf[slot, :, D:]

        def expert(wg_ref, wu_ref, wd_ref):
            a = _dot(x, wg_ref[0])
            hid = (a * jax.nn.sigmoid(a) * _dot(x, wu_ref[0])).astype(BF16)
            return _dot(hid, wd_ref[0])

        ybuf[slot] = (w[:, 0:1] * expert(wga_ref, wua_ref, wda_ref)
                      + w[:, 1:2] * expert(wgb_ref, wub_ref, wdb_ref))
        issue_scatter(i, slot)

    @pl.when(i == pl.num_programs(0) - 1)
    def _():
        wait_scatter(0)
        wait_scatter(1)


def _moe(h2, ridx, w_gate, w_up, w_down):
    t = h2.shape[0]
    ea, eb, n_used, n_valid, src = _route_tables(ridx, t)
    n_tiles = ea.shape[0]
    assert t // MOE_TM >= 2
    wspec = lambda which, shape: pl.BlockSpec(
        (1,) + shape, (lambda i, ea, eb, nu, nv, tok: (ea[i], 0, 0)) if which == 0
        else (lambda i, ea, eb, nu, nv, tok: (eb[i], 0, 0)))
    wg, wu, wd = w_gate.astype(BF16), w_up.astype(BF16), w_down.astype(BF16)
    return pl.pallas_call(
        _moe_kernel,
        grid_spec=pltpu.PrefetchScalarGridSpec(
            num_scalar_prefetch=5,
            grid=(n_tiles,),
            in_specs=[pl.BlockSpec(memory_space=pl.ANY),
                      wspec(0, (D, D_FF)), wspec(0, (D, D_FF)), wspec(0, (D_FF, D)),
                      wspec(1, (D, D_FF)), wspec(1, (D, D_FF)), wspec(1, (D_FF, D))],
            out_specs=pl.BlockSpec(memory_space=pl.ANY),
            scratch_shapes=[pltpu.VMEM((2, MOE_TM, MOE_ROW), F32), pltpu.VMEM((2, MOE_TM, D), F32),
                            pltpu.VMEM((2, MOE_TM, D), F32),
                            pltpu.SemaphoreType.DMA((2,)), pltpu.SemaphoreType.DMA((2,))]),
        out_shape=jax.ShapeDtypeStruct((t, D), F32),
        compiler_params=_cparams(("arbitrary",)),
    )(ea, eb, n_used, n_valid, src, h2, wg, wu, wd, wg, wu, wd)


def kernel(x, c, ctx, c_ctx, ada_w, ada_b, norm1_g, norm2_g, final_g, s5_lambda_re, s5_lambda_im, s5_log_step, s5_b_re, s5_b_im, s5_c_re, s5_c_im, s5_d, s5_w_glu, s5_b_glu, attn_w_kvq, attn_q_gain, attn_k_gain, attn_w_o, ml_w_in, ml_b_gate, ml_norm_g, ml_w_out, router_w, router_bias, moe_w_gate, moe_w_up, moe_w_down):
    nb = x.shape[0]
    depth = ada_w.shape[0]
    assert x.shape[1:] == (SEQ, D) and ctx.shape[1:] == (CTX, D) and nb + 1 <= 16
    xs = jnp.concatenate([ctx, x], axis=1).reshape(nb * NTOK, D)
    cond_in = jnp.zeros((16, D), F32).at[:nb].set(c).at[nb].set(c_ctx)
    mods = _ada(cond_in, ada_w, ada_b).reshape(depth, 16, 6, D)

    y2 = None
    for layer in range(depth):
        kind, j = layer % N_MIXERS, layer // N_MIXERS
        xs, h = _pre(xs, y2, mods[layer - 1] if layer else None, mods[layer], norm1_g[layer], nb,
                     F32 if kind == 0 else BF16, chunk_major=kind == 0)
        if kind == 0:
            y = _s5_mixer(h, nb, s5_lambda_re[j], s5_lambda_im[j], s5_log_step[j], s5_b_re[j], s5_b_im[j],
                          s5_c_re[j], s5_c_im[j], s5_d[j], s5_w_glu[j], s5_b_glu[j])
        elif kind == 1:
            y = _gqa_mixer(h, nb, attn_w_kvq[j], attn_q_gain[j], attn_k_gain[j], attn_w_o[j])
        else:
            y = _mlstm_mixer(h, nb, ml_w_in[j], ml_b_gate[j], ml_norm_g[j], ml_w_out[j])
        xs, h2, ridx = _post(xs, y, mods[layer], norm2_g[layer], router_w, router_bias, nb)
        y2 = _moe(h2, ridx, moe_w_gate[layer], moe_w_up[layer], moe_w_down[layer])
    out = _final(xs, y2, mods[depth - 1], final_g, nb)
    return out.reshape(nb, SEQ, D)
```

```python
import functools
import math

import jax
import jax.numpy as jnp
import numpy as np
from jax import lax
from jax.experimental import pallas as pl
from jax.experimental.pallas import tpu as pltpu

F32 = jnp.float32
BF16 = jnp.bfloat16

D = 1024
CTX = 256
SEQ = 2048
NTOK = CTX + SEQ
GRID_W = 64
EPS = 1e-6
N_MIXERS = 3

TOK = 256
TILES_PER_B = NTOK // TOK
LIN_TM = 512

S5_GROUP = 16
S5_GROUPS = D // S5_GROUP
S5_STATE = 64
S5_CHUNK = 16
S5_NCH = NTOK // S5_CHUNK
S5_CTX_CH = CTX // S5_CHUNK
S5_SLAB_GROUPS = 128 // S5_GROUP

HEAD_DIM = 64
N_Q_HEADS = 16
N_KV_HEADS = 4
Q_PER_KV = 4
KV_COLS = 2 * N_KV_HEADS * HEAD_DIM
ROPE_THETA = 10000.0
ATT_TQ = 128

ML_HEADS = 8
ML_DQK = 64
ML_DV = 128
ML_QK = 512
ML_CHUNK = 128
ML_NCH = NTOK // ML_CHUNK
ML_CTX_CH = CTX // ML_CHUNK

N_EXPERTS = 16
N_GROUPS = 4
D_FF = 512
MOE_TM = 256
MOE_ROW = D + 128

VMEM_LIMIT = 56 * 1024 * 1024


def _cparams(sem):
    return pltpu.CompilerParams(dimension_semantics=sem, vmem_limit_bytes=VMEM_LIMIT)


def _nt_dot(a, b, **kw):
    return lax.dot_general(a, b, (((1,), (1,)), ((), ())), preferred_element_type=F32, **kw)


def _dot(a, b):
    return jnp.dot(a, b, preferred_element_type=F32)


def _rms(x):
    return x * lax.rsqrt(jnp.mean(x * x, axis=-1, keepdims=True) + EPS)


def _ada_kernel(c_ref, w_ref, b_ref, o_ref):
    c = c_ref[...]
    cond = (c * jax.nn.sigmoid(c)).astype(BF16)
    o_ref[0] = _dot(cond, w_ref[0].astype(BF16)) + b_ref[0]


def _ada(cond_in, ada_w, ada_b):
    depth, _, n = ada_w.shape
    tn = 1536
    rows = cond_in.shape[0]
    return pl.pallas_call(
        _ada_kernel,
        grid=(depth, n // tn),
        in_specs=[pl.BlockSpec((rows, D), lambda l, j: (0, 0)),
                  pl.BlockSpec((1, D, tn), lambda l, j: (l, 0, j)),
                  pl.BlockSpec((1, 1, tn), lambda l, j: (l, 0, j))],
        out_specs=pl.BlockSpec((1, rows, tn), lambda l, j: (l, 0, j)),
        out_shape=jax.ShapeDtypeStruct((depth, rows, n), F32),
        compiler_params=_cparams(("parallel", "parallel")),
    )(cond_in, ada_w, ada_b.reshape(depth, 1, n))


def _mod_row(nb):
    return lambda i: (jnp.where(i % TILES_PER_B == 0, nb, i // TILES_PER_B), 0, 0)


def _pre_kernel(has_res, *refs):
    if has_res:
        x_ref, y_ref, mprev_ref, mcur_ref, g_ref, xo_ref, h_ref = refs
        x = x_ref[...] + mprev_ref[0, 5:6, :] * y_ref[...]
        xo_ref[...] = x
    else:
        x_ref, mcur_ref, g_ref, h_ref = refs
        x = x_ref[...]
    h = _rms(x) * g_ref[...]
    h = h * (1.0 + mcur_ref[0, 1:2, :]) + mcur_ref[0, 0:1, :]
    h_ref[...] = h.astype(h_ref.dtype).reshape(h_ref.shape)


def _chunk_major(nb):
    shape = (S5_NCH, nb, S5_CHUNK, D)
    spec = pl.BlockSpec((TOK // S5_CHUNK, 1, S5_CHUNK, D), lambda i: (i % TILES_PER_B, i // TILES_PER_B, 0, 0))
    return shape, spec


def _pre(x, y2, mods_prev, mods_cur, g, nb, h_dtype, chunk_major=False):
    t = x.shape[0]
    tok = pl.BlockSpec((TOK, D), lambda i: (i, 0))
    mod = pl.BlockSpec((1, 6, D), _mod_row(nb))
    vec = pl.BlockSpec((1, D), lambda i: (0, 0))
    h_shape, h_spec = _chunk_major(nb) if chunk_major else ((t, D), tok)
    if y2 is None:
        return x, pl.pallas_call(
            functools.partial(_pre_kernel, False),
            grid=(t // TOK,), in_specs=[tok, mod, vec], out_specs=h_spec,
            out_shape=jax.ShapeDtypeStruct(h_shape, h_dtype),
            compiler_params=_cparams(("parallel",)),
        )(x, mods_cur, g.reshape(1, D))
    return pl.pallas_call(
        functools.partial(_pre_kernel, True),
        grid=(t // TOK,), in_specs=[tok, tok, mod, mod, vec], out_specs=[tok, h_spec],
        out_shape=[jax.ShapeDtypeStruct((t, D), F32), jax.ShapeDtypeStruct(h_shape, h_dtype)],
        compiler_params=_cparams(("parallel",)),
    )(x, y2, mods_prev, mods_cur, g.reshape(1, D))


def _post_kernel(x_ref, y_ref, m_ref, g_ref, rwt_ref, rb_ref, xo_ref, h_ref, ridx_ref):
    x = x_ref[...] + m_ref[0, 2:3, :] * y_ref[...].reshape(x_ref.shape)
    xo_ref[...] = x
    h2 = _rms(x) * g_ref[...]
    h2 = h2 * (1.0 + m_ref[0, 4:5, :]) + m_ref[0, 3:4, :]
    h_ref[:, :D] = h2

    logits = _nt_dot(rwt_ref[...], h2, precision=lax.Precision.HIGHEST)
    scores = jax.nn.sigmoid(logits)
    sel = scores + rb_ref[...]
    row = lax.broadcasted_iota(jnp.int32, sel.shape, 0)
    per_group = N_EXPERTS // N_GROUPS
    best_val, best = None, None
    for g in range(N_GROUPS):
        r = [sel[per_group * g + j:per_group * g + j + 1, :] for j in range(per_group)]
        gs = None
        for i in range(per_group):
            for j in range(i + 1, per_group):
                gs = r[i] + r[j] if gs is None else jnp.maximum(gs, r[i] + r[j])
        if g == 0:
            best_val, best = gs, jnp.zeros(gs.shape, jnp.int32)
        else:
            upd = gs > best_val
            best = jnp.where(upd, g, best)
            best_val = jnp.where(upd, gs, best_val)
    masked = jnp.where((row // per_group) == best, sel, -jnp.inf)
    m1 = jnp.max(masked, axis=0, keepdims=True)
    i1 = jnp.min(jnp.where(masked == m1, row, N_EXPERTS), axis=0, keepdims=True)
    masked2 = jnp.where(row == i1, -jnp.inf, masked)
    m2 = jnp.max(masked2, axis=0, keepdims=True)
    i2 = jnp.min(jnp.where(masked2 == m2, row, N_EXPERTS), axis=0, keepdims=True)
    lo, hi = jnp.minimum(i1, i2), jnp.maximum(i1, i2)
    s_lo = jnp.sum(jnp.where(row == lo, scores, 0.0), axis=0, keepdims=True)
    s_hi = jnp.sum(jnp.where(row == hi, scores, 0.0), axis=0, keepdims=True)
    ridx_ref[...] = jnp.concatenate([lo, hi], axis=0)
    wts = jnp.concatenate([s_lo, s_hi, jnp.zeros((126, TOK), F32)], axis=0) / (s_lo + s_hi)
    h_ref[:, D:] = wts.T


def _post(x, y, mods, g, router_w, router_bias, nb):
    t = x.shape[0]
    tok = pl.BlockSpec((TOK, D), lambda i: (i, 0))
    return pl.pallas_call(
        _post_kernel,
        grid=(t // TOK,),
        in_specs=[tok, _chunk_major(nb)[1] if y.ndim == 4 else tok, pl.BlockSpec((1, 6, D), _mod_row(nb)),
                  pl.BlockSpec((1, D), lambda i: (0, 0)),
                  pl.BlockSpec((N_EXPERTS, D), lambda i: (0, 0)),
                  pl.BlockSpec((N_EXPERTS, 1), lambda i: (0, 0))],
        out_specs=[tok, pl.BlockSpec((TOK, MOE_ROW), lambda i: (i, 0)), pl.BlockSpec((2, TOK), lambda i: (0, i))],
        out_shape=[jax.ShapeDtypeStruct((t, D), F32), jax.ShapeDtypeStruct((t, MOE_ROW), F32),
                   jax.ShapeDtypeStruct((2, t), jnp.int32)],
        compiler_params=_cparams(("parallel",)),
    )(x, y, mods, g.reshape(1, D), router_w.T, router_bias.reshape(N_EXPERTS, 1))


def _final_kernel(x_ref, y_ref, m_ref, g_ref, o_ref):
    x = x_ref[...] + m_ref[0, 5:6, :] * y_ref[...]
    o_ref[...] = _rms(x) * g_ref[...]


def _final(x, y2, mods, g, nb):
    lat_tiles = SEQ // TOK
    src = lambda j: ((j // lat_tiles) * TILES_PER_B + CTX // TOK + j % lat_tiles, 0)
    tok_in = pl.BlockSpec((TOK, D), src)
    return pl.pallas_call(
        _final_kernel,
        grid=(nb * lat_tiles,),
        in_specs=[tok_in, tok_in, pl.BlockSpec((1, 6, D), lambda j: (j // lat_tiles, 0, 0)),
                  pl.BlockSpec((1, D), lambda j: (0, 0))],
        out_specs=pl.BlockSpec((TOK, D), lambda j: (j, 0)),
        out_shape=jax.ShapeDtypeStruct((nb * SEQ, D), F32),
        compiler_params=_cparams(("parallel",)),
    )(x, y2, mods, g.reshape(1, D))


def _linear_kernel(x_ref, w_ref, o_ref):
    o_ref[...] = _dot(x_ref[...].astype(BF16), w_ref[...]).astype(o_ref.dtype)


def _linear(x, w, out_dtype):
    t, k = x.shape
    n = w.shape[1]
    return pl.pallas_call(
        _linear_kernel,
        grid=(t // LIN_TM,),
        in_specs=[pl.BlockSpec((LIN_TM, k), lambda i: (i, 0)), pl.BlockSpec((k, n), lambda i: (0, 0))],
        out_specs=pl.BlockSpec((LIN_TM, n), lambda i: (i, 0)),
        out_shape=jax.ShapeDtypeStruct((t, n), out_dtype),
        compiler_params=_cparams(("parallel",)),
    )(x, w)


def _s5_operators(lam_re, lam_im, log_step, b_re, b_im, c_re, c_im):
    n = S5_CHUNK
    dt = jnp.exp(log_step)[None, :, :, None]
    k = jnp.arange(n + 1, dtype=F32)[:, None, None, None]
    mag = jnp.exp(k * lam_re[None] * dt)
    ang = k * lam_im[None] * dt
    pw_re, pw_im = mag * jnp.cos(ang), mag * jnp.sin(ang)
    a_re, a_im = pw_re[1], pw_im[1]
    den = lam_re * lam_re + lam_im * lam_im
    n_re, n_im = a_re - 1.0, a_im
    f_re = (n_re * lam_re + n_im * lam_im) / den
    f_im = (n_im * lam_re - n_re * lam_im) / den
    bt_re, bt_im = b_re.swapaxes(-1, -2), b_im.swapaxes(-1, -2)
    bb_re = f_re[:, :, None, :] * bt_re - f_im[:, :, None, :] * bt_im
    bb_im = f_re[:, :, None, :] * bt_im + f_im[:, :, None, :] * bt_re

    cp_re = c_re[None] * pw_re[:, :, :, None, :] - c_im[None] * pw_im[:, :, :, None, :]
    cp_im = c_re[None] * pw_im[:, :, :, None, :] + c_im[None] * pw_re[:, :, :, None, :]
    kk = (jnp.einsum('dgcp,kdgop->dgcko', bb_re, cp_re[:n]) - jnp.einsum('dgcp,kdgop->dgcko', bb_im, cp_im[:n]))
    wide = n * S5_GROUP
    kf = jnp.pad(kk[0].reshape(S5_GROUPS, S5_GROUP, wide), ((0, 0), (0, 0), (wide - S5_GROUP, 0)))
    kb = jnp.pad(kk[1, :, :, ::-1].reshape(S5_GROUPS, S5_GROUP, wide), ((0, 0), (0, 0), (0, wide - S5_GROUP)))
    toep = jnp.stack([kf[:, :, wide - S5_GROUP * (s + 1):2 * wide - S5_GROUP * (s + 1)]
                      + kb[:, :, S5_GROUP * (n - 1 - s):S5_GROUP * (n - 1 - s) + wide] for s in range(n)], axis=1)
    toep = toep.reshape(S5_GROUPS, wide, wide)

    def state_in(pw_sel, d):
        pr = pw_re[pw_sel, d].swapaxes(0, 1)[:, :, None, :]
        pi = pw_im[pw_sel, d].swapaxes(0, 1)[:, :, None, :]
        e_re = pr * bb_re[d][:, None] - pi * bb_im[d][:, None]
        e_im = pr * bb_im[d][:, None] + pi * bb_re[d][:, None]
        to_w = lambda e: jnp.pad(e.reshape(S5_GROUPS, wide, S5_STATE), ((0, 0), (0, 0), (0, 128 - S5_STATE)))
        return to_w(e_re), to_w(e_im)

    def state_out(pw_sel, d):
        ct_re = jnp.tile(c_re[d].swapaxes(-1, -2), (1, 1, n))
        ct_im = jnp.tile(c_im[d].swapaxes(-1, -2), (1, 1, n))
        pr = jnp.repeat(pw_re[pw_sel, d].transpose(1, 2, 0), S5_GROUP, axis=2)
        pi = jnp.repeat(pw_im[pw_sel, d].transpose(1, 2, 0), S5_GROUP, axis=2)
        to_w = lambda o: jnp.pad(o, ((0, 0), (0, 128 - S5_STATE), (0, 0)))
        return to_w(ct_re * pr - ct_im * pi), to_w(-(ct_re * pi + ct_im * pr))

    ar = jnp.arange(n)
    w1 = jnp.concatenate([toep, *state_in(n - 1 - ar, 0), *state_in(ar, 1)], axis=2)
    w2 = jnp.concatenate([*state_out(ar + 1, 0), *state_out(n - ar, 1)], axis=1)
    pad = lambda a: jnp.pad(a, ((0, 0), (0, 128 - S5_STATE)))
    a16 = jnp.stack([pad(pw_re[n, 0]), pad(pw_im[n, 0]), pad(pw_re[n, 1]), pad(pw_im[n, 1])], axis=1)
    return w1.astype(BF16), w2.astype(BF16), a16


def _s5_select():
    sel = np.zeros((S5_SLAB_GROUPS, S5_CHUNK * 128, S5_CHUNK * S5_GROUP), np.float32)
    s, c = np.meshgrid(np.arange(S5_CHUNK), np.arange(S5_GROUP), indexing="ij")
    for j in range(S5_SLAB_GROUPS):
        sel[j, s * 128 + S5_GROUP * j + c, s * S5_GROUP + c] = 1.0
    return jnp.asarray(sel, BF16)


def _s5_kernel(nb, h_ref, sel_ref, w1_ref, w2_ref, a_ref, y_ref, x2_scr, s_scr, x_scr):
    rows = S5_NCH * nb

    @pl.when(pl.program_id(1) == 0)
    def _():
        for s in range(S5_CHUNK):
            x2_scr[:, 128 * s:128 * (s + 1)] = h_ref[pl.ds(s, rows, stride=S5_CHUNK), :].astype(BF16)
        y_ref[...] = jnp.zeros(y_ref.shape, F32)

    u = _dot(x2_scr[...], sel_ref[0]).astype(BF16)
    s_scr[...] = _dot(u, w1_ref[0])

    def step(i, carry):
        cf = i
        cb = jnp.where(i < S5_CTX_CH, S5_CTX_CH - 1 - i, S5_NCH + S5_CTX_CH - 1 - i)
        rows_f = pl.ds(pl.multiple_of(cf * nb, nb), nb)
        rows_b = pl.ds(pl.multiple_of(cb * nb, nb), nb)
        fr, fi, br, bi = carry
        x_scr[rows_f, 0:128] = fr
        x_scr[rows_f, 128:256] = fi
        x_scr[rows_b, 256:384] = br
        x_scr[rows_b, 384:512] = bi
        ar, ai = a_ref[0, 0:1, :], a_ref[0, 1:2, :]
        nfr = ar * fr - ai * fi + s_scr[rows_f, 256:384]
        nfi = ar * fi + ai * fr + s_scr[rows_f, 384:512]
        ar, ai = a_ref[0, 2:3, :], a_ref[0, 3:4, :]
        nbr = ar * br - ai * bi + s_scr[rows_b, 512:640]
        nbi = ar * bi + ai * br + s_scr[rows_b, 640:768]
        return nfr, nfi, nbr, nbi

    zero = jnp.zeros((nb, 128), F32)
    lax.fori_loop(0, S5_NCH, step, (zero, zero, zero, zero))
    yg = s_scr[:, 0:256] + _dot(x_scr[...].astype(BF16), w2_ref[0])
    y_ref[0] += _nt_dot(yg.astype(BF16), sel_ref[0])


def _s5_scan(h_cm, w1, w2, a16, nb):
    rows = S5_NCH * nb
    kc = S5_CHUNK * S5_GROUP
    n_slabs = D // 128
    per_group = lambda shape: pl.BlockSpec((1,) + shape, lambda q, j: (q * S5_SLAB_GROUPS + j, 0, 0))
    return pl.pallas_call(
        functools.partial(_s5_kernel, nb),
        grid=(n_slabs, S5_SLAB_GROUPS),
        in_specs=[pl.BlockSpec((nb * NTOK, 128), lambda q, j: (0, q)),
                  pl.BlockSpec((1, S5_CHUNK * 128, kc), lambda q, j: (j, 0, 0)),
                  per_group((kc, 768)), per_group((512, kc)), per_group((4, 128))],
        out_specs=pl.BlockSpec((1, rows, S5_CHUNK * 128), lambda q, j: (q, 0, 0)),
        out_shape=jax.ShapeDtypeStruct((n_slabs, rows, S5_CHUNK * 128), F32),
        scratch_shapes=[pltpu.VMEM((rows, S5_CHUNK * 128), BF16), pltpu.VMEM((rows, 768), F32),
                        pltpu.VMEM((rows, 512), F32)],
        compiler_params=_cparams(("parallel", "arbitrary")),
    )(h_cm.reshape(nb * NTOK, D), _s5_select(), w1, w2, a16)


def _gelu(x):
    return 0.5 * x * (1.0 + jnp.tanh(math.sqrt(2.0 / math.pi) * (x + 0.044715 * (x * x * x))))


def _glu_kernel(y_ref, h_ref, d_ref, w_ref, b_ref, o_ref, ys_scr):
    rows = TOK // S5_CHUNK
    for q in range(D // 128):
        for s in range(S5_CHUNK):
            ys_scr[q, pl.ds(s, rows, stride=S5_CHUNK), :] = y_ref[q, :, 128 * s:128 * (s + 1)]
    y = jnp.concatenate([ys_scr[q] for q in range(D // 128)], axis=1)
    z = _gelu(y + d_ref[...] * h_ref[...]).astype(BF16)
    r = _dot(z, w_ref[...]) + b_ref[...]
    o_ref[...] = r[:, :D] * jax.nn.sigmoid(r[:, D:])


def _glu(y, h, d_skip, w, b):
    t = h.shape[0]
    tok = pl.BlockSpec((TOK, D), lambda i: (i, 0))
    return pl.pallas_call(
        _glu_kernel,
        grid=(t // TOK,),
        in_specs=[pl.BlockSpec((D // 128, TOK // S5_CHUNK, S5_CHUNK * 128), lambda i: (0, i, 0)), tok,
                  pl.BlockSpec((1, D), lambda i: (0, 0)),
                  pl.BlockSpec((D, 2 * D), lambda i: (0, 0)), pl.BlockSpec((1, 2 * D), lambda i: (0, 0))],
        out_specs=tok,
        out_shape=jax.ShapeDtypeStruct((t, D), F32),
        scratch_shapes=[pltpu.VMEM((D // 128, TOK, 128), F32)],
        compiler_params=_cparams(("parallel",)),
    )(y, h, d_skip.reshape(1, D), w, b.reshape(1, 2 * D))


def _s5_mixer(h_cm, nb, lam_re, lam_im, log_step, b_re, b_im, c_re, c_im, d_skip, w_glu, b_glu):
    w1, w2, a16 = _s5_operators(lam_re, lam_im, log_step, b_re, b_im, c_re, c_im)
    y = _s5_scan(h_cm, w1, w2, a16, nb)
    y = _glu(y, h_cm.reshape(nb * NTOK, D), d_skip, w_glu.astype(BF16), b_glu)
    return y.reshape(h_cm.shape)


def _rope_tables():
    rows = SEQ // GRID_W
    row = jnp.repeat(jnp.arange(rows), GRID_W).astype(F32)
    col = jnp.tile(jnp.arange(GRID_W), rows).astype(F32)
    half = HEAD_DIM // 4
    inv_freq = ROPE_THETA ** (-jnp.arange(half, dtype=F32) / half)
    ang_r, ang_c = row[:, None] * inv_freq, col[:, None] * inv_freq
    cos = jnp.concatenate([jnp.cos(ang_r)] * 2 + [jnp.cos(ang_c)] * 2, axis=1)
    sin = jnp.concatenate([-jnp.sin(ang_r), jnp.sin(ang_r), -jnp.sin(ang_c), jnp.sin(ang_c)], axis=1)
    cos = jnp.concatenate([jnp.ones((CTX, HEAD_DIM), F32), cos], axis=0)
    sin = jnp.concatenate([jnp.zeros((CTX, HEAD_DIM), F32), sin], axis=0)
    return jnp.tile(cos, (1, 2)), jnp.tile(sin, (1, 2))


def _qk_prep_kernel(p_ref, cos_ref, sin_ref, qg_ref, kg_ref, q_ref, k_ref, v_ref):
    lane = lax.broadcasted_iota(jnp.int32, (TOK, 128), 1)
    low_head = lane < HEAD_DIM
    first_half = (lane % 32) < 16
    cos, sin = cos_ref[...], sin_ref[...]

    def norm_rope(x, gain):
        sq = x * x
        s_lo = jnp.sum(jnp.where(low_head, sq, 0.0), axis=1, keepdims=True)
        s_hi = jnp.sum(sq, axis=1, keepdims=True) - s_lo
        inv = lax.rsqrt(jnp.where(low_head, s_lo, s_hi) * (1.0 / HEAD_DIM) + EPS)
        xn = x * inv * gain
        partner = jnp.where(first_half, pltpu.roll(xn, 128 - 16, 1), pltpu.roll(xn, 16, 1))
        return xn * cos + partner * sin

    ones_col = (lax.broadcasted_iota(jnp.int32, (TOK, HEAD_DIM), 1) == 0).astype(BF16)
    for j in range(KV_COLS // 2 // 128):
        kt = norm_rope(p_ref[:, 128 * j:128 * (j + 1)], kg_ref[...]).astype(BF16)
        vt = p_ref[:, KV_COLS // 2 + 128 * j:KV_COLS // 2 + 128 * (j + 1)].astype(BF16)
        for hh in range(2):
            k_ref[0, 2 * j + hh] = kt[:, HEAD_DIM * hh:HEAD_DIM * (hh + 1)]
            v_ref[0, 2 * j + hh] = jnp.concatenate([vt[:, HEAD_DIM * hh:HEAD_DIM * (hh + 1)], ones_col], axis=1)
    for j in range(D // 128):
        qt = norm_rope(p_ref[:, KV_COLS + 128 * j:KV_COLS + 128 * (j + 1)], qg_ref[...])
        q_ref[:, 128 * j:128 * (j + 1)] = (qt * (HEAD_DIM ** -0.5 * math.log2(math.e))).astype(BF16)


def _qk_prep(proj, q_gain, k_gain, nb):
    t, ncol = proj.shape
    cos, sin = _rope_tables()
    tab = pl.BlockSpec((TOK, 128), lambda i: (i % TILES_PER_B, 0))
    gain = pl.BlockSpec((1, 128), lambda i: (0, 0))
    head_major = lambda width: pl.BlockSpec((1, N_KV_HEADS, TOK, width),
                                            lambda i: (i // TILES_PER_B, 0, i % TILES_PER_B, 0))
    hm_shape = lambda width: jax.ShapeDtypeStruct((nb, N_KV_HEADS, NTOK, width), BF16)
    return pl.pallas_call(
        _qk_prep_kernel,
        grid=(t // TOK,),
        in_specs=[pl.BlockSpec((TOK, ncol), lambda i: (i, 0)), tab, tab, gain, gain],
        out_specs=[pl.BlockSpec((TOK, D), lambda i: (i, 0)), head_major(HEAD_DIM), head_major(2 * HEAD_DIM)],
        out_shape=[jax.ShapeDtypeStruct((t, D), BF16), hm_shape(HEAD_DIM), hm_shape(2 * HEAD_DIM)],
        compiler_params=_cparams(("parallel",)),
    )(proj, cos, sin, jnp.tile(q_gain, 2).reshape(1, 128), jnp.tile(k_gain, 2).reshape(1, 128))


def _attn_kernel(q_ref, k_ref, v_ref, o_ref):
    def attend(n_keys):
        outs = []
        for kh in range(N_KV_HEADS):
            qs = jnp.concatenate(
                [q_ref[0, :, HEAD_DIM * (Q_PER_KV * kh + g):HEAD_DIM * (Q_PER_KV * kh + g + 1)]
                 for g in range(Q_PER_KV)], axis=0)
            s = _nt_dot(qs, k_ref[0, kh, :n_keys, :])
            p = jnp.exp2(s - jnp.max(s, axis=1, keepdims=True))
            ov = _dot(p.astype(BF16), v_ref[0, kh, :n_keys, :])
            o = ov[:, :HEAD_DIM] / ov[:, HEAD_DIM:HEAD_DIM + 1]
            outs += [o[ATT_TQ * g:ATT_TQ * (g + 1)] for g in range(Q_PER_KV)]
        o_ref[0] = jnp.concatenate(outs, axis=1).astype(BF16)

    is_ctx = pl.program_id(1) < CTX // ATT_TQ

    @pl.when(is_ctx)
    def _():
        attend(CTX)

    @pl.when(jnp.logical_not(is_ctx))
    def _():
        attend(NTOK)


def _attention(q, k, v, nb):
    kv = lambda width: pl.BlockSpec((1, N_KV_HEADS, NTOK, width), lambda b, i: (b, 0, 0, 0))
    qo = pl.BlockSpec((1, ATT_TQ, D), lambda b, i: (b, i, 0))
    return pl.pallas_call(
        _attn_kernel,
        grid=(nb, NTOK // ATT_TQ),
        in_specs=[qo, kv(HEAD_DIM), kv(2 * HEAD_DIM)], out_specs=qo,
        out_shape=jax.ShapeDtypeStruct((nb, NTOK, D), BF16),
        compiler_params=_cparams(("parallel", "parallel")),
    )(q.reshape(nb, NTOK, D), k, v)


def _gqa_mixer(h, nb, w_kvq, q_gain, k_gain, w_o):
    proj = _linear(h, w_kvq.astype(BF16), F32)
    q, k, v = _qk_prep(proj, q_gain, k_gain, nb)
    o = _attention(q, k, v, nb)
    return _linear(o.reshape(nb * NTOK, D), w_o.astype(BF16), F32)


def _split3(x):
    hi = x.astype(BF16).astype(F32)
    r = x - hi
    mid = r.astype(BF16).astype(F32)
    lo = (r - mid).astype(BF16).astype(F32)
    return hi, mid, lo


def _log_sigmoid(x):
    return jnp.minimum(x, 0.0) - jnp.log1p(jnp.exp(-jnp.abs(x)))


def _mlstm_kernel(h_ref, k_ref, v_ref, q_ref, wg_ref, wgt_ref, bc_ref, br_ref, o_ref,
                  col_scr, row_scr, st_scr, cl_scr, c0_scr, mloc_scr, m0_scr):
    hb = h_ref[0]
    g_col = _dot(hb, wg_ref[0]) + bc_ref[0]
    g_row = _nt_dot(wgt_ref[0], hb) + br_ref[0]
    ii = lax.broadcasted_iota(jnp.int32, (ML_CHUNK, ML_CHUNK), 0)
    jj = lax.broadcasted_iota(jnp.int32, (ML_CHUNK, ML_CHUNK), 1)
    low = ii >= jj
    upp = ii <= jj
    tri_l, tri_u = low.astype(BF16), upp.astype(BF16)
    tri_lu_rows = jnp.concatenate([tri_l, tri_u], axis=0)
    tri_ul_cols = jnp.concatenate([tri_u, tri_l], axis=1)
    lane_c = lax.broadcasted_iota(jnp.int32, (ML_CHUNK, 8), 1)
    sub_r = lax.broadcasted_iota(jnp.int32, (8, ML_CHUNK), 0)
    for c in range(ML_NCH):
        sl = slice(ML_CHUNK * c, ML_CHUNK * (c + 1))
        gc = g_col[sl]
        parts = _split3(jnp.where(lane_c % 2 == 1, _log_sigmoid(gc), 0.0))
        cs = _dot(tri_lu_rows, jnp.concatenate(parts, axis=1).astype(BF16))
        cs = cs[:, 0:8] + cs[:, 8:16] + cs[:, 16:24]
        cum = jnp.where((lane_c // 2) % 2 == 1, cs[ML_CHUNK:], cs[:ML_CHUNK])
        col_scr[sl, :] = jnp.where(lane_c % 2 == 1, cum, gc)
        gr = g_row[:, sl]
        parts = _split3(jnp.where(sub_r % 2 == 1, _log_sigmoid(gr), 0.0))
        rs = _dot(jnp.concatenate(parts, axis=0).astype(BF16), tri_ul_cols)
        rs = rs[0:8] + rs[8:16] + rs[16:24]
        cum = jnp.where((sub_r // 2) % 2 == 1, rs[:, ML_CHUNK:], rs[:, :ML_CHUNK])
        row_scr[c] = jnp.where(sub_r % 2 == 1, cum, gr)

    ones_col = (lax.broadcasted_iota(jnp.int32, (ML_CHUNK, ML_DV), 1) == 0).astype(BF16)

    def chunk_rows(c):
        return pl.ds(pl.multiple_of(c * ML_CHUNK, ML_CHUNK), ML_CHUNK)

    def gates(c, idx):
        rows, rr = chunk_rows(c), row_scr[c]
        ig_c, cum_c = col_scr[rows, 2 * idx:2 * idx + 1], col_scr[rows, 2 * idx + 1:2 * idx + 2]
        ig_r, cum_r = rr[2 * idx:2 * idx + 1, :], rr[2 * idx + 1:2 * idx + 2, :]
        g = cum_r[:, ML_CHUNK - 1:ML_CHUNK] if idx % 2 == 0 else cum_r[:, 0:1]
        return ig_c, cum_c, ig_r, cum_r, g

    def keys_values(c, hl):
        rows = chunk_rows(c)
        ks = k_ref[0, rows, ML_DQK * hl:ML_DQK * (hl + 1)] * (ML_DQK ** -0.5)
        v_aug = jnp.concatenate([v_ref[0, rows, ML_DV * hl:ML_DV * (hl + 1)], ones_col], axis=1)
        return ks, v_aug

    scalar_tile = lambda m: jnp.broadcast_to(m, (8, 128))

    def local_state(c, carry):
        for hl in range(2):
            ks, v_aug = keys_values(c, hl)
            ke = []
            for d in range(2):
                ig_c, cum_c, _, _, g = gates(c, 2 * hl + d)
                w = g - cum_c + ig_c
                m_loc = jnp.max(w, axis=0, keepdims=True)
                mloc_scr[c, 2 * hl + d] = scalar_tile(m_loc)
                ke.append(ks.astype(F32) * jnp.exp(w - m_loc))
            ke = jnp.concatenate(ke, axis=1).astype(BF16)
            cl_scr[hl, c] = lax.dot_general(ke, v_aug, (((0,), (0,)), ((), ())), preferred_element_type=F32)
        return carry

    lax.fori_loop(0, ML_NCH, local_state, 0)

    st_scr[...] = jnp.zeros(st_scr.shape, F32)

    def recur(i, ms):
        new_ms = []
        for hl in range(2):
            for d in range(2):
                idx = 2 * hl + d
                if d == 0:
                    c = i
                else:
                    c = jnp.where(i < ML_CTX_CH, ML_CTX_CH - 1 - i, ML_NCH + ML_CTX_CH - 1 - i)
                g = gates(c, idx)[4]
                m0, c_aug = ms[idx], st_scr[idx]
                c0_scr[idx, c] = c_aug.astype(BF16)
                m0_scr[c, idx] = scalar_tile(m0)
                m_loc = mloc_scr[c, idx, 0:1, 0:1]
                m_new = jnp.maximum(g + m0, m_loc)
                st_scr[idx] = (jnp.exp(g + m0 - m_new) * c_aug
                               + jnp.exp(m_loc - m_new) * cl_scr[hl, c, ML_DQK * d:ML_DQK * (d + 1), :])
                new_ms.append(m_new)
        return tuple(new_ms)

    lax.fori_loop(0, ML_NCH, recur, tuple(jnp.zeros((1, 1), F32) for _ in range(4)))

    def outputs(c, carry):
        rows = chunk_rows(c)
        for hl in range(2):
            ks, v_aug = keys_values(c, hl)
            q = q_ref[0, rows, ML_DQK * hl:ML_DQK * (hl + 1)]
            raw = _nt_dot(q, ks)
            qc = _dot(q, jnp.concatenate([c0_scr[2 * hl, c], c0_scr[2 * hl + 1, c]], axis=1))
            s, a_inter, m_row = [], [], []
            for d in range(2):
                idx = 2 * hl + d
                ig_c, cum_c, ig_r, cum_r, _ = gates(c, idx)
                log_d = jnp.where(low if d == 0 else upp, cum_c - cum_r + ig_r, -jnp.inf)
                log_inter = cum_c + m0_scr[c, idx, 0:1, 0:1]
                m_row.append(jnp.maximum(log_inter, jnp.max(log_d, axis=1, keepdims=True)))
                s.append(raw * jnp.exp(log_d - m_row[d]))
                a_inter.append(jnp.exp(log_inter - m_row[d]))
            sv = _dot(jnp.concatenate(s, axis=0).astype(BF16), v_aug)
            h = None
            for d in range(2):
                nd = a_inter[d] * qc[:, 2 * ML_DV * d:2 * ML_DV * (d + 1)] + sv[ML_CHUNK * d:ML_CHUNK * (d + 1)]
                den = jnp.maximum(jnp.abs(nd[:, ML_DV:ML_DV + 1]), jnp.exp(-m_row[d]))
                h = nd[:, :ML_DV] / den if h is None else h + nd[:, :ML_DV] / den
            o_ref[0, rows, ML_DV * hl:ML_DV * (hl + 1)] = h
        return carry

    lax.fori_loop(0, ML_NCH, outputs, 0)


def _mlstm_scan(h, proj, wg, wgt, b_col, b_row, nb):
    pairs = ML_HEADS // 2
    k_blk = pl.BlockSpec((1, NTOK, 2 * ML_DQK), lambda b, p: (b, 0, p))
    v_blk = pl.BlockSpec((1, NTOK, 2 * ML_DV), lambda b, p: (b, 0, ML_QK // (2 * ML_DV) + p))
    q_blk = pl.BlockSpec((1, NTOK, 2 * ML_DQK), lambda b, p: (b, 0, (ML_QK + D) // (2 * ML_DQK) + p))
    per_pair = lambda shape: pl.BlockSpec((1,) + shape, lambda b, p: (p, 0, 0))
    proj3 = proj.reshape(nb, NTOK, proj.shape[-1])
    return pl.pallas_call(
        _mlstm_kernel,
        grid=(nb, pairs),
        in_specs=[pl.BlockSpec((1, NTOK, D), lambda b, p: (b, 0, 0)), k_blk, v_blk, q_blk,
                  per_pair((D, 8)), per_pair((8, D)), per_pair((1, 8)), per_pair((8, 1))],
        out_specs=pl.BlockSpec((1, NTOK, 2 * ML_DV), lambda b, p: (b, 0, p)),
        out_shape=jax.ShapeDtypeStruct((nb, NTOK, D), F32),
        scratch_shapes=[pltpu.VMEM((NTOK, 8), F32), pltpu.VMEM((ML_NCH, 8, ML_CHUNK), F32),
                        pltpu.VMEM((4, ML_DQK, 2 * ML_DV), F32),
                        pltpu.VMEM((2, ML_NCH, 2 * ML_DQK, 2 * ML_DV), F32),
                        pltpu.VMEM((4, ML_NCH, ML_DQK, 2 * ML_DV), BF16),
                        pltpu.VMEM((ML_NCH, 4, 8, 128), F32), pltpu.VMEM((ML_NCH, 4, 8, 128), F32)],
        compiler_params=_cparams(("parallel", "parallel")),
    )(h.reshape(nb, NTOK, D), proj3, proj3, proj3, wg, wgt, b_col, b_row)


def _ml_finish_kernel(hh_ref, o_ref, g_ref, w_ref, y_ref):
    parts = [_rms(hh_ref[:, ML_DV * i:ML_DV * (i + 1)]) for i in range(ML_HEADS)]
    hn = jnp.concatenate(parts, axis=1) * g_ref[...]
    z = (hn * jax.nn.sigmoid(o_ref[...].astype(F32))).astype(BF16)
    y_ref[...] = _dot(z, w_ref[...])


def _ml_finish(hh, proj, norm_g, w_out):
    t = hh.shape[0]
    tok = pl.BlockSpec((LIN_TM, D), lambda i: (i, 0))
    return pl.pallas_call(
        _ml_finish_kernel,
        grid=(t // LIN_TM,),
        in_specs=[tok, pl.BlockSpec((LIN_TM, D), lambda i: (i, (ML_QK + D + ML_QK) // D)),
                  pl.BlockSpec((1, D), lambda i: (0, 0)), pl.BlockSpec((D, D), lambda i: (0, 0))],
        out_specs=tok,
        out_shape=jax.ShapeDtypeStruct((t, D), F32),
        compiler_params=_cparams(("parallel",)),
    )(hh, proj, norm_g.reshape(1, D), w_out)


def _mlstm_mixer(h, nb, w_in, b_gate, norm_g, w_out):
    n_state = ML_QK + D
    n_gate = 4 * ML_HEADS
    w_main = jnp.concatenate([w_in[:, :n_state], w_in[:, n_state + n_gate:]], axis=1).astype(BF16)
    pairs = ML_HEADS // 2
    wg = w_in[:, n_state:n_state + n_gate].reshape(D, 2, 2, pairs, 2).transpose(3, 0, 4, 1, 2)
    wg = wg.reshape(pairs, D, 8).astype(BF16)
    bg = b_gate.reshape(2, 2, pairs, 2).transpose(2, 3, 0, 1).reshape(pairs, 8)
    proj = _linear(h, w_main, BF16)
    hh = _mlstm_scan(h, proj, wg, wg.transpose(0, 2, 1), bg[:, None, :], bg[:, :, None], nb)
    return _ml_finish(hh.reshape(nb * NTOK, D), proj, norm_g, w_out.astype(BF16))


PAIRS = tuple((a, b) for a in range(4) for b in range(a + 1, 4))
N_CAT = N_GROUPS * len(PAIRS)


def _route_tables(ridx, t):
    n_tiles = t // MOE_TM + N_CAT
    n_rows = n_tiles * MOE_TM
    lo, hi = ridx[0], ridx[1]
    a, b = lo % 4, hi % 4
    cat = (lo // 4) * len(PAIRS) + (a * (7 - a)) // 2 + (b - a - 1)
    onehot = (cat[:, None] == jnp.arange(N_CAT)[None, :]).astype(jnp.int32)
    rank = jnp.take_along_axis(jnp.cumsum(onehot, axis=0) - onehot, cat[:, None], axis=1)[:, 0]
    counts = jnp.sum(onehot, axis=0)
    padded = ((counts + MOE_TM - 1) // MOE_TM) * MOE_TM
    ends = jnp.cumsum(padded)
    pos = (ends - padded)[cat] + rank
    tok_ids = jnp.arange(t, dtype=jnp.int32)
    src = jnp.zeros((n_rows,), jnp.int32).at[pos].set(tok_ids)
    n_used = ends[-1] // MOE_TM
    tile_start = jnp.minimum(jnp.arange(n_tiles), n_used - 1) * MOE_TM
    tile_cat = jnp.sum((ends[None, :] <= tile_start[:, None]).astype(jnp.int32), axis=1)
    n_valid = jnp.clip((ends - padded + counts)[tile_cat] - tile_start, 0, MOE_TM)
    pa = jnp.array([p[0] for p in PAIRS], jnp.int32)[tile_cat % len(PAIRS)]
    pb = jnp.array([p[1] for p in PAIRS], jnp.int32)[tile_cat % len(PAIRS)]
    base = (tile_cat // len(PAIRS)) * 4
    i32 = lambda a: a.astype(jnp.int32)
    return i32(base + pa), i32(base + pb), i32(n_used).reshape(1), i32(n_valid), src


def _moe_kernel(ea_ref, eb_ref, nu_ref, nv_ref, tok_ref,
                h_hbm, wga_ref, wua_ref, wda_ref, wgb_ref, wub_ref, wdb_ref, y_hbm,
                xbuf, ybuf, bin_scr, gsem, ssem):
    i = pl.program_id(0)
    n_used = nu_ref[0]

    def by_eights(row_fn):
        def body(c, carry):
            for u in range(8):
                row_fn(c * 8 + u, carry)
            return carry
        return body

    def issue_gather(tile, sl):
        def row(r, carry):
            tok = tok_ref[tile * MOE_TM + r]
            pltpu.make_async_copy(h_hbm.at[pl.ds(tok, 1)], xbuf.at[sl, pl.ds(r, 1)], gsem.at[sl]).start()
            return carry
        lax.fori_loop(0, MOE_TM // 8, by_eights(row), 0)

    def wait_gather(sl):
        pltpu.make_async_copy(h_hbm.at[pl.ds(0, MOE_TM)], xbuf.at[sl], gsem.at[sl]).wait()

    def issue_scatter(tile, sl):
        def to_hbm(r, carry):
            tok = tok_ref[tile * MOE_TM + r]
            pltpu.make_async_copy(ybuf.at[sl, pl.ds(r, 1)], y_hbm.at[pl.ds(tok, 1)], ssem.at[sl]).start()
            return carry

        def to_bin(r, carry):
            pltpu.make_async_copy(ybuf.at[sl, pl.ds(r, 1)], bin_scr.at[sl, pl.ds(r, 1)], ssem.at[sl]).start()
            return carry
        nv = nv_ref[tile]
        nv_down = lax.shift_right_logical(nv, 3)
        nv_up = lax.shift_right_logical(nv + 7, 3)
        lax.fori_loop(0, nv_down, by_eights(to_hbm), 0)
        lax.fori_loop(nv_down * 8, nv, to_hbm, 0)
        lax.fori_loop(nv, nv_up * 8, to_bin, 0)
        lax.fori_loop(nv_up, MOE_TM // 8, by_eights(to_bin), 0)

    def wait_scatter(sl):
        pltpu.make_async_copy(ybuf.at[sl], y_hbm.at[pl.ds(0, MOE_TM)], ssem.at[sl]).wait()

    @pl.when(i == 0)
    def _():
        issue_gather(0, 0)

    def tile(slot):
        wait_gather(slot)

        @pl.when(i >= 2)
        def _():
            wait_scatter(slot)

        nxt = jnp.minimum(i + 1, n_used - 1) * MOE_TM
        for r in range(MOE_TM):
            pltpu.make_async_copy(h_hbm.at[pl.ds(tok_ref[nxt + r], 1)], xbuf.at[1 - slot, pl.ds(r, 1)],
                                  gsem.at[1 - slot]).start()

        x = xbuf[slot, :, :D].astype(BF16)
        w = xbuf[slot, :, D:]

        def expert(wg_ref, wu_ref, wd_ref):
            a = _dot(x, wg_ref[0].astype(BF16))
            hid = (a * jax.nn.sigmoid(a) * _dot(x, wu_ref[0].astype(BF16))).astype(BF16)
            return _dot(hid, wd_ref[0].astype(BF16))

        ybuf[slot] = (w[:, 0:1] * expert(wga_ref, wua_ref, wda_ref)
                      + w[:, 1:2] * expert(wgb_ref, wub_ref, wdb_ref))
        issue_scatter(i, slot)

    for parity in range(2):
        pl.when(jnp.logical_and(i < n_used, i % 2 == parity))(functools.partial(tile, parity))

    @pl.when(i == pl.num_programs(0) - 1)
    def _():
        wait_scatter(0)
        wait_scatter(1)
        wait_gather(n_used % 2)


def _moe(h2, ridx, w_gate, w_up, w_down):
    t = h2.shape[0]
    ea, eb, n_used, n_valid, src = _route_tables(ridx, t)
    n_tiles = ea.shape[0]
    assert t // MOE_TM >= 2
    wspec = lambda which, shape: pl.BlockSpec(
        (1,) + shape, (lambda i, ea, eb, nu, nv, tok: (ea[i], 0, 0)) if which == 0
        else (lambda i, ea, eb, nu, nv, tok: (eb[i], 0, 0)))
    wg, wu, wd = w_gate, w_up, w_down
    return pl.pallas_call(
        _moe_kernel,
        grid_spec=pltpu.PrefetchScalarGridSpec(
            num_scalar_prefetch=5,
            grid=(n_tiles,),
            in_specs=[pl.BlockSpec(memory_space=pl.ANY),
                      wspec(0, (D, D_FF)), wspec(0, (D, D_FF)), wspec(0, (D_FF, D)),
                      wspec(1, (D, D_FF)), wspec(1, (D, D_FF)), wspec(1, (D_FF, D))],
            out_specs=pl.BlockSpec(memory_space=pl.ANY),
            scratch_shapes=[pltpu.VMEM((2, MOE_TM, MOE_ROW), F32), pltpu.VMEM((2, MOE_TM, D), F32),
                            pltpu.VMEM((2, MOE_TM, D), F32),
                            pltpu.SemaphoreType.DMA((2,)), pltpu.SemaphoreType.DMA((2,))]),
        out_shape=jax.ShapeDtypeStruct((t, D), F32),
        compiler_params=_cparams(("arbitrary",)),
    )(ea, eb, n_used, n_valid, src, h2, wg, wu, wd, wg, wu, wd)


def kernel(x, c, ctx, c_ctx, ada_w, ada_b, norm1_g, norm2_g, final_g, s5_lambda_re, s5_lambda_im, s5_log_step, s5_b_re, s5_b_im, s5_c_re, s5_c_im, s5_d, s5_w_glu, s5_b_glu, attn_w_kvq, attn_q_gain, attn_k_gain, attn_w_o, ml_w_in, ml_b_gate, ml_norm_g, ml_w_out, router_w, router_bias, moe_w_gate, moe_w_up, moe_w_down):
    nb = x.shape[0]
    depth = ada_w.shape[0]
    assert x.shape[1:] == (SEQ, D) and ctx.shape[1:] == (CTX, D) and nb + 1 <= 16
    xs = jnp.concatenate([ctx, x], axis=1).reshape(nb * NTOK, D)
    cond_in = jnp.zeros((16, D), F32).at[:nb].set(c).at[nb].set(c_ctx)
    mods = _ada(cond_in, ada_w, ada_b).reshape(depth, 16, 6, D)

    y2 = None
    for layer in range(depth):
        kind, j = layer % N_MIXERS, layer // N_MIXERS
        xs, h = _pre(xs, y2, mods[layer - 1] if layer else None, mods[layer], norm1_g[layer], nb,
                     F32 if kind == 0 else BF16, chunk_major=kind == 0)
        if kind == 0:
            y = _s5_mixer(h, nb, s5_lambda_re[j], s5_lambda_im[j], s5_log_step[j], s5_b_re[j], s5_b_im[j],
                          s5_c_re[j], s5_c_im[j], s5_d[j], s5_w_glu[j], s5_b_glu[j])
        elif kind == 1:
            y = _gqa_mixer(h, nb, attn_w_kvq[j], attn_q_gain[j], attn_k_gain[j], attn_w_o[j])
        else:
            y = _mlstm_mixer(h, nb, ml_w_in[j], ml_b_gate[j], ml_norm_g[j], ml_w_out[j])
        xs, h2, ridx = _post(xs, y, mods[layer], norm2_g[layer], router_w, router_bias, nb)
        y2 = _moe(h2, ridx, moe_w_gate[layer], moe_w_up[layer], moe_w_down[layer])
    out = _final(xs, y2, mods[depth - 1], final_g, nb)
    return out.reshape(nb, SEQ, D)
```

```python
import functools
import math

import jax
import jax.numpy as jnp
import numpy as np
from jax import lax
from jax.experimental import pallas as pl
from jax.experimental.pallas import tpu as pltpu

F32 = jnp.float32
BF16 = jnp.bfloat16

D = 1024
CTX = 256
SEQ = 2048
NTOK = CTX + SEQ
GRID_W = 64
EPS = 1e-6
N_MIXERS = 3

TOK = 256
TILES_PER_B = NTOK // TOK
LIN_TM = 512

S5_GROUP = 16
S5_GROUPS = D // S5_GROUP
S5_STATE = 64
S5_CHUNK = 16
S5_NCH = NTOK // S5_CHUNK
S5_CTX_CH = CTX // S5_CHUNK
S5_SLAB_GROUPS = 128 // S5_GROUP

HEAD_DIM = 64
N_Q_HEADS = 16
N_KV_HEADS = 4
Q_PER_KV = 4
KV_COLS = 2 * N_KV_HEADS * HEAD_DIM
ROPE_THETA = 10000.0
ATT_TQ = 128

ML_HEADS = 8
ML_DQK = 64
ML_DV = 128
ML_QK = 512
ML_CHUNK = 128
ML_NCH = NTOK // ML_CHUNK
ML_CTX_CH = CTX // ML_CHUNK

N_EXPERTS = 16
N_GROUPS = 4
D_FF = 512
MOE_TM = 256
MOE_ROW = D + 128

VMEM_LIMIT = 56 * 1024 * 1024


def _cparams(sem):
    return pltpu.CompilerParams(dimension_semantics=sem, vmem_limit_bytes=VMEM_LIMIT)


def _nt_dot(a, b, **kw):
    return lax.dot_general(a, b, (((1,), (1,)), ((), ())), preferred_element_type=F32, **kw)


def _dot(a, b):
    return jnp.dot(a, b, preferred_element_type=F32)


def _rms(x):
    return x * lax.rsqrt(jnp.mean(x * x, axis=-1, keepdims=True) + EPS)


def _ada_kernel(c_ref, w_ref, b_ref, o_ref):
    c = c_ref[...]
    cond = (c * jax.nn.sigmoid(c)).astype(BF16)
    o_ref[0] = _dot(cond, w_ref[0].astype(BF16)) + b_ref[0]


def _ada(cond_in, ada_w, ada_b):
    depth, _, n = ada_w.shape
    tn = 1536
    rows = cond_in.shape[0]
    return pl.pallas_call(
        _ada_kernel,
        grid=(depth, n // tn),
        in_specs=[pl.BlockSpec((rows, D), lambda l, j: (0, 0)),
                  pl.BlockSpec((1, D, tn), lambda l, j: (l, 0, j)),
                  pl.BlockSpec((1, 1, tn), lambda l, j: (l, 0, j))],
        out_specs=pl.BlockSpec((1, rows, tn), lambda l, j: (l, 0, j)),
        out_shape=jax.ShapeDtypeStruct((depth, rows, n), F32),
        compiler_params=_cparams(("parallel", "parallel")),
    )(cond_in, ada_w, ada_b.reshape(depth, 1, n))


def _mod_row(nb):
    return lambda i: (jnp.where(i % TILES_PER_B == 0, nb, i // TILES_PER_B), 0, 0)


def _pre_kernel(has_res, *refs):
    if has_res:
        x_ref, y_ref, mprev_ref, mcur_ref, g_ref, xo_ref, h_ref = refs
        x = x_ref[...] + mprev_ref[0, 5:6, :] * y_ref[...]
        xo_ref[...] = x
    else:
        x_ref, mcur_ref, g_ref, h_ref = refs
        x = x_ref[...]
    h = _rms(x) * g_ref[...]
    h = h * (1.0 + mcur_ref[0, 1:2, :]) + mcur_ref[0, 0:1, :]
    h_ref[...] = h.astype(h_ref.dtype).reshape(h_ref.shape)


def _chunk_major(nb):
    shape = (S5_NCH, nb, S5_CHUNK, D)
    spec = pl.BlockSpec((TOK // S5_CHUNK, 1, S5_CHUNK, D), lambda i: (i % TILES_PER_B, i // TILES_PER_B, 0, 0))
    return shape, spec


def _pre(x, y2, mods_prev, mods_cur, g, nb, h_dtype, chunk_major=False):
    t = x.shape[0]
    tok = pl.BlockSpec((TOK, D), lambda i: (i, 0))
    mod = pl.BlockSpec((1, 6, D), _mod_row(nb))
    vec = pl.BlockSpec((1, D), lambda i: (0, 0))
    h_shape, h_spec = _chunk_major(nb) if chunk_major else ((t, D), tok)
    if y2 is None:
        return x, pl.pallas_call(
            functools.partial(_pre_kernel, False),
            grid=(t // TOK,), in_specs=[tok, mod, vec], out_specs=h_spec,
            out_shape=jax.ShapeDtypeStruct(h_shape, h_dtype),
            compiler_params=_cparams(("parallel",)),
        )(x, mods_cur, g.reshape(1, D))
    return pl.pallas_call(
        functools.partial(_pre_kernel, True),
        grid=(t // TOK,), in_specs=[tok, tok, mod, mod, vec], out_specs=[tok, h_spec],
        out_shape=[jax.ShapeDtypeStruct((t, D), F32), jax.ShapeDtypeStruct(h_shape, h_dtype)],
        compiler_params=_cparams(("parallel",)),
    )(x, y2, mods_prev, mods_cur, g.reshape(1, D))


def _post_kernel(x_ref, y_ref, m_ref, g_ref, rwt_ref, rb_ref, xo_ref, h_ref, ridx_ref):
    x = x_ref[...] + m_ref[0, 2:3, :] * y_ref[...].reshape(x_ref.shape)
    xo_ref[...] = x
    h2 = _rms(x) * g_ref[...]
    h2 = h2 * (1.0 + m_ref[0, 4:5, :]) + m_ref[0, 3:4, :]
    h_ref[:, :D] = h2

    logits = _nt_dot(rwt_ref[...], h2, precision=lax.Precision.HIGHEST)
    scores = jax.nn.sigmoid(logits)
    sel = scores + rb_ref[...]
    row = lax.broadcasted_iota(jnp.int32, sel.shape, 0)
    per_group = N_EXPERTS // N_GROUPS
    best_val, best = None, None
    for g in range(N_GROUPS):
        r = [sel[per_group * g + j:per_group * g + j + 1, :] for j in range(per_group)]
        gs = None
        for i in range(per_group):
            for j in range(i + 1, per_group):
                gs = r[i] + r[j] if gs is None else jnp.maximum(gs, r[i] + r[j])
        if g == 0:
            best_val, best = gs, jnp.zeros(gs.shape, jnp.int32)
        else:
            upd = gs > best_val
            best = jnp.where(upd, g, best)
            best_val = jnp.where(upd, gs, best_val)
    masked = jnp.where((row // per_group) == best, sel, -jnp.inf)
    m1 = jnp.max(masked, axis=0, keepdims=True)
    i1 = jnp.min(jnp.where(masked == m1, row, N_EXPERTS), axis=0, keepdims=True)
    masked2 = jnp.where(row == i1, -jnp.inf, masked)
    m2 = jnp.max(masked2, axis=0, keepdims=True)
    i2 = jnp.min(jnp.where(masked2 == m2, row, N_EXPERTS), axis=0, keepdims=True)
    lo, hi = jnp.minimum(i1, i2), jnp.maximum(i1, i2)
    s_lo = jnp.sum(jnp.where(row == lo, scores, 0.0), axis=0, keepdims=True)
    s_hi = jnp.sum(jnp.where(row == hi, scores, 0.0), axis=0, keepdims=True)
    ridx_ref[...] = jnp.concatenate([lo, hi], axis=0)
    wts = jnp.concatenate([s_lo, s_hi, jnp.zeros((126, TOK), F32)], axis=0) / (s_lo + s_hi)
    h_ref[:, D:] = wts.T


def _post(x, y, mods, g, router_w, router_bias, nb):
    t = x.shape[0]
    tok = pl.BlockSpec((TOK, D), lambda i: (i, 0))
    return pl.pallas_call(
        _post_kernel,
        grid=(t // TOK,),
        in_specs=[tok, _chunk_major(nb)[1] if y.ndim == 4 else tok, pl.BlockSpec((1, 6, D), _mod_row(nb)),
                  pl.BlockSpec((1, D), lambda i: (0, 0)),
                  pl.BlockSpec((N_EXPERTS, D), lambda i: (0, 0)),
                  pl.BlockSpec((N_EXPERTS, 1), lambda i: (0, 0))],
        out_specs=[tok, pl.BlockSpec((TOK, MOE_ROW), lambda i: (i, 0)), pl.BlockSpec((2, TOK), lambda i: (0, i))],
        out_shape=[jax.ShapeDtypeStruct((t, D), F32), jax.ShapeDtypeStruct((t, MOE_ROW), F32),
                   jax.ShapeDtypeStruct((2, t), jnp.int32)],
        compiler_params=_cparams(("parallel",)),
    )(x, y, mods, g.reshape(1, D), router_w.T, router_bias.reshape(N_EXPERTS, 1))


def _final_kernel(x_ref, y_ref, m_ref, g_ref, o_ref):
    x = x_ref[...] + m_ref[0, 5:6, :] * y_ref[...]
    o_ref[...] = _rms(x) * g_ref[...]


def _final(x, y2, mods, g, nb):
    lat_tiles = SEQ // TOK
    src = lambda j: ((j // lat_tiles) * TILES_PER_B + CTX // TOK + j % lat_tiles, 0)
    tok_in = pl.BlockSpec((TOK, D), src)
    return pl.pallas_call(
        _final_kernel,
        grid=(nb * lat_tiles,),
        in_specs=[tok_in, tok_in, pl.BlockSpec((1, 6, D), lambda j: (j // lat_tiles, 0, 0)),
                  pl.BlockSpec((1, D), lambda j: (0, 0))],
        out_specs=pl.BlockSpec((TOK, D), lambda j: (j, 0)),
        out_shape=jax.ShapeDtypeStruct((nb * SEQ, D), F32),
        compiler_params=_cparams(("parallel",)),
    )(x, y2, mods, g.reshape(1, D))


def _linear_kernel(x_ref, w_ref, o_ref):
    o_ref[...] = _dot(x_ref[...].astype(BF16), w_ref[...]).astype(o_ref.dtype)


def _linear(x, w, out_dtype):
    t, k = x.shape
    n = w.shape[1]
    return pl.pallas_call(
        _linear_kernel,
        grid=(t // LIN_TM,),
        in_specs=[pl.BlockSpec((LIN_TM, k), lambda i: (i, 0)), pl.BlockSpec((k, n), lambda i: (0, 0))],
        out_specs=pl.BlockSpec((LIN_TM, n), lambda i: (i, 0)),
        out_shape=jax.ShapeDtypeStruct((t, n), out_dtype),
        compiler_params=_cparams(("parallel",)),
    )(x, w)


def _s5_operators(lam_re, lam_im, log_step, b_re, b_im, c_re, c_im):
    n = S5_CHUNK
    dt = jnp.exp(log_step)[None, :, :, None]
    k = jnp.arange(n + 1, dtype=F32)[:, None, None, None]
    mag = jnp.exp(k * lam_re[None] * dt)
    ang = k * lam_im[None] * dt
    pw_re, pw_im = mag * jnp.cos(ang), mag * jnp.sin(ang)
    a_re, a_im = pw_re[1], pw_im[1]
    den = lam_re * lam_re + lam_im * lam_im
    n_re, n_im = a_re - 1.0, a_im
    f_re = (n_re * lam_re + n_im * lam_im) / den
    f_im = (n_im * lam_re - n_re * lam_im) / den
    bt_re, bt_im = b_re.swapaxes(-1, -2), b_im.swapaxes(-1, -2)
    bb_re = f_re[:, :, None, :] * bt_re - f_im[:, :, None, :] * bt_im
    bb_im = f_re[:, :, None, :] * bt_im + f_im[:, :, None, :] * bt_re

    cp_re = c_re[None] * pw_re[:, :, :, None, :] - c_im[None] * pw_im[:, :, :, None, :]
    cp_im = c_re[None] * pw_im[:, :, :, None, :] + c_im[None] * pw_re[:, :, :, None, :]
    kk = (jnp.einsum('dgcp,kdgop->dgcko', bb_re, cp_re[:n]) - jnp.einsum('dgcp,kdgop->dgcko', bb_im, cp_im[:n]))
    wide = n * S5_GROUP
    lags = jnp.concatenate([kk[0].reshape(S5_GROUPS, S5_GROUP, wide),
                            kk[1, :, :, ::-1].reshape(S5_GROUPS, S5_GROUP, wide)], axis=1)

    def state_in(pw_sel, d):
        pr = pw_re[pw_sel, d].swapaxes(0, 1)[:, :, None, :]
        pi = pw_im[pw_sel, d].swapaxes(0, 1)[:, :, None, :]
        e_re = pr * bb_re[d][:, None] - pi * bb_im[d][:, None]
        e_im = pr * bb_im[d][:, None] + pi * bb_re[d][:, None]
        to_w = lambda e: jnp.pad(e.reshape(S5_GROUPS, wide, S5_STATE), ((0, 0), (0, 0), (0, 128 - S5_STATE)))
        return to_w(e_re), to_w(e_im)

    def state_out(pw_sel, d):
        ct_re = jnp.tile(c_re[d].swapaxes(-1, -2), (1, 1, n))
        ct_im = jnp.tile(c_im[d].swapaxes(-1, -2), (1, 1, n))
        pr = jnp.repeat(pw_re[pw_sel, d].transpose(1, 2, 0), S5_GROUP, axis=2)
        pi = jnp.repeat(pw_im[pw_sel, d].transpose(1, 2, 0), S5_GROUP, axis=2)
        to_w = lambda o: jnp.pad(o, ((0, 0), (0, 128 - S5_STATE), (0, 0)))
        return to_w(ct_re * pr - ct_im * pi), to_w(-(ct_re * pi + ct_im * pr))

    ar = jnp.arange(n)
    w1 = jnp.concatenate([*state_in(n - 1 - ar, 0), *state_in(ar, 1)], axis=2)
    w2 = jnp.concatenate([*state_out(ar + 1, 0), *state_out(n - ar, 1)], axis=1)
    pad = lambda a: jnp.pad(a, ((0, 0), (0, 128 - S5_STATE)))
    a16 = jnp.stack([pad(pw_re[n, 0]), pad(pw_im[n, 0]), pad(pw_re[n, 1]), pad(pw_im[n, 1])], axis=1)
    return lags, w1.astype(BF16), w2.astype(BF16), a16


def _s5_select():
    sel = np.zeros((S5_SLAB_GROUPS, S5_CHUNK * 128, S5_CHUNK * S5_GROUP), np.float32)
    s, c = np.meshgrid(np.arange(S5_CHUNK), np.arange(S5_GROUP), indexing="ij")
    for j in range(S5_SLAB_GROUPS):
        sel[j, s * 128 + S5_GROUP * j + c, s * S5_GROUP + c] = 1.0
    return jnp.asarray(sel, BF16)


def _s5_kernel(nb, h_ref, sel_ref, lag_ref, w1_ref, w2_ref, a_ref, y_ref, x2_scr, s_scr, x_scr, toep_scr):
    rows = S5_NCH * nb

    @pl.when(pl.program_id(1) == 0)
    def _():
        for s in range(S5_CHUNK):
            x2_scr[:, 128 * s:128 * (s + 1)] = h_ref[pl.ds(s, rows, stride=S5_CHUNK), :].astype(BF16)
        y_ref[...] = jnp.zeros(y_ref.shape, F32)

    wide = S5_CHUNK * S5_GROUP
    zeros = jnp.zeros((S5_GROUP, wide), F32)
    kf = jnp.concatenate([lag_ref[0, :S5_GROUP, :], zeros], axis=1)
    kb = jnp.concatenate([lag_ref[0, S5_GROUP:, :], zeros], axis=1)
    for s in range(S5_CHUNK):
        right, left = S5_GROUP * s, S5_GROUP * (S5_CHUNK - 1 - s)
        blk = ((pltpu.roll(kf, right, 1) if right else kf)[:, :wide]
               + (pltpu.roll(kb, 2 * wide - left, 1) if left else kb)[:, :wide])
        toep_scr[S5_GROUP * s:S5_GROUP * (s + 1), :] = blk.astype(BF16)

    u = _dot(x2_scr[...], sel_ref[0]).astype(BF16)
    s_scr[:, 0:wide] = _dot(u, toep_scr[...])
    s_scr[:, wide:] = _dot(u, w1_ref[0])

    def step(i, carry):
        cf = i
        cb = jnp.where(i < S5_CTX_CH, S5_CTX_CH - 1 - i, S5_NCH + S5_CTX_CH - 1 - i)
        rows_f = pl.ds(pl.multiple_of(cf * nb, nb), nb)
        rows_b = pl.ds(pl.multiple_of(cb * nb, nb), nb)
        fr, fi, br, bi = carry
        x_scr[rows_f, 0:128] = fr
        x_scr[rows_f, 128:256] = fi
        x_scr[rows_b, 256:384] = br
        x_scr[rows_b, 384:512] = bi
        ar, ai = a_ref[0, 0:1, :], a_ref[0, 1:2, :]
        nfr = ar * fr - ai * fi + s_scr[rows_f, 256:384]
        nfi = ar * fi + ai * fr + s_scr[rows_f, 384:512]
        ar, ai = a_ref[0, 2:3, :], a_ref[0, 3:4, :]
        nbr = ar * br - ai * bi + s_scr[rows_b, 512:640]
        nbi = ar * bi + ai * br + s_scr[rows_b, 640:768]
        return nfr, nfi, nbr, nbi

    zero = jnp.zeros((nb, 128), F32)
    lax.fori_loop(0, S5_NCH, step, (zero, zero, zero, zero))
    yg = s_scr[:, 0:256] + _dot(x_scr[...].astype(BF16), w2_ref[0])
    y_ref[0] += _nt_dot(yg.astype(BF16), sel_ref[0])


def _s5_scan(h_cm, lags, w1, w2, a16, nb):
    rows = S5_NCH * nb
    kc = S5_CHUNK * S5_GROUP
    n_slabs = D // 128
    per_group = lambda shape: pl.BlockSpec((1,) + shape, lambda q, j: (q * S5_SLAB_GROUPS + j, 0, 0))
    return pl.pallas_call(
        functools.partial(_s5_kernel, nb),
        grid=(n_slabs, S5_SLAB_GROUPS),
        in_specs=[pl.BlockSpec((nb * NTOK, 128), lambda q, j: (0, q)),
                  pl.BlockSpec((1, S5_CHUNK * 128, kc), lambda q, j: (j, 0, 0)),
                  per_group((2 * S5_GROUP, kc)), per_group((kc, 512)), per_group((512, kc)), per_group((4, 128))],
        out_specs=pl.BlockSpec((1, rows, S5_CHUNK * 128), lambda q, j: (q, 0, 0)),
        out_shape=jax.ShapeDtypeStruct((n_slabs, rows, S5_CHUNK * 128), F32),
        scratch_shapes=[pltpu.VMEM((rows, S5_CHUNK * 128), BF16), pltpu.VMEM((rows, 768), F32),
                        pltpu.VMEM((rows, 512), F32), pltpu.VMEM((kc, kc), BF16)],
        compiler_params=_cparams(("parallel", "arbitrary")),
    )(h_cm.reshape(nb * NTOK, D), _s5_select(), lags, w1, w2, a16)


def _gelu(x):
    return 0.5 * x * (1.0 + jnp.tanh(math.sqrt(2.0 / math.pi) * (x + 0.044715 * (x * x * x))))


def _glu_kernel(y_ref, h_ref, d_ref, w_ref, b_ref, o_ref, ys_scr):
    rows = TOK // S5_CHUNK
    for q in range(D // 128):
        for s in range(S5_CHUNK):
            ys_scr[q, pl.ds(s, rows, stride=S5_CHUNK), :] = y_ref[q, :, 128 * s:128 * (s + 1)]
    y = jnp.concatenate([ys_scr[q] for q in range(D // 128)], axis=1)
    z = _gelu(y + d_ref[...] * h_ref[...]).astype(BF16)
    r = _dot(z, w_ref[...]) + b_ref[...]
    o_ref[...] = r[:, :D] * jax.nn.sigmoid(r[:, D:])


def _glu(y, h, d_skip, w, b):
    t = h.shape[0]
    tok = pl.BlockSpec((TOK, D), lambda i: (i, 0))
    return pl.pallas_call(
        _glu_kernel,
        grid=(t // TOK,),
        in_specs=[pl.BlockSpec((D // 128, TOK // S5_CHUNK, S5_CHUNK * 128), lambda i: (0, i, 0)), tok,
                  pl.BlockSpec((1, D), lambda i: (0, 0)),
                  pl.BlockSpec((D, 2 * D), lambda i: (0, 0)), pl.BlockSpec((1, 2 * D), lambda i: (0, 0))],
        out_specs=tok,
        out_shape=jax.ShapeDtypeStruct((t, D), F32),
        scratch_shapes=[pltpu.VMEM((D // 128, TOK, 128), F32)],
        compiler_params=_cparams(("parallel",)),
    )(y, h, d_skip.reshape(1, D), w, b.reshape(1, 2 * D))


def _s5_mixer(h_cm, nb, lam_re, lam_im, log_step, b_re, b_im, c_re, c_im, d_skip, w_glu, b_glu):
    lags, w1, w2, a16 = _s5_operators(lam_re, lam_im, log_step, b_re, b_im, c_re, c_im)
    y = _s5_scan(h_cm, lags, w1, w2, a16, nb)
    y = _glu(y, h_cm.reshape(nb * NTOK, D), d_skip, w_glu.astype(BF16), b_glu)
    return y.reshape(h_cm.shape)


def _rope_tables():
    rows = SEQ // GRID_W
    row = jnp.repeat(jnp.arange(rows), GRID_W).astype(F32)
    col = jnp.tile(jnp.arange(GRID_W), rows).astype(F32)
    half = HEAD_DIM // 4
    inv_freq = ROPE_THETA ** (-jnp.arange(half, dtype=F32) / half)
    ang_r, ang_c = row[:, None] * inv_freq, col[:, None] * inv_freq
    cos = jnp.concatenate([jnp.cos(ang_r)] * 2 + [jnp.cos(ang_c)] * 2, axis=1)
    sin = jnp.concatenate([-jnp.sin(ang_r), jnp.sin(ang_r), -jnp.sin(ang_c), jnp.sin(ang_c)], axis=1)
    cos = jnp.concatenate([jnp.ones((CTX, HEAD_DIM), F32), cos], axis=0)
    sin = jnp.concatenate([jnp.zeros((CTX, HEAD_DIM), F32), sin], axis=0)
    return jnp.tile(cos, (1, 2)), jnp.tile(sin, (1, 2))


def _qk_prep_kernel(p_ref, cos_ref, sin_ref, qg_ref, kg_ref, q_ref, k_ref, v_ref):
    lane = lax.broadcasted_iota(jnp.int32, (TOK, 128), 1)
    low_head = lane < HEAD_DIM
    first_half = (lane % 32) < 16
    cos, sin = cos_ref[...], sin_ref[...]

    def norm_rope(x, gain):
        sq = x * x
        s_lo = jnp.sum(jnp.where(low_head, sq, 0.0), axis=1, keepdims=True)
        s_hi = jnp.sum(sq, axis=1, keepdims=True) - s_lo
        inv = lax.rsqrt(jnp.where(low_head, s_lo, s_hi) * (1.0 / HEAD_DIM) + EPS)
        xn = x * inv * gain
        partner = jnp.where(first_half, pltpu.roll(xn, 128 - 16, 1), pltpu.roll(xn, 16, 1))
        return xn * cos + partner * sin

    ones_col = (lax.broadcasted_iota(jnp.int32, (TOK, HEAD_DIM), 1) == 0).astype(BF16)
    for j in range(KV_COLS // 2 // 128):
        kt = norm_rope(p_ref[:, 128 * j:128 * (j + 1)], kg_ref[...]).astype(BF16)
        vt = p_ref[:, KV_COLS // 2 + 128 * j:KV_COLS // 2 + 128 * (j + 1)].astype(BF16)
        for hh in range(2):
            k_ref[0, 2 * j + hh] = kt[:, HEAD_DIM * hh:HEAD_DIM * (hh + 1)]
            v_ref[0, 2 * j + hh] = jnp.concatenate([vt[:, HEAD_DIM * hh:HEAD_DIM * (hh + 1)], ones_col], axis=1)
    for j in range(D // 128):
        qt = norm_rope(p_ref[:, KV_COLS + 128 * j:KV_COLS + 128 * (j + 1)], qg_ref[...])
        q_ref[:, 128 * j:128 * (j + 1)] = (qt * (HEAD_DIM ** -0.5 * math.log2(math.e))).astype(BF16)


def _qk_prep(proj, q_gain, k_gain, nb):
    t, ncol = proj.shape
    cos, sin = _rope_tables()
    tab = pl.BlockSpec((TOK, 128), lambda i: (i % TILES_PER_B, 0))
    gain = pl.BlockSpec((1, 128), lambda i: (0, 0))
    head_major = lambda width: pl.BlockSpec((1, N_KV_HEADS, TOK, width),
                                            lambda i: (i // TILES_PER_B, 0, i % TILES_PER_B, 0))
    hm_shape = lambda width: jax.ShapeDtypeStruct((nb, N_KV_HEADS, NTOK, width), BF16)
    return pl.pallas_call(
        _qk_prep_kernel,
        grid=(t // TOK,),
        in_specs=[pl.BlockSpec((TOK, ncol), lambda i: (i, 0)), tab, tab, gain, gain],
        out_specs=[pl.BlockSpec((TOK, D), lambda i: (i, 0)), head_major(HEAD_DIM), head_major(2 * HEAD_DIM)],
        out_shape=[jax.ShapeDtypeStruct((t, D), BF16), hm_shape(HEAD_DIM), hm_shape(2 * HEAD_DIM)],
        compiler_params=_cparams(("parallel",)),
    )(proj, cos, sin, jnp.tile(q_gain, 2).reshape(1, 128), jnp.tile(k_gain, 2).reshape(1, 128))


def _attn_kernel(q_ref, k_ref, v_ref, o_ref):
    def attend(n_keys):
        outs = []
        for kh in range(N_KV_HEADS):
            qs = jnp.concatenate(
                [q_ref[0, :, HEAD_DIM * (Q_PER_KV * kh + g):HEAD_DIM * (Q_PER_KV * kh + g + 1)]
                 for g in range(Q_PER_KV)], axis=0)
            s = _nt_dot(qs, k_ref[0, kh, :n_keys, :])
            p = jnp.exp2(s - jnp.max(s, axis=1, keepdims=True))
            ov = _dot(p.astype(BF16), v_ref[0, kh, :n_keys, :])
            o = ov[:, :HEAD_DIM] / ov[:, HEAD_DIM:HEAD_DIM + 1]
            outs += [o[ATT_TQ * g:ATT_TQ * (g + 1)] for g in range(Q_PER_KV)]
        o_ref[0] = jnp.concatenate(outs, axis=1).astype(BF16)

    is_ctx = pl.program_id(1) < CTX // ATT_TQ

    @pl.when(is_ctx)
    def _():
        attend(CTX)

    @pl.when(jnp.logical_not(is_ctx))
    def _():
        attend(NTOK)


def _attention(q, k, v, nb):
    kv = lambda width: pl.BlockSpec((1, N_KV_HEADS, NTOK, width), lambda b, i: (b, 0, 0, 0))
    qo = pl.BlockSpec((1, ATT_TQ, D), lambda b, i: (b, i, 0))
    return pl.pallas_call(
        _attn_kernel,
        grid=(nb, NTOK // ATT_TQ),
        in_specs=[qo, kv(HEAD_DIM), kv(2 * HEAD_DIM)], out_specs=qo,
        out_shape=jax.ShapeDtypeStruct((nb, NTOK, D), BF16),
        compiler_params=_cparams(("parallel", "parallel")),
    )(q.reshape(nb, NTOK, D), k, v)


def _gqa_mixer(h, nb, w_kvq, q_gain, k_gain, w_o):
    proj = _linear(h, w_kvq.astype(BF16), F32)
    q, k, v = _qk_prep(proj, q_gain, k_gain, nb)
    o = _attention(q, k, v, nb)
    return _linear(o.reshape(nb * NTOK, D), w_o.astype(BF16), F32)


def _split3(x):
    hi = x.astype(BF16).astype(F32)
    r = x - hi
    mid = r.astype(BF16).astype(F32)
    lo = (r - mid).astype(BF16).astype(F32)
    return hi, mid, lo


def _log_sigmoid(x):
    return jnp.minimum(x, 0.0) - jnp.log1p(jnp.exp(-jnp.abs(x)))


def _mlstm_kernel(h_ref, k_ref, v_ref, q_ref, wg_ref, wgt_ref, bc_ref, br_ref, o_ref,
                  col_scr, row_scr, st_scr, cl_scr, c0_scr, mloc_scr, m0_scr):
    hb = h_ref[0]
    g_col = _dot(hb, wg_ref[0]) + bc_ref[0]
    g_row = _nt_dot(wgt_ref[0], hb) + br_ref[0]
    ii = lax.broadcasted_iota(jnp.int32, (ML_CHUNK, ML_CHUNK), 0)
    jj = lax.broadcasted_iota(jnp.int32, (ML_CHUNK, ML_CHUNK), 1)
    low = ii >= jj
    upp = ii <= jj
    tri_l, tri_u = low.astype(BF16), upp.astype(BF16)
    tri_lu_rows = jnp.concatenate([tri_l, tri_u], axis=0)
    tri_ul_cols = jnp.concatenate([tri_u, tri_l], axis=1)
    lane_c = lax.broadcasted_iota(jnp.int32, (ML_CHUNK, 8), 1)
    sub_r = lax.broadcasted_iota(jnp.int32, (8, ML_CHUNK), 0)
    for c in range(ML_NCH):
        sl = slice(ML_CHUNK * c, ML_CHUNK * (c + 1))
        gc = g_col[sl]
        parts = _split3(jnp.where(lane_c % 2 == 1, _log_sigmoid(gc), 0.0))
        cs = _dot(tri_lu_rows, jnp.concatenate(parts, axis=1).astype(BF16))
        cs = cs[:, 0:8] + cs[:, 8:16] + cs[:, 16:24]
        cum = jnp.where((lane_c // 2) % 2 == 1, cs[ML_CHUNK:], cs[:ML_CHUNK])
        col_scr[sl, :] = jnp.where(lane_c % 2 == 1, cum, gc)
        gr = g_row[:, sl]
        parts = _split3(jnp.where(sub_r % 2 == 1, _log_sigmoid(gr), 0.0))
        rs = _dot(jnp.concatenate(parts, axis=0).astype(BF16), tri_ul_cols)
        rs = rs[0:8] + rs[8:16] + rs[16:24]
        cum = jnp.where((sub_r // 2) % 2 == 1, rs[:, ML_CHUNK:], rs[:, :ML_CHUNK])
        row_scr[c] = jnp.where(sub_r % 2 == 1, cum, gr)

    ones_col = (lax.broadcasted_iota(jnp.int32, (ML_CHUNK, ML_DV), 1) == 0).astype(BF16)

    def chunk_rows(c):
        return pl.ds(pl.multiple_of(c * ML_CHUNK, ML_CHUNK), ML_CHUNK)

    def gates(c, idx):
        rows, rr = chunk_rows(c), row_scr[c]
        ig_c, cum_c = col_scr[rows, 2 * idx:2 * idx + 1], col_scr[rows, 2 * idx + 1:2 * idx + 2]
        ig_r, cum_r = rr[2 * idx:2 * idx + 1, :], rr[2 * idx + 1:2 * idx + 2, :]
        g = cum_r[:, ML_CHUNK - 1:ML_CHUNK] if idx % 2 == 0 else cum_r[:, 0:1]
        return ig_c, cum_c, ig_r, cum_r, g

    def keys_values(c, hl):
        rows = chunk_rows(c)
        ks = k_ref[0, rows, ML_DQK * hl:ML_DQK * (hl + 1)] * (ML_DQK ** -0.5)
        v_aug = jnp.concatenate([v_ref[0, rows, ML_DV * hl:ML_DV * (hl + 1)], ones_col], axis=1)
        return ks, v_aug

    scalar_tile = lambda m: jnp.broadcast_to(m, (8, 128))

    def local_state(c, carry):
        for hl in range(2):
            ks, v_aug = keys_values(c, hl)
            ke = []
            for d in range(2):
                ig_c, cum_c, _, _, g = gates(c, 2 * hl + d)
                w = g - cum_c + ig_c
                m_loc = jnp.max(w, axis=0, keepdims=True)
                mloc_scr[c, 2 * hl + d] = scalar_tile(m_loc)
                ke.append(ks.astype(F32) * jnp.exp(w - m_loc))
            ke = jnp.concatenate(ke, axis=1).astype(BF16)
            cl_scr[hl, c] = lax.dot_general(ke, v_aug, (((0,), (0,)), ((), ())), preferred_element_type=F32)
        return carry

    lax.fori_loop(0, ML_NCH, local_state, 0)

    st_scr[...] = jnp.zeros(st_scr.shape, F32)

    def recur(i, ms):
        new_ms = []
        for hl in range(2):
            for d in range(2):
                idx = 2 * hl + d
                if d == 0:
                    c = i
                else:
                    c = jnp.where(i < ML_CTX_CH, ML_CTX_CH - 1 - i, ML_NCH + ML_CTX_CH - 1 - i)
                g = gates(c, idx)[4]
                m0, c_aug = ms[idx], st_scr[idx]
                c0_scr[idx, c] = c_aug.astype(BF16)
                m0_scr[c, idx] = scalar_tile(m0)
                m_loc = mloc_scr[c, idx, 0:1, 0:1]
                m_new = jnp.maximum(g + m0, m_loc)
                st_scr[idx] = (jnp.exp(g + m0 - m_new) * c_aug
                               + jnp.exp(m_loc - m_new) * cl_scr[hl, c, ML_DQK * d:ML_DQK * (d + 1), :])
                new_ms.append(m_new)
        return tuple(new_ms)

    lax.fori_loop(0, ML_NCH, recur, tuple(jnp.zeros((1, 1), F32) for _ in range(4)))

    def outputs(c, carry):
        rows = chunk_rows(c)
        for hl in range(2):
            ks, v_aug = keys_values(c, hl)
            q = q_ref[0, rows, ML_DQK * hl:ML_DQK * (hl + 1)]
            raw = _nt_dot(q, ks)
            qc = _dot(q, jnp.concatenate([c0_scr[2 * hl, c], c0_scr[2 * hl + 1, c]], axis=1))
            s, a_inter, m_row = [], [], []
            for d in range(2):
                idx = 2 * hl + d
                ig_c, cum_c, ig_r, cum_r, _ = gates(c, idx)
                log_d = jnp.where(low if d == 0 else upp, cum_c - cum_r + ig_r, -jnp.inf)
                log_inter = cum_c + m0_scr[c, idx, 0:1, 0:1]
                m_row.append(jnp.maximum(log_inter, jnp.max(log_d, axis=1, keepdims=True)))
                s.append(raw * jnp.exp(log_d - m_row[d]))
                a_inter.append(jnp.exp(log_inter - m_row[d]))
            sv = _dot(jnp.concatenate(s, axis=0).astype(BF16), v_aug)
            h = None
            for d in range(2):
                nd = a_inter[d] * qc[:, 2 * ML_DV * d:2 * ML_DV * (d + 1)] + sv[ML_CHUNK * d:ML_CHUNK * (d + 1)]
                den = jnp.maximum(jnp.abs(nd[:, ML_DV:ML_DV + 1]), jnp.exp(-m_row[d]))
                h = nd[:, :ML_DV] / den if h is None else h + nd[:, :ML_DV] / den
            o_ref[0, rows, ML_DV * hl:ML_DV * (hl + 1)] = h
        return carry

    lax.fori_loop(0, ML_NCH, outputs, 0)


def _mlstm_scan(h, proj, wg, wgt, b_col, b_row, nb):
    pairs = ML_HEADS // 2
    k_blk = pl.BlockSpec((1, NTOK, 2 * ML_DQK), lambda b, p: (b, 0, p))
    v_blk = pl.BlockSpec((1, NTOK, 2 * ML_DV), lambda b, p: (b, 0, ML_QK // (2 * ML_DV) + p))
    q_blk = pl.BlockSpec((1, NTOK, 2 * ML_DQK), lambda b, p: (b, 0, (ML_QK + D) // (2 * ML_DQK) + p))
    per_pair = lambda shape: pl.BlockSpec((1,) + shape, lambda b, p: (p, 0, 0))
    proj3 = proj.reshape(nb, NTOK, proj.shape[-1])
    return pl.pallas_call(
        _mlstm_kernel,
        grid=(nb, pairs),
        in_specs=[pl.BlockSpec((1, NTOK, D), lambda b, p: (b, 0, 0)), k_blk, v_blk, q_blk,
                  per_pair((D, 8)), per_pair((8, D)), per_pair((1, 8)), per_pair((8, 1))],
        out_specs=pl.BlockSpec((1, NTOK, 2 * ML_DV), lambda b, p: (b, 0, p)),
        out_shape=jax.ShapeDtypeStruct((nb, NTOK, D), F32),
        scratch_shapes=[pltpu.VMEM((NTOK, 8), F32), pltpu.VMEM((ML_NCH, 8, ML_CHUNK), F32),
                        pltpu.VMEM((4, ML_DQK, 2 * ML_DV), F32),
                        pltpu.VMEM((2, ML_NCH, 2 * ML_DQK, 2 * ML_DV), F32),
                        pltpu.VMEM((4, ML_NCH, ML_DQK, 2 * ML_DV), BF16),
                        pltpu.VMEM((ML_NCH, 4, 8, 128), F32), pltpu.VMEM((ML_NCH, 4, 8, 128), F32)],
        compiler_params=_cparams(("parallel", "parallel")),
    )(h.reshape(nb, NTOK, D), proj3, proj3, proj3, wg, wgt, b_col, b_row)


def _ml_finish_kernel(hh_ref, o_ref, g_ref, w_ref, y_ref):
    parts = [_rms(hh_ref[:, ML_DV * i:ML_DV * (i + 1)]) for i in range(ML_HEADS)]
    hn = jnp.concatenate(parts, axis=1) * g_ref[...]
    z = (hn * jax.nn.sigmoid(o_ref[...].astype(F32))).astype(BF16)
    y_ref[...] = _dot(z, w_ref[...])


def _ml_finish(hh, proj, norm_g, w_out):
    t = hh.shape[0]
    tok = pl.BlockSpec((LIN_TM, D), lambda i: (i, 0))
    return pl.pallas_call(
        _ml_finish_kernel,
        grid=(t // LIN_TM,),
        in_specs=[tok, pl.BlockSpec((LIN_TM, D), lambda i: (i, (ML_QK + D + ML_QK) // D)),
                  pl.BlockSpec((1, D), lambda i: (0, 0)), pl.BlockSpec((D, D), lambda i: (0, 0))],
        out_specs=tok,
        out_shape=jax.ShapeDtypeStruct((t, D), F32),
        compiler_params=_cparams(("parallel",)),
    )(hh, proj, norm_g.reshape(1, D), w_out)


def _mlstm_mixer(h, nb, w_in, b_gate, norm_g, w_out):
    n_state = ML_QK + D
    n_gate = 4 * ML_HEADS
    w_main = jnp.concatenate([w_in[:, :n_state], w_in[:, n_state + n_gate:]], axis=1).astype(BF16)
    pairs = ML_HEADS // 2
    wg = w_in[:, n_state:n_state + n_gate].reshape(D, 2, 2, pairs, 2).transpose(3, 0, 4, 1, 2)
    wg = wg.reshape(pairs, D, 8).astype(BF16)
    bg = b_gate.reshape(2, 2, pairs, 2).transpose(2, 3, 0, 1).reshape(pairs, 8)
    proj = _linear(h, w_main, BF16)
    hh = _mlstm_scan(h, proj, wg, wg.transpose(0, 2, 1), bg[:, None, :], bg[:, :, None], nb)
    return _ml_finish(hh.reshape(nb * NTOK, D), proj, norm_g, w_out.astype(BF16))


PAIRS = tuple((a, b) for a in range(4) for b in range(a + 1, 4))
N_CAT = N_GROUPS * len(PAIRS)


def _route_tables(ridx, t):
    n_tiles = t // MOE_TM + N_CAT
    n_rows = n_tiles * MOE_TM
    lo, hi = ridx[0], ridx[1]
    a, b = lo % 4, hi % 4
    cat = (lo // 4) * len(PAIRS) + (a * (7 - a)) // 2 + (b - a - 1)
    onehot = (cat[:, None] == jnp.arange(N_CAT)[None, :]).astype(jnp.int32)
    rank = jnp.take_along_axis(jnp.cumsum(onehot, axis=0) - onehot, cat[:, None], axis=1)[:, 0]
    counts = jnp.sum(onehot, axis=0)
    padded = ((counts + MOE_TM - 1) // MOE_TM) * MOE_TM
    ends = jnp.cumsum(padded)
    pos = (ends - padded)[cat] + rank
    tok_ids = jnp.arange(t, dtype=jnp.int32)
    src = jnp.zeros((n_rows,), jnp.int32).at[pos].set(tok_ids)
    n_used = ends[-1] // MOE_TM
    tile_start = jnp.minimum(jnp.arange(n_tiles), n_used - 1) * MOE_TM
    tile_cat = jnp.sum((ends[None, :] <= tile_start[:, None]).astype(jnp.int32), axis=1)
    n_valid = jnp.clip((ends - padded + counts)[tile_cat] - tile_start, 0, MOE_TM)
    pa = jnp.array([p[0] for p in PAIRS], jnp.int32)[tile_cat % len(PAIRS)]
    pb = jnp.array([p[1] for p in PAIRS], jnp.int32)[tile_cat % len(PAIRS)]
    base = (tile_cat // len(PAIRS)) * 4
    i32 = lambda a: a.astype(jnp.int32)
    return i32(base + pa), i32(base + pb), i32(n_used).reshape(1), i32(n_valid), src


def _moe_kernel(ea_ref, eb_ref, nu_ref, nv_ref, tok_ref,
                h_hbm, wga_ref, wua_ref, wda_ref, wgb_ref, wub_ref, wdb_ref, y_hbm,
                xbuf, ybuf, bin_scr, gsem, ssem):
    i = pl.program_id(0)
    n_used = nu_ref[0]

    def by_eights(row_fn):
        def body(c, carry):
            for u in range(8):
                row_fn(c * 8 + u, carry)
            return carry
        return body

    def issue_gather(tile, sl):
        def row(r, carry):
            tok = tok_ref[tile * MOE_TM + r]
            pltpu.make_async_copy(h_hbm.at[pl.ds(tok, 1)], xbuf.at[sl, pl.ds(r, 1)], gsem.at[sl]).start()
            return carry
        lax.fori_loop(0, MOE_TM // 8, by_eights(row), 0)

    def wait_gather(sl):
        pltpu.make_async_copy(h_hbm.at[pl.ds(0, MOE_TM)], xbuf.at[sl], gsem.at[sl]).wait()

    def issue_scatter(tile, sl):
        def to_hbm(r, carry):
            tok = tok_ref[tile * MOE_TM + r]
            pltpu.make_async_copy(ybuf.at[sl, pl.ds(r, 1)], y_hbm.at[pl.ds(tok, 1)], ssem.at[sl]).start()
            return carry

        def to_bin(r, carry):
            pltpu.make_async_copy(ybuf.at[sl, pl.ds(r, 1)], bin_scr.at[sl, pl.ds(r, 1)], ssem.at[sl]).start()
            return carry
        nv = nv_ref[tile]
        nv_down = lax.shift_right_logical(nv, 3)
        nv_up = lax.shift_right_logical(nv + 7, 3)
        lax.fori_loop(0, nv_down, by_eights(to_hbm), 0)
        lax.fori_loop(nv_down * 8, nv, to_hbm, 0)
        lax.fori_loop(nv, nv_up * 8, to_bin, 0)
        lax.fori_loop(nv_up, MOE_TM // 8, by_eights(to_bin), 0)

    def wait_scatter(sl):
        pltpu.make_async_copy(ybuf.at[sl], y_hbm.at[pl.ds(0, MOE_TM)], ssem.at[sl]).wait()

    @pl.when(i == 0)
    def _():
        issue_gather(0, 0)

    def tile(slot):
        wait_gather(slot)

        @pl.when(i >= 2)
        def _():
            wait_scatter(slot)

        nxt = jnp.minimum(i + 1, n_used - 1) * MOE_TM
        for r in range(MOE_TM):
            pltpu.make_async_copy(h_hbm.at[pl.ds(tok_ref[nxt + r], 1)], xbuf.at[1 - slot, pl.ds(r, 1)],
                                  gsem.at[1 - slot]).start()

        x = xbuf[slot, :, :D].astype(BF16)
        w = xbuf[slot, :, D:]

        def expert(wg_ref, wu_ref, wd_ref):
            a = _dot(x, wg_ref[0].astype(BF16))
            hid = (a * jax.nn.sigmoid(a) * _dot(x, wu_ref[0].astype(BF16))).astype(BF16)
            return _dot(hid, wd_ref[0].astype(BF16))

        ybuf[slot] = (w[:, 0:1] * expert(wga_ref, wua_ref, wda_ref)
                      + w[:, 1:2] * expert(wgb_ref, wub_ref, wdb_ref))
        issue_scatter(i, slot)

    for parity in range(2):
        pl.when(jnp.logical_and(i < n_used, i % 2 == parity))(functools.partial(tile, parity))

    @pl.when(i == pl.num_programs(0) - 1)
    def _():
        wait_scatter(0)
        wait_scatter(1)
        wait_gather(n_used % 2)


def _moe(h2, ridx, w_gate, w_up, w_down):
    t = h2.shape[0]
    ea, eb, n_used, n_valid, src = _route_tables(ridx, t)
    n_tiles = ea.shape[0]
    assert t // MOE_TM >= 2
    wspec = lambda which, shape: pl.BlockSpec(
        (1,) + shape, (lambda i, ea, eb, nu, nv, tok: (ea[i], 0, 0)) if which == 0
        else (lambda i, ea, eb, nu, nv, tok: (eb[i], 0, 0)))
    wg, wu, wd = w_gate, w_up, w_down
    return pl.pallas_call(
        _moe_kernel,
        grid_spec=pltpu.PrefetchScalarGridSpec(
            num_scalar_prefetch=5,
            grid=(n_tiles,),
            in_specs=[pl.BlockSpec(memory_space=pl.ANY),
                      wspec(0, (D, D_FF)), wspec(0, (D, D_FF)), wspec(0, (D_FF, D)),
                      wspec(1, (D, D_FF)), wspec(1, (D, D_FF)), wspec(1, (D_FF, D))],
            out_specs=pl.BlockSpec(memory_space=pl.ANY),
            scratch_shapes=[pltpu.VMEM((2, MOE_TM, MOE_ROW), F32), pltpu.VMEM((2, MOE_TM, D), F32),
                            pltpu.VMEM((2, MOE_TM, D), F32),
                            pltpu.SemaphoreType.DMA((2,)), pltpu.SemaphoreType.DMA((2,))]),
        out_shape=jax.ShapeDtypeStruct((t, D), F32),
        compiler_params=_cparams(("arbitrary",)),
    )(ea, eb, n_used, n_valid, src, h2, wg, wu, wd, wg, wu, wd)


def kernel(x, c, ctx, c_ctx, ada_w, ada_b, norm1_g, norm2_g, final_g, s5_lambda_re, s5_lambda_im, s5_log_step, s5_b_re, s5_b_im, s5_c_re, s5_c_im, s5_d, s5_w_glu, s5_b_glu, attn_w_kvq, attn_q_gain, attn_k_gain, attn_w_o, ml_w_in, ml_b_gate, ml_norm_g, ml_w_out, router_w, router_bias, moe_w_gate, moe_w_up, moe_w_down):
    nb = x.shape[0]
    depth = ada_w.shape[0]
    assert x.shape[1:] == (SEQ, D) and ctx.shape[1:] == (CTX, D) and nb + 1 <= 16
    xs = jnp.concatenate([ctx, x], axis=1).reshape(nb * NTOK, D)
    cond_in = jnp.zeros((16, D), F32).at[:nb].set(c).at[nb].set(c_ctx)
    mods = _ada(cond_in, ada_w, ada_b).reshape(depth, 16, 6, D)

    y2 = None
    for layer in range(depth):
        kind, j = layer % N_MIXERS, layer // N_MIXERS
        xs, h = _pre(xs, y2, mods[layer - 1] if layer else None, mods[layer], norm1_g[layer], nb,
                     F32 if kind == 0 else BF16, chunk_major=kind == 0)
        if kind == 0:
            y = _s5_mixer(h, nb, s5_lambda_re[j], s5_lambda_im[j], s5_log_step[j], s5_b_re[j], s5_b_im[j],
                          s5_c_re[j], s5_c_im[j], s5_d[j], s5_w_glu[j], s5_b_glu[j])
        elif kind == 1:
            y = _gqa_mixer(h, nb, attn_w_kvq[j], attn_q_gain[j], attn_k_gain[j], attn_w_o[j])
        else:
            y = _mlstm_mixer(h, nb, ml_w_in[j], ml_b_gate[j], ml_norm_g[j], ml_w_out[j])
        xs, h2, ridx = _post(xs, y, mods[layer], norm2_g[layer], router_w, router_bias, nb)
        y2 = _moe(h2, ridx, moe_w_gate[layer], moe_w_up[layer], moe_w_down[layer])
    out = _final(xs, y2, mods[depth - 1], final_g, nb)
    return out.reshape(nb, SEQ, D)
```

```python
import functools
import math

import jax
import jax.numpy as jnp
import numpy as np
from jax import lax
from jax.experimental import pallas as pl
from jax.experimental.pallas import tpu as pltpu

F32 = jnp.float32
BF16 = jnp.bfloat16

D = 1024
CTX = 256
SEQ = 2048
NTOK = CTX + SEQ
GRID_W = 64
EPS = 1e-6
N_MIXERS = 3

TOK = 256
TILES_PER_B = NTOK // TOK
LIN_TM = 512

S5_GROUP = 16
S5_GROUPS = D // S5_GROUP
S5_STATE = 64
S5_CHUNK = 16
S5_NCH = NTOK // S5_CHUNK
S5_CTX_CH = CTX // S5_CHUNK
S5_SLAB_GROUPS = 128 // S5_GROUP

HEAD_DIM = 64
N_Q_HEADS = 16
N_KV_HEADS = 4
Q_PER_KV = 4
KV_COLS = 2 * N_KV_HEADS * HEAD_DIM
ROPE_THETA = 10000.0
ATT_TQ = 128

ML_HEADS = 8
ML_DQK = 64
ML_DV = 128
ML_QK = 512
ML_CHUNK = 128
ML_NCH = NTOK // ML_CHUNK
ML_CTX_CH = CTX // ML_CHUNK

N_EXPERTS = 16
N_GROUPS = 4
D_FF = 512
MOE_TM = 256
MOE_ROW = D + 128

VMEM_LIMIT = 56 * 1024 * 1024


def _cparams(sem):
    return pltpu.CompilerParams(dimension_semantics=sem, vmem_limit_bytes=VMEM_LIMIT)


def _nt_dot(a, b, **kw):
    return lax.dot_general(a, b, (((1,), (1,)), ((), ())), preferred_element_type=F32, **kw)


def _dot(a, b):
    return jnp.dot(a, b, preferred_element_type=F32)


def _rms(x):
    return x * lax.rsqrt(jnp.mean(x * x, axis=-1, keepdims=True) + EPS)


def _ada_kernel(c_ref, w_ref, b_ref, o_ref):
    c = c_ref[...]
    cond = (c * jax.nn.sigmoid(c)).astype(BF16)
    o_ref[0] = _dot(cond, w_ref[0].astype(BF16)) + b_ref[0]


def _ada(cond_in, ada_w, ada_b):
    depth, _, n = ada_w.shape
    tn = 1536
    rows = cond_in.shape[0]
    return pl.pallas_call(
        _ada_kernel,
        grid=(depth, n // tn),
        in_specs=[pl.BlockSpec((rows, D), lambda l, j: (0, 0)),
                  pl.BlockSpec((1, D, tn), lambda l, j: (l, 0, j)),
                  pl.BlockSpec((1, 1, tn), lambda l, j: (l, 0, j))],
        out_specs=pl.BlockSpec((1, rows, tn), lambda l, j: (l, 0, j)),
        out_shape=jax.ShapeDtypeStruct((depth, rows, n), F32),
        compiler_params=_cparams(("parallel", "parallel")),
    )(cond_in, ada_w, ada_b.reshape(depth, 1, n))


def _mod_row(nb):
    return lambda i: (jnp.where(i % TILES_PER_B == 0, nb, i // TILES_PER_B), 0, 0)


def _pre_kernel(has_res, *refs):
    if has_res:
        x_ref, y_ref, mprev_ref, mcur_ref, g_ref, xo_ref, h_ref = refs
        x = x_ref[...] + mprev_ref[0, 5:6, :] * y_ref[...]
        xo_ref[...] = x
    else:
        x_ref, mcur_ref, g_ref, h_ref = refs
        x = x_ref[...]
    h = _rms(x) * g_ref[...]
    h = h * (1.0 + mcur_ref[0, 1:2, :]) + mcur_ref[0, 0:1, :]
    h_ref[...] = h.astype(h_ref.dtype).reshape(h_ref.shape)


def _chunk_major(nb):
    shape = (S5_NCH, nb, S5_CHUNK, D)
    spec = pl.BlockSpec((TOK // S5_CHUNK, 1, S5_CHUNK, D), lambda i: (i % TILES_PER_B, i // TILES_PER_B, 0, 0))
    return shape, spec


def _pre(x, y2, mods_prev, mods_cur, g, nb, h_dtype, chunk_major=False):
    t = x.shape[0]
    tok = pl.BlockSpec((TOK, D), lambda i: (i, 0))
    mod = pl.BlockSpec((1, 6, D), _mod_row(nb))
    vec = pl.BlockSpec((1, D), lambda i: (0, 0))
    h_shape, h_spec = _chunk_major(nb) if chunk_major else ((t, D), tok)
    if y2 is None:
        return x, pl.pallas_call(
            functools.partial(_pre_kernel, False),
            grid=(t // TOK,), in_specs=[tok, mod, vec], out_specs=h_spec,
            out_shape=jax.ShapeDtypeStruct(h_shape, h_dtype),
            compiler_params=_cparams(("parallel",)),
        )(x, mods_cur, g.reshape(1, D))
    return pl.pallas_call(
        functools.partial(_pre_kernel, True),
        grid=(t // TOK,), in_specs=[tok, tok, mod, mod, vec], out_specs=[tok, h_spec],
        out_shape=[jax.ShapeDtypeStruct((t, D), F32), jax.ShapeDtypeStruct(h_shape, h_dtype)],
        compiler_params=_cparams(("parallel",)),
    )(x, y2, mods_prev, mods_cur, g.reshape(1, D))


def _post_kernel(x_ref, y_ref, m_ref, g_ref, rwt_ref, rb_ref, xo_ref, h_ref, ridx_ref):
    x = x_ref[...] + m_ref[0, 2:3, :] * y_ref[...].reshape(x_ref.shape)
    xo_ref[...] = x
    h2 = _rms(x) * g_ref[...]
    h2 = h2 * (1.0 + m_ref[0, 4:5, :]) + m_ref[0, 3:4, :]
    h_ref[:, :D] = h2

    logits = _nt_dot(rwt_ref[...], h2, precision=lax.Precision.HIGHEST)
    scores = jax.nn.sigmoid(logits)
    sel = scores + rb_ref[...]
    row = lax.broadcasted_iota(jnp.int32, sel.shape, 0)
    per_group = N_EXPERTS // N_GROUPS
    best_val, best = None, None
    for g in range(N_GROUPS):
        r = [sel[per_group * g + j:per_group * g + j + 1, :] for j in range(per_group)]
        gs = None
        for i in range(per_group):
            for j in range(i + 1, per_group):
                gs = r[i] + r[j] if gs is None else jnp.maximum(gs, r[i] + r[j])
        if g == 0:
            best_val, best = gs, jnp.zeros(gs.shape, jnp.int32)
        else:
            upd = gs > best_val
            best = jnp.where(upd, g, best)
            best_val = jnp.where(upd, gs, best_val)
    masked = jnp.where((row // per_group) == best, sel, -jnp.inf)
    m1 = jnp.max(masked, axis=0, keepdims=True)
    i1 = jnp.min(jnp.where(masked == m1, row, N_EXPERTS), axis=0, keepdims=True)
    masked2 = jnp.where(row == i1, -jnp.inf, masked)
    m2 = jnp.max(masked2, axis=0, keepdims=True)
    i2 = jnp.min(jnp.where(masked2 == m2, row, N_EXPERTS), axis=0, keepdims=True)
    lo, hi = jnp.minimum(i1, i2), jnp.maximum(i1, i2)
    s_lo = jnp.sum(jnp.where(row == lo, scores, 0.0), axis=0, keepdims=True)
    s_hi = jnp.sum(jnp.where(row == hi, scores, 0.0), axis=0, keepdims=True)
    ridx_ref[...] = jnp.concatenate([lo, hi], axis=0)
    wts = jnp.concatenate([s_lo, s_hi, jnp.zeros((126, TOK), F32)], axis=0) / (s_lo + s_hi)
    h_ref[:, D:] = wts.T


def _post(x, y, mods, g, router_w, router_bias, nb):
    t = x.shape[0]
    tok = pl.BlockSpec((TOK, D), lambda i: (i, 0))
    return pl.pallas_call(
        _post_kernel,
        grid=(t // TOK,),
        in_specs=[tok, _chunk_major(nb)[1] if y.ndim == 4 else tok, pl.BlockSpec((1, 6, D), _mod_row(nb)),
                  pl.BlockSpec((1, D), lambda i: (0, 0)),
                  pl.BlockSpec((N_EXPERTS, D), lambda i: (0, 0)),
                  pl.BlockSpec((N_EXPERTS, 1), lambda i: (0, 0))],
        out_specs=[tok, pl.BlockSpec((TOK, MOE_ROW), lambda i: (i, 0)), pl.BlockSpec((2, TOK), lambda i: (0, i))],
        out_shape=[jax.ShapeDtypeStruct((t, D), F32), jax.ShapeDtypeStruct((t, MOE_ROW), F32),
                   jax.ShapeDtypeStruct((2, t), jnp.int32)],
        compiler_params=_cparams(("parallel",)),
    )(x, y, mods, g.reshape(1, D), router_w.T, router_bias.reshape(N_EXPERTS, 1))


def _final_kernel(x_ref, y_ref, m_ref, g_ref, o_ref):
    x = x_ref[...] + m_ref[0, 5:6, :] * y_ref[...]
    o_ref[...] = _rms(x) * g_ref[...]


def _final(x, y2, mods, g, nb):
    lat_tiles = SEQ // TOK
    src = lambda j: ((j // lat_tiles) * TILES_PER_B + CTX // TOK + j % lat_tiles, 0)
    tok_in = pl.BlockSpec((TOK, D), src)
    return pl.pallas_call(
        _final_kernel,
        grid=(nb * lat_tiles,),
        in_specs=[tok_in, tok_in, pl.BlockSpec((1, 6, D), lambda j: (j // lat_tiles, 0, 0)),
                  pl.BlockSpec((1, D), lambda j: (0, 0))],
        out_specs=pl.BlockSpec((TOK, D), lambda j: (j, 0)),
        out_shape=jax.ShapeDtypeStruct((nb * SEQ, D), F32),
        compiler_params=_cparams(("parallel",)),
    )(x, y2, mods, g.reshape(1, D))


def _linear_kernel(x_ref, w_ref, o_ref):
    o_ref[...] = _dot(x_ref[...].astype(BF16), w_ref[...]).astype(o_ref.dtype)


def _linear(x, w, out_dtype):
    t, k = x.shape
    n = w.shape[1]
    return pl.pallas_call(
        _linear_kernel,
        grid=(t // LIN_TM,),
        in_specs=[pl.BlockSpec((LIN_TM, k), lambda i: (i, 0)), pl.BlockSpec((k, n), lambda i: (0, 0))],
        out_specs=pl.BlockSpec((LIN_TM, n), lambda i: (i, 0)),
        out_shape=jax.ShapeDtypeStruct((t, n), out_dtype),
        compiler_params=_cparams(("parallel",)),
    )(x, w)


def _s5_operators(lam_re, lam_im, log_step, b_re, b_im, c_re, c_im):
    n = S5_CHUNK
    dt = jnp.exp(log_step)[None, :, :, None]
    k = jnp.arange(n + 1, dtype=F32)[:, None, None, None]
    mag = jnp.exp(k * lam_re[None] * dt)
    ang = k * lam_im[None] * dt
    pw_re, pw_im = mag * jnp.cos(ang), mag * jnp.sin(ang)
    a_re, a_im = pw_re[1], pw_im[1]
    den = lam_re * lam_re + lam_im * lam_im
    n_re, n_im = a_re - 1.0, a_im
    f_re = (n_re * lam_re + n_im * lam_im) / den
    f_im = (n_im * lam_re - n_re * lam_im) / den
    bt_re, bt_im = b_re.swapaxes(-1, -2), b_im.swapaxes(-1, -2)
    bb_re = f_re[:, :, None, :] * bt_re - f_im[:, :, None, :] * bt_im
    bb_im = f_re[:, :, None, :] * bt_im + f_im[:, :, None, :] * bt_re

    cp_re = c_re[None] * pw_re[:, :, :, None, :] - c_im[None] * pw_im[:, :, :, None, :]
    cp_im = c_re[None] * pw_im[:, :, :, None, :] + c_im[None] * pw_re[:, :, :, None, :]
    kk = (jnp.einsum('dgcp,kdgop->dgcko', bb_re, cp_re[:n]) - jnp.einsum('dgcp,kdgop->dgcko', bb_im, cp_im[:n]))
    wide = n * S5_GROUP
    lags = jnp.concatenate([kk[0].reshape(S5_GROUPS, S5_GROUP, wide),
                            kk[1, :, :, ::-1].reshape(S5_GROUPS, S5_GROUP, wide)], axis=1)

    def state_in(pw_sel, d):
        pr = pw_re[pw_sel, d].swapaxes(0, 1)[:, :, None, :]
        pi = pw_im[pw_sel, d].swapaxes(0, 1)[:, :, None, :]
        e_re = pr * bb_re[d][:, None] - pi * bb_im[d][:, None]
        e_im = pr * bb_im[d][:, None] + pi * bb_re[d][:, None]
        to_w = lambda e: jnp.pad(e.reshape(S5_GROUPS, wide, S5_STATE), ((0, 0), (0, 0), (0, 128 - S5_STATE)))
        return to_w(e_re), to_w(e_im)

    def state_out(pw_sel, d):
        ct_re = jnp.tile(c_re[d].swapaxes(-1, -2), (1, 1, n))
        ct_im = jnp.tile(c_im[d].swapaxes(-1, -2), (1, 1, n))
        pr = jnp.repeat(pw_re[pw_sel, d].transpose(1, 2, 0), S5_GROUP, axis=2)
        pi = jnp.repeat(pw_im[pw_sel, d].transpose(1, 2, 0), S5_GROUP, axis=2)
        to_w = lambda o: jnp.pad(o, ((0, 0), (0, 128 - S5_STATE), (0, 0)))
        return to_w(ct_re * pr - ct_im * pi), to_w(-(ct_re * pi + ct_im * pr))

    ar = jnp.arange(n)
    w1 = jnp.concatenate([*state_in(n - 1 - ar, 0), *state_in(ar, 1)], axis=2)
    w2 = jnp.concatenate([*state_out(ar + 1, 0), *state_out(n - ar, 1)], axis=1)
    pad = lambda a: jnp.pad(a, ((0, 0), (0, 128 - S5_STATE)))
    a16 = jnp.stack([pad(pw_re[n, 0]), pad(pw_im[n, 0]), pad(pw_re[n, 1]), pad(pw_im[n, 1])], axis=1)
    return lags, w1.astype(BF16), w2.astype(BF16), a16


def _s5_select():
    sel = np.zeros((S5_SLAB_GROUPS, S5_CHUNK * 128, S5_CHUNK * S5_GROUP), np.float32)
    s, c = np.meshgrid(np.arange(S5_CHUNK), np.arange(S5_GROUP), indexing="ij")
    for j in range(S5_SLAB_GROUPS):
        sel[j, s * 128 + S5_GROUP * j + c, s * S5_GROUP + c] = 1.0
    return jnp.asarray(sel, BF16)


def _s5_kernel(nb, h_ref, sel_ref, lag_ref, w1_ref, w2_ref, a_ref, y_ref, x2_scr, s_scr, x_scr, toep_scr):
    rows = S5_NCH * nb

    @pl.when(pl.program_id(1) == 0)
    def _():
        for s in range(S5_CHUNK):
            x2_scr[:, 128 * s:128 * (s + 1)] = h_ref[pl.ds(s, rows, stride=S5_CHUNK), :].astype(BF16)
        y_ref[...] = jnp.zeros(y_ref.shape, F32)

    wide = S5_CHUNK * S5_GROUP
    zeros = jnp.zeros((S5_GROUP, wide), F32)
    kf = jnp.concatenate([lag_ref[0, :S5_GROUP, :], zeros], axis=1)
    kb = jnp.concatenate([lag_ref[0, S5_GROUP:, :], zeros], axis=1)
    for s in range(S5_CHUNK):
        right, left = S5_GROUP * s, S5_GROUP * (S5_CHUNK - 1 - s)
        blk = ((pltpu.roll(kf, right, 1) if right else kf)[:, :wide]
               + (pltpu.roll(kb, 2 * wide - left, 1) if left else kb)[:, :wide])
        toep_scr[S5_GROUP * s:S5_GROUP * (s + 1), :] = blk.astype(BF16)

    u = _dot(x2_scr[...], sel_ref[0]).astype(BF16)
    s_scr[:, 0:wide] = _dot(u, toep_scr[...])
    s_scr[:, wide:] = _dot(u, w1_ref[0])

    def step(i, carry):
        cf = i
        cb = jnp.where(i < S5_CTX_CH, S5_CTX_CH - 1 - i, S5_NCH + S5_CTX_CH - 1 - i)
        rows_f = pl.ds(pl.multiple_of(cf * nb, nb), nb)
        rows_b = pl.ds(pl.multiple_of(cb * nb, nb), nb)
        fr, fi, br, bi = carry
        x_scr[rows_f, 0:128] = fr
        x_scr[rows_f, 128:256] = fi
        x_scr[rows_b, 256:384] = br
        x_scr[rows_b, 384:512] = bi
        ar, ai = a_ref[0, 0:1, :], a_ref[0, 1:2, :]
        nfr = ar * fr - ai * fi + s_scr[rows_f, 256:384]
        nfi = ar * fi + ai * fr + s_scr[rows_f, 384:512]
        ar, ai = a_ref[0, 2:3, :], a_ref[0, 3:4, :]
        nbr = ar * br - ai * bi + s_scr[rows_b, 512:640]
        nbi = ar * bi + ai * br + s_scr[rows_b, 640:768]
        return nfr, nfi, nbr, nbi

    zero = jnp.zeros((nb, 128), F32)
    lax.fori_loop(0, S5_NCH, step, (zero, zero, zero, zero))
    yg = s_scr[:, 0:256] + _dot(x_scr[...].astype(BF16), w2_ref[0])
    y_ref[0] += _nt_dot(yg.astype(BF16), sel_ref[0])


def _s5_scan(h_cm, lags, w1, w2, a16, nb):
    rows = S5_NCH * nb
    kc = S5_CHUNK * S5_GROUP
    n_slabs = D // 128
    per_group = lambda shape: pl.BlockSpec((1,) + shape, lambda q, j: (q * S5_SLAB_GROUPS + j, 0, 0))
    return pl.pallas_call(
        functools.partial(_s5_kernel, nb),
        grid=(n_slabs, S5_SLAB_GROUPS),
        in_specs=[pl.BlockSpec((nb * NTOK, 128), lambda q, j: (0, q)),
                  pl.BlockSpec((1, S5_CHUNK * 128, kc), lambda q, j: (j, 0, 0)),
                  per_group((2 * S5_GROUP, kc)), per_group((kc, 512)), per_group((512, kc)), per_group((4, 128))],
        out_specs=pl.BlockSpec((1, rows, S5_CHUNK * 128), lambda q, j: (q, 0, 0)),
        out_shape=jax.ShapeDtypeStruct((n_slabs, rows, S5_CHUNK * 128), F32),
        scratch_shapes=[pltpu.VMEM((rows, S5_CHUNK * 128), BF16), pltpu.VMEM((rows, 768), F32),
                        pltpu.VMEM((rows, 512), F32), pltpu.VMEM((kc, kc), BF16)],
        compiler_params=_cparams(("parallel", "arbitrary")),
    )(h_cm.reshape(nb * NTOK, D), _s5_select(), lags, w1, w2, a16)


def _gelu(x):
    return 0.5 * x * (1.0 + jnp.tanh(math.sqrt(2.0 / math.pi) * (x + 0.044715 * (x * x * x))))


def _glu_kernel(y_ref, h_ref, d_ref, w_ref, b_ref, o_ref, ys_scr):
    rows = TOK // S5_CHUNK
    for q in range(D // 128):
        for s in range(S5_CHUNK):
            ys_scr[q, pl.ds(s, rows, stride=S5_CHUNK), :] = y_ref[q, :, 128 * s:128 * (s + 1)]
    y = jnp.concatenate([ys_scr[q] for q in range(D // 128)], axis=1)
    z = _gelu(y + d_ref[...] * h_ref[...]).astype(BF16)
    r = _dot(z, w_ref[...]) + b_ref[...]
    o_ref[...] = r[:, :D] * jax.nn.sigmoid(r[:, D:])


def _glu(y, h, d_skip, w, b):
    t = h.shape[0]
    tok = pl.BlockSpec((TOK, D), lambda i: (i, 0))
    return pl.pallas_call(
        _glu_kernel,
        grid=(t // TOK,),
        in_specs=[pl.BlockSpec((D // 128, TOK // S5_CHUNK, S5_CHUNK * 128), lambda i: (0, i, 0)), tok,
                  pl.BlockSpec((1, D), lambda i: (0, 0)),
                  pl.BlockSpec((D, 2 * D), lambda i: (0, 0)), pl.BlockSpec((1, 2 * D), lambda i: (0, 0))],
        out_specs=tok,
        out_shape=jax.ShapeDtypeStruct((t, D), F32),
        scratch_shapes=[pltpu.VMEM((D // 128, TOK, 128), F32)],
        compiler_params=_cparams(("parallel",)),
    )(y, h, d_skip.reshape(1, D), w, b.reshape(1, 2 * D))


def _s5_mixer(h_cm, nb, lam_re, lam_im, log_step, b_re, b_im, c_re, c_im, d_skip, w_glu, b_glu):
    lags, w1, w2, a16 = _s5_operators(lam_re, lam_im, log_step, b_re, b_im, c_re, c_im)
    y = _s5_scan(h_cm, lags, w1, w2, a16, nb)
    y = _glu(y, h_cm.reshape(nb * NTOK, D), d_skip, w_glu.astype(BF16), b_glu)
    return y.reshape(h_cm.shape)


def _rope_tables():
    rows = SEQ // GRID_W
    row = jnp.repeat(jnp.arange(rows), GRID_W).astype(F32)
    col = jnp.tile(jnp.arange(GRID_W), rows).astype(F32)
    half = HEAD_DIM // 4
    inv_freq = ROPE_THETA ** (-jnp.arange(half, dtype=F32) / half)
    ang_r, ang_c = row[:, None] * inv_freq, col[:, None] * inv_freq
    cos = jnp.concatenate([jnp.cos(ang_r)] * 2 + [jnp.cos(ang_c)] * 2, axis=1)
    sin = jnp.concatenate([-jnp.sin(ang_r), jnp.sin(ang_r), -jnp.sin(ang_c), jnp.sin(ang_c)], axis=1)
    cos = jnp.concatenate([jnp.ones((CTX, HEAD_DIM), F32), cos], axis=0)
    sin = jnp.concatenate([jnp.zeros((CTX, HEAD_DIM), F32), sin], axis=0)
    return jnp.tile(cos, (1, 2)), jnp.tile(sin, (1, 2))


def _qk_prep_kernel(p_ref, cos_ref, sin_ref, qg_ref, kg_ref, q_ref, k_ref, v_ref):
    lane = lax.broadcasted_iota(jnp.int32, (TOK, 128), 1)
    low_head = lane < HEAD_DIM
    first_half = (lane % 32) < 16
    cos, sin = cos_ref[...], sin_ref[...]

    def norm_rope(x, gain):
        sq = x * x
        s_lo = jnp.sum(jnp.where(low_head, sq, 0.0), axis=1, keepdims=True)
        s_hi = jnp.sum(sq, axis=1, keepdims=True) - s_lo
        inv = lax.rsqrt(jnp.where(low_head, s_lo, s_hi) * (1.0 / HEAD_DIM) + EPS)
        xn = x * inv * gain
        partner = jnp.where(first_half, pltpu.roll(xn, 128 - 16, 1), pltpu.roll(xn, 16, 1))
        return xn * cos + partner * sin

    ones_col = (lax.broadcasted_iota(jnp.int32, (TOK, HEAD_DIM), 1) == 0).astype(BF16)
    for j in range(KV_COLS // 2 // 128):
        kt = norm_rope(p_ref[:, 128 * j:128 * (j + 1)], kg_ref[...]).astype(BF16)
        vt = p_ref[:, KV_COLS // 2 + 128 * j:KV_COLS // 2 + 128 * (j + 1)].astype(BF16)
        for hh in range(2):
            k_ref[0, 2 * j + hh] = kt[:, HEAD_DIM * hh:HEAD_DIM * (hh + 1)]
            v_ref[0, 2 * j + hh] = jnp.concatenate([vt[:, HEAD_DIM * hh:HEAD_DIM * (hh + 1)], ones_col], axis=1)
    for j in range(D // 128):
        qt = norm_rope(p_ref[:, KV_COLS + 128 * j:KV_COLS + 128 * (j + 1)], qg_ref[...])
        q_ref[:, 128 * j:128 * (j + 1)] = (qt * (HEAD_DIM ** -0.5 * math.log2(math.e))).astype(BF16)


def _qk_prep(proj, q_gain, k_gain, nb):
    t, ncol = proj.shape
    cos, sin = _rope_tables()
    tab = pl.BlockSpec((TOK, 128), lambda i: (i % TILES_PER_B, 0))
    gain = pl.BlockSpec((1, 128), lambda i: (0, 0))
    head_major = lambda width: pl.BlockSpec((1, N_KV_HEADS, TOK, width),
                                            lambda i: (i // TILES_PER_B, 0, i % TILES_PER_B, 0))
    hm_shape = lambda width: jax.ShapeDtypeStruct((nb, N_KV_HEADS, NTOK, width), BF16)
    return pl.pallas_call(
        _qk_prep_kernel,
        grid=(t // TOK,),
        in_specs=[pl.BlockSpec((TOK, ncol), lambda i: (i, 0)), tab, tab, gain, gain],
        out_specs=[pl.BlockSpec((TOK, D), lambda i: (i, 0)), head_major(HEAD_DIM), head_major(2 * HEAD_DIM)],
        out_shape=[jax.ShapeDtypeStruct((t, D), BF16), hm_shape(HEAD_DIM), hm_shape(2 * HEAD_DIM)],
        compiler_params=_cparams(("parallel",)),
    )(proj, cos, sin, jnp.tile(q_gain, 2).reshape(1, 128), jnp.tile(k_gain, 2).reshape(1, 128))


def _attn_kernel(q_ref, k_ref, v_ref, o_ref):
    def attend(n_keys):
        outs = []
        for kh in range(N_KV_HEADS):
            qs = jnp.concatenate(
                [q_ref[0, :, HEAD_DIM * (Q_PER_KV * kh + g):HEAD_DIM * (Q_PER_KV * kh + g + 1)]
                 for g in range(Q_PER_KV)], axis=0)
            s = _nt_dot(qs, k_ref[0, kh, :n_keys, :])
            p = jnp.exp2(s - jnp.max(s, axis=1, keepdims=True))
            ov = _dot(p.astype(BF16), v_ref[0, kh, :n_keys, :])
            o = ov[:, :HEAD_DIM] / ov[:, HEAD_DIM:HEAD_DIM + 1]
            outs += [o[ATT_TQ * g:ATT_TQ * (g + 1)] for g in range(Q_PER_KV)]
        o_ref[0] = jnp.concatenate(outs, axis=1).astype(BF16)

    is_ctx = pl.program_id(1) < CTX // ATT_TQ

    @pl.when(is_ctx)
    def _():
        attend(CTX)

    @pl.when(jnp.logical_not(is_ctx))
    def _():
        attend(NTOK)


def _attention(q, k, v, nb):
    kv = lambda width: pl.BlockSpec((1, N_KV_HEADS, NTOK, width), lambda b, i: (b, 0, 0, 0))
    qo = pl.BlockSpec((1, ATT_TQ, D), lambda b, i: (b, i, 0))
    return pl.pallas_call(
        _attn_kernel,
        grid=(nb, NTOK // ATT_TQ),
        in_specs=[qo, kv(HEAD_DIM), kv(2 * HEAD_DIM)], out_specs=qo,
        out_shape=jax.ShapeDtypeStruct((nb, NTOK, D), BF16),
        compiler_params=_cparams(("parallel", "parallel")),
    )(q.reshape(nb, NTOK, D), k, v)


def _gqa_mixer(h, nb, w_kvq, q_gain, k_gain, w_o):
    proj = _linear(h, w_kvq.astype(BF16), F32)
    q, k, v = _qk_prep(proj, q_gain, k_gain, nb)
    o = _attention(q, k, v, nb)
    return _linear(o.reshape(nb * NTOK, D), w_o.astype(BF16), F32)


def _split3(x):
    hi = x.astype(BF16).astype(F32)
    r = x - hi
    mid = r.astype(BF16).astype(F32)
    lo = (r - mid).astype(BF16).astype(F32)
    return hi, mid, lo


def _log_sigmoid(x):
    return jnp.minimum(x, 0.0) - jnp.log1p(jnp.exp(-jnp.abs(x)))


def _mlstm_kernel(h_ref, k_ref, v_ref, q_ref, wg_ref, wgt_ref, bc_ref, br_ref, o_ref,
                  col_scr, row_scr, st_scr, cl_scr, c0_scr, mloc_scr, m0_scr):
    hb = h_ref[0]
    g_col = _dot(hb, wg_ref[0]) + bc_ref[0]
    g_row = _nt_dot(wgt_ref[0], hb) + br_ref[0]
    ii = lax.broadcasted_iota(jnp.int32, (ML_CHUNK, ML_CHUNK), 0)
    jj = lax.broadcasted_iota(jnp.int32, (ML_CHUNK, ML_CHUNK), 1)
    low = ii >= jj
    upp = ii <= jj
    tri_l, tri_u = low.astype(BF16), upp.astype(BF16)
    tri_lu_rows = jnp.concatenate([tri_l, tri_u], axis=0)
    tri_ul_cols = jnp.concatenate([tri_u, tri_l], axis=1)
    lane_c = lax.broadcasted_iota(jnp.int32, (ML_CHUNK, 8), 1)
    sub_r = lax.broadcasted_iota(jnp.int32, (8, ML_CHUNK), 0)
    for c in range(ML_NCH):
        sl = slice(ML_CHUNK * c, ML_CHUNK * (c + 1))
        gc = g_col[sl]
        parts = _split3(jnp.where(lane_c % 2 == 1, _log_sigmoid(gc), 0.0))
        cs = _dot(tri_lu_rows, jnp.concatenate(parts, axis=1).astype(BF16))
        cs = cs[:, 0:8] + cs[:, 8:16] + cs[:, 16:24]
        cum = jnp.where((lane_c // 2) % 2 == 1, cs[ML_CHUNK:], cs[:ML_CHUNK])
        col_scr[sl, :] = jnp.where(lane_c % 2 == 1, cum, gc)
        gr = g_row[:, sl]
        parts = _split3(jnp.where(sub_r % 2 == 1, _log_sigmoid(gr), 0.0))
        rs = _dot(jnp.concatenate(parts, axis=0).astype(BF16), tri_ul_cols)
        rs = rs[0:8] + rs[8:16] + rs[16:24]
        cum = jnp.where((sub_r // 2) % 2 == 1, rs[:, ML_CHUNK:], rs[:, :ML_CHUNK])
        row_scr[c] = jnp.where(sub_r % 2 == 1, cum, gr)

    ones_col = (lax.broadcasted_iota(jnp.int32, (ML_CHUNK, ML_DV), 1) == 0).astype(BF16)

    def chunk_rows(c):
        return pl.ds(pl.multiple_of(c * ML_CHUNK, ML_CHUNK), ML_CHUNK)

    def gates(c, idx):
        rows, rr = chunk_rows(c), row_scr[c]
        ig_c, cum_c = col_scr[rows, 2 * idx:2 * idx + 1], col_scr[rows, 2 * idx + 1:2 * idx + 2]
        ig_r, cum_r = rr[2 * idx:2 * idx + 1, :], rr[2 * idx + 1:2 * idx + 2, :]
        g = cum_r[:, ML_CHUNK - 1:ML_CHUNK] if idx % 2 == 0 else cum_r[:, 0:1]
        return ig_c, cum_c, ig_r, cum_r, g

    def keys_values(c, hl):
        rows = chunk_rows(c)
        ks = k_ref[0, rows, ML_DQK * hl:ML_DQK * (hl + 1)] * (ML_DQK ** -0.5)
        v_aug = jnp.concatenate([v_ref[0, rows, ML_DV * hl:ML_DV * (hl + 1)], ones_col], axis=1)
        return ks, v_aug

    scalar_tile = lambda m: jnp.broadcast_to(m, (8, 128))

    def local_state(c, carry):
        for hl in range(2):
            ks, v_aug = keys_values(c, hl)
            ke = []
            for d in range(2):
                ig_c, cum_c, _, _, g = gates(c, 2 * hl + d)
                w = g - cum_c + ig_c
                m_loc = jnp.max(w, axis=0, keepdims=True)
                mloc_scr[c, 2 * hl + d] = scalar_tile(m_loc)
                ke.append(ks.astype(F32) * jnp.exp(w - m_loc))
            ke = jnp.concatenate(ke, axis=1).astype(BF16)
            cl_scr[hl, c] = lax.dot_general(ke, v_aug, (((0,), (0,)), ((), ())), preferred_element_type=F32)
        return carry

    lax.fori_loop(0, ML_NCH, local_state, 0)

    st_scr[...] = jnp.zeros(st_scr.shape, F32)

    def recur(i, ms):
        new_ms = []
        for hl in range(2):
            for d in range(2):
                idx = 2 * hl + d
                if d == 0:
                    c = i
                else:
                    c = jnp.where(i < ML_CTX_CH, ML_CTX_CH - 1 - i, ML_NCH + ML_CTX_CH - 1 - i)
                g = gates(c, idx)[4]
                m0, c_aug = ms[idx], st_scr[idx]
                c0_scr[idx, c] = c_aug.astype(BF16)
                m0_scr[c, idx] = scalar_tile(m0)
                m_loc = mloc_scr[c, idx, 0:1, 0:1]
                m_new = jnp.maximum(g + m0, m_loc)
                st_scr[idx] = (jnp.exp(g + m0 - m_new) * c_aug
                               + jnp.exp(m_loc - m_new) * cl_scr[hl, c, ML_DQK * d:ML_DQK * (d + 1), :])
                new_ms.append(m_new)
        return tuple(new_ms)

    lax.fori_loop(0, ML_NCH, recur, tuple(jnp.zeros((1, 1), F32) for _ in range(4)))

    def outputs(c, carry):
        rows = chunk_rows(c)
        for hl in range(2):
            ks, v_aug = keys_values(c, hl)
            q = q_ref[0, rows, ML_DQK * hl:ML_DQK * (hl + 1)]
            raw = _nt_dot(q, ks)
            qc = _dot(q, jnp.concatenate([c0_scr[2 * hl, c], c0_scr[2 * hl + 1, c]], axis=1))
            s, a_inter, m_row = [], [], []
            for d in range(2):
                idx = 2 * hl + d
                ig_c, cum_c, ig_r, cum_r, _ = gates(c, idx)
                log_d = jnp.where(low if d == 0 else upp, cum_c - cum_r + ig_r, -jnp.inf)
                log_inter = cum_c + m0_scr[c, idx, 0:1, 0:1]
                m_row.append(jnp.maximum(log_inter, jnp.max(log_d, axis=1, keepdims=True)))
                s.append(raw * jnp.exp(log_d - m_row[d]))
                a_inter.append(jnp.exp(log_inter - m_row[d]))
            sv = _dot(jnp.concatenate(s, axis=0).astype(BF16), v_aug)
            h = None
            for d in range(2):
                nd = a_inter[d] * qc[:, 2 * ML_DV * d:2 * ML_DV * (d + 1)] + sv[ML_CHUNK * d:ML_CHUNK * (d + 1)]
                den = jnp.maximum(jnp.abs(nd[:, ML_DV:ML_DV + 1]), jnp.exp(-m_row[d]))
                h = nd[:, :ML_DV] / den if h is None else h + nd[:, :ML_DV] / den
            o_ref[0, rows, ML_DV * hl:ML_DV * (hl + 1)] = h
        return carry

    lax.fori_loop(0, ML_NCH, outputs, 0)


def _mlstm_scan(h, proj, wg, wgt, b_col, b_row, nb):
    pairs = ML_HEADS // 2
    k_blk = pl.BlockSpec((1, NTOK, 2 * ML_DQK), lambda b, p: (b, 0, p))
    v_blk = pl.BlockSpec((1, NTOK, 2 * ML_DV), lambda b, p: (b, 0, ML_QK // (2 * ML_DV) + p))
    q_blk = pl.BlockSpec((1, NTOK, 2 * ML_DQK), lambda b, p: (b, 0, (ML_QK + D) // (2 * ML_DQK) + p))
    per_pair = lambda shape: pl.BlockSpec((1,) + shape, lambda b, p: (p, 0, 0))
    proj3 = proj.reshape(nb, NTOK, proj.shape[-1])
    return pl.pallas_call(
        _mlstm_kernel,
        grid=(nb, pairs),
        in_specs=[pl.BlockSpec((1, NTOK, D), lambda b, p: (b, 0, 0)), k_blk, v_blk, q_blk,
                  per_pair((D, 8)), per_pair((8, D)), per_pair((1, 8)), per_pair((8, 1))],
        out_specs=pl.BlockSpec((1, NTOK, 2 * ML_DV), lambda b, p: (b, 0, p)),
        out_shape=jax.ShapeDtypeStruct((nb, NTOK, D), F32),
        scratch_shapes=[pltpu.VMEM((NTOK, 8), F32), pltpu.VMEM((ML_NCH, 8, ML_CHUNK), F32),
                        pltpu.VMEM((4, ML_DQK, 2 * ML_DV), F32),
                        pltpu.VMEM((2, ML_NCH, 2 * ML_DQK, 2 * ML_DV), F32),
                        pltpu.VMEM((4, ML_NCH, ML_DQK, 2 * ML_DV), BF16),
                        pltpu.VMEM((ML_NCH, 4, 8, 128), F32), pltpu.VMEM((ML_NCH, 4, 8, 128), F32)],
        compiler_params=_cparams(("parallel", "parallel")),
    )(h.reshape(nb, NTOK, D), proj3, proj3, proj3, wg, wgt, b_col, b_row)


def _ml_finish_kernel(hh_ref, o_ref, g_ref, w_ref, y_ref):
    parts = [_rms(hh_ref[:, ML_DV * i:ML_DV * (i + 1)]) for i in range(ML_HEADS)]
    hn = jnp.concatenate(parts, axis=1) * g_ref[...]
    z = (hn * jax.nn.sigmoid(o_ref[...].astype(F32))).astype(BF16)
    y_ref[...] = _dot(z, w_ref[...])


def _ml_finish(hh, proj, norm_g, w_out):
    t = hh.shape[0]
    tok = pl.BlockSpec((LIN_TM, D), lambda i: (i, 0))
    return pl.pallas_call(
        _ml_finish_kernel,
        grid=(t // LIN_TM,),
        in_specs=[tok, pl.BlockSpec((LIN_TM, D), lambda i: (i, (ML_QK + D + ML_QK) // D)),
                  pl.BlockSpec((1, D), lambda i: (0, 0)), pl.BlockSpec((D, D), lambda i: (0, 0))],
        out_specs=tok,
        out_shape=jax.ShapeDtypeStruct((t, D), F32),
        compiler_params=_cparams(("parallel",)),
    )(hh, proj, norm_g.reshape(1, D), w_out)


def _ml_project_kernel(x_ref, w_ref, o_ref, w_scr):
    n_state, n_gate = ML_QK + D, 4 * ML_HEADS

    @pl.when(pl.program_id(0) == 0)
    def _():
        w_scr[:, :n_state] = w_ref[:, :n_state].astype(BF16)
        w_scr[:, n_state:] = w_ref[:, n_state + n_gate:].astype(BF16)

    o_ref[...] = _dot(x_ref[...], w_scr[...]).astype(o_ref.dtype)


def _ml_project(h, w_in):
    t = h.shape[0]
    n = w_in.shape[1] - 4 * ML_HEADS
    return pl.pallas_call(
        _ml_project_kernel,
        grid=(t // LIN_TM,),
        in_specs=[pl.BlockSpec((LIN_TM, D), lambda i: (i, 0)), pl.BlockSpec(w_in.shape, lambda i: (0, 0))],
        out_specs=pl.BlockSpec((LIN_TM, n), lambda i: (i, 0)),
        out_shape=jax.ShapeDtypeStruct((t, n), BF16),
        scratch_shapes=[pltpu.VMEM((D, n), BF16)],
        compiler_params=_cparams(("arbitrary",)),
    )(h, w_in)


def _mlstm_mixer(h, nb, w_in, b_gate, norm_g, w_out):
    n_state = ML_QK + D
    n_gate = 4 * ML_HEADS
    pairs = ML_HEADS // 2
    wg = w_in[:, n_state:n_state + n_gate].reshape(D, 2, 2, pairs, 2).transpose(3, 0, 4, 1, 2)
    wg = wg.reshape(pairs, D, 8).astype(BF16)
    bg = b_gate.reshape(2, 2, pairs, 2).transpose(2, 3, 0, 1).reshape(pairs, 8)
    proj = _ml_project(h, w_in)
    hh = _mlstm_scan(h, proj, wg, wg.transpose(0, 2, 1), bg[:, None, :], bg[:, :, None], nb)
    return _ml_finish(hh.reshape(nb * NTOK, D), proj, norm_g, w_out.astype(BF16))


PAIRS = tuple((a, b) for a in range(4) for b in range(a + 1, 4))
N_CAT = N_GROUPS * len(PAIRS)


def _route_tables(ridx, t):
    n_tiles = t // MOE_TM + N_CAT
    n_rows = n_tiles * MOE_TM
    lo, hi = ridx[0], ridx[1]
    a, b = lo % 4, hi % 4
    cat = (lo // 4) * len(PAIRS) + (a * (7 - a)) // 2 + (b - a - 1)
    onehot = (cat[:, None] == jnp.arange(N_CAT)[None, :]).astype(jnp.int32)
    rank = jnp.take_along_axis(jnp.cumsum(onehot, axis=0) - onehot, cat[:, None], axis=1)[:, 0]
    counts = jnp.sum(onehot, axis=0)
    padded = ((counts + MOE_TM - 1) // MOE_TM) * MOE_TM
    ends = jnp.cumsum(padded)
    pos = (ends - padded)[cat] + rank
    tok_ids = jnp.arange(t, dtype=jnp.int32)
    src = jnp.zeros((n_rows,), jnp.int32).at[pos].set(tok_ids)
    n_used = ends[-1] // MOE_TM
    tile_start = jnp.minimum(jnp.arange(n_tiles), n_used - 1) * MOE_TM
    tile_cat = jnp.sum((ends[None, :] <= tile_start[:, None]).astype(jnp.int32), axis=1)
    n_valid = jnp.clip((ends - padded + counts)[tile_cat] - tile_start, 0, MOE_TM)
    pa = jnp.array([p[0] for p in PAIRS], jnp.int32)[tile_cat % len(PAIRS)]
    pb = jnp.array([p[1] for p in PAIRS], jnp.int32)[tile_cat % len(PAIRS)]
    base = (tile_cat // len(PAIRS)) * 4
    i32 = lambda a: a.astype(jnp.int32)
    return i32(base + pa), i32(base + pb), i32(n_used).reshape(1), i32(n_valid), src


def _moe_kernel(ea_ref, eb_ref, nu_ref, nv_ref, tok_ref,
                h_hbm, wga_ref, wua_ref, wda_ref, wgb_ref, wub_ref, wdb_ref, y_hbm,
                xbuf, ybuf, bin_scr, gsem, ssem):
    i = pl.program_id(0)
    n_used = nu_ref[0]

    def by_eights(row_fn):
        def body(c, carry):
            for u in range(8):
                row_fn(c * 8 + u, carry)
            return carry
        return body

    def issue_gather(tile, sl):
        def row(r, carry):
            tok = tok_ref[tile * MOE_TM + r]
            pltpu.make_async_copy(h_hbm.at[pl.ds(tok, 1)], xbuf.at[sl, pl.ds(r, 1)], gsem.at[sl]).start()
            return carry
        lax.fori_loop(0, MOE_TM // 8, by_eights(row), 0)

    def wait_gather(sl):
        pltpu.make_async_copy(h_hbm.at[pl.ds(0, MOE_TM)], xbuf.at[sl], gsem.at[sl]).wait()

    def issue_scatter(tile, sl):
        def to_hbm(r, carry):
            tok = tok_ref[tile * MOE_TM + r]
            pltpu.make_async_copy(ybuf.at[sl, pl.ds(r, 1)], y_hbm.at[pl.ds(tok, 1)], ssem.at[sl]).start()
            return carry

        def to_bin(r, carry):
            pltpu.make_async_copy(ybuf.at[sl, pl.ds(r, 1)], bin_scr.at[sl, pl.ds(r, 1)], ssem.at[sl]).start()
            return carry
        nv = nv_ref[tile]
        nv_down = lax.shift_right_logical(nv, 3)
        nv_up = lax.shift_right_logical(nv + 7, 3)
        lax.fori_loop(0, nv_down, by_eights(to_hbm), 0)
        lax.fori_loop(nv_down * 8, nv, to_hbm, 0)
        lax.fori_loop(nv, nv_up * 8, to_bin, 0)
        lax.fori_loop(nv_up, MOE_TM // 8, by_eights(to_bin), 0)

    def wait_scatter(sl):
        pltpu.make_async_copy(ybuf.at[sl], y_hbm.at[pl.ds(0, MOE_TM)], ssem.at[sl]).wait()

    @pl.when(i == 0)
    def _():
        issue_gather(0, 0)

    def tile(slot):
        wait_gather(slot)

        @pl.when(i >= 2)
        def _():
            wait_scatter(slot)

        nxt = jnp.minimum(i + 1, n_used - 1) * MOE_TM
        for r in range(MOE_TM):
            pltpu.make_async_copy(h_hbm.at[pl.ds(tok_ref[nxt + r], 1)], xbuf.at[1 - slot, pl.ds(r, 1)],
                                  gsem.at[1 - slot]).start()

        x = xbuf[slot, :, :D].astype(BF16)
        w = xbuf[slot, :, D:]

        def expert(wg_ref, wu_ref, wd_ref):
            a = _dot(x, wg_ref[0].astype(BF16))
            hid = (a * jax.nn.sigmoid(a) * _dot(x, wu_ref[0].astype(BF16))).astype(BF16)
            return _dot(hid, wd_ref[0].astype(BF16))

        ybuf[slot] = (w[:, 0:1] * expert(wga_ref, wua_ref, wda_ref)
                      + w[:, 1:2] * expert(wgb_ref, wub_ref, wdb_ref))
        issue_scatter(i, slot)

    for parity in range(2):
        pl.when(jnp.logical_and(i < n_used, i % 2 == parity))(functools.partial(tile, parity))

    @pl.when(i == pl.num_programs(0) - 1)
    def _():
        wait_scatter(0)
        wait_scatter(1)
        wait_gather(n_used % 2)


def _moe(h2, ridx, w_gate, w_up, w_down):
    t = h2.shape[0]
    ea, eb, n_used, n_valid, src = _route_tables(ridx, t)
    n_tiles = ea.shape[0]
    assert t // MOE_TM >= 2
    wspec = lambda which, shape: pl.BlockSpec(
        (1,) + shape, (lambda i, ea, eb, nu, nv, tok: (ea[i], 0, 0)) if which == 0
        else (lambda i, ea, eb, nu, nv, tok: (eb[i], 0, 0)))
    wg, wu, wd = w_gate, w_up, w_down
    return pl.pallas_call(
        _moe_kernel,
        grid_spec=pltpu.PrefetchScalarGridSpec(
            num_scalar_prefetch=5,
            grid=(n_tiles,),
            in_specs=[pl.BlockSpec(memory_space=pl.ANY),
                      wspec(0, (D, D_FF)), wspec(0, (D, D_FF)), wspec(0, (D_FF, D)),
                      wspec(1, (D, D_FF)), wspec(1, (D, D_FF)), wspec(1, (D_FF, D))],
            out_specs=pl.BlockSpec(memory_space=pl.ANY),
            scratch_shapes=[pltpu.VMEM((2, MOE_TM, MOE_ROW), F32), pltpu.VMEM((2, MOE_TM, D), F32),
                            pltpu.VMEM((2, MOE_TM, D), F32),
                            pltpu.SemaphoreType.DMA((2,)), pltpu.SemaphoreType.DMA((2,))]),
        out_shape=jax.ShapeDtypeStruct((t, D), F32),
        compiler_params=_cparams(("arbitrary",)),
    )(ea, eb, n_used, n_valid, src, h2, wg, wu, wd, wg, wu, wd)


def kernel(x, c, ctx, c_ctx, ada_w, ada_b, norm1_g, norm2_g, final_g, s5_lambda_re, s5_lambda_im, s5_log_step, s5_b_re, s5_b_im, s5_c_re, s5_c_im, s5_d, s5_w_glu, s5_b_glu, attn_w_kvq, attn_q_gain, attn_k_gain, attn_w_o, ml_w_in, ml_b_gate, ml_norm_g, ml_w_out, router_w, router_bias, moe_w_gate, moe_w_up, moe_w_down):
    nb = x.shape[0]
    depth = ada_w.shape[0]
    assert x.shape[1:] == (SEQ, D) and ctx.shape[1:] == (CTX, D) and nb + 1 <= 16
    xs = jnp.concatenate([ctx, x], axis=1).reshape(nb * NTOK, D)
    cond_in = jnp.zeros((16, D), F32).at[:nb].set(c).at[nb].set(c_ctx)
    mods = _ada(cond_in, ada_w, ada_b).reshape(depth, 16, 6, D)

    y2 = None
    for layer in range(depth):
        kind, j = layer % N_MIXERS, layer // N_MIXERS
        xs, h = _pre(xs, y2, mods[layer - 1] if layer else None, mods[layer], norm1_g[layer], nb,
                     F32 if kind == 0 else BF16, chunk_major=kind == 0)
        if kind == 0:
            y = _s5_mixer(h, nb, s5_lambda_re[j], s5_lambda_im[j], s5_log_step[j], s5_b_re[j], s5_b_im[j],
                          s5_c_re[j], s5_c_im[j], s5_d[j], s5_w_glu[j], s5_b_glu[j])
        elif kind == 1:
            y = _gqa_mixer(h, nb, attn_w_kvq[j], attn_q_gain[j], attn_k_gain[j], attn_w_o[j])
        else:
            y = _mlstm_mixer(h, nb, ml_w_in[j], ml_b_gate[j], ml_norm_g[j], ml_w_out[j])
        xs, h2, ridx = _post(xs, y, mods[layer], norm2_g[layer], router_w, router_bias, nb)
        y2 = _moe(h2, ridx, moe_w_gate[layer], moe_w_up[layer], moe_w_down[layer])
    out = _final(xs, y2, mods[depth - 1], final_g, nb)
    return out.reshape(nb, SEQ, D)
```

```python
import functools
import math

import jax
import jax.numpy as jnp
import numpy as np
from jax import lax
from jax.experimental import pallas as pl
from jax.experimental.pallas import tpu as pltpu

F32 = jnp.float32
BF16 = jnp.bfloat16

D = 1024
CTX = 256
SEQ = 2048
NTOK = CTX + SEQ
GRID_W = 64
EPS = 1e-6
N_MIXERS = 3

TOK = 256
TILES_PER_B = NTOK // TOK
LIN_TM = 512

S5_GROUP = 16
S5_GROUPS = D // S5_GROUP
S5_STATE = 64
S5_CHUNK = 16
S5_NCH = NTOK // S5_CHUNK
S5_CTX_CH = CTX // S5_CHUNK
S5_SLAB_GROUPS = 128 // S5_GROUP

HEAD_DIM = 64
N_Q_HEADS = 16
N_KV_HEADS = 4
Q_PER_KV = 4
KV_COLS = 2 * N_KV_HEADS * HEAD_DIM
ROPE_THETA = 10000.0
ATT_TQ = 128

ML_HEADS = 8
ML_DQK = 64
ML_DV = 128
ML_QK = 512
ML_CHUNK = 128
ML_NCH = NTOK // ML_CHUNK
ML_CTX_CH = CTX // ML_CHUNK

N_EXPERTS = 16
N_GROUPS = 4
D_FF = 512
MOE_TM = 256
MOE_ROW = D + 128

VMEM_LIMIT = 56 * 1024 * 1024


def _cparams(sem):
    return pltpu.CompilerParams(dimension_semantics=sem, vmem_limit_bytes=VMEM_LIMIT)


def _nt_dot(a, b, **kw):
    return lax.dot_general(a, b, (((1,), (1,)), ((), ())), preferred_element_type=F32, **kw)


def _dot(a, b):
    return jnp.dot(a, b, preferred_element_type=F32)


def _rms(x):
    return x * lax.rsqrt(jnp.mean(x * x, axis=-1, keepdims=True) + EPS)


def _ada_kernel(c_ref, w_ref, b_ref, o_ref):
    c = c_ref[...]
    cond = (c * jax.nn.sigmoid(c)).astype(BF16)
    o_ref[0] = _dot(cond, w_ref[0].astype(BF16)) + b_ref[0]


def _ada(cond_in, ada_w, ada_b):
    depth, _, n = ada_w.shape
    tn = 1536
    rows = cond_in.shape[0]
    return pl.pallas_call(
        _ada_kernel,
        grid=(depth, n // tn),
        in_specs=[pl.BlockSpec((rows, D), lambda l, j: (0, 0)),
                  pl.BlockSpec((1, D, tn), lambda l, j: (l, 0, j)),
                  pl.BlockSpec((1, 1, tn), lambda l, j: (l, 0, j))],
        out_specs=pl.BlockSpec((1, rows, tn), lambda l, j: (l, 0, j)),
        out_shape=jax.ShapeDtypeStruct((depth, rows, n), F32),
        compiler_params=_cparams(("parallel", "parallel")),
    )(cond_in, ada_w, ada_b.reshape(depth, 1, n))


def _mod_row(nb):
    return lambda i: (jnp.where(i % TILES_PER_B == 0, nb, i // TILES_PER_B), 0, 0)


def _pre_kernel(has_res, *refs):
    if has_res:
        x_ref, y_ref, mprev_ref, mcur_ref, g_ref, xo_ref, h_ref = refs
        x = x_ref[...] + mprev_ref[0, 5:6, :] * y_ref[...]
        xo_ref[...] = x
    else:
        x_ref, mcur_ref, g_ref, h_ref = refs
        x = x_ref[...]
    h = _rms(x) * g_ref[...]
    h = h * (1.0 + mcur_ref[0, 1:2, :]) + mcur_ref[0, 0:1, :]
    h_ref[...] = h.astype(h_ref.dtype).reshape(h_ref.shape)


def _chunk_major(nb):
    shape = (S5_NCH, nb, S5_CHUNK, D)
    spec = pl.BlockSpec((TOK // S5_CHUNK, 1, S5_CHUNK, D), lambda i: (i % TILES_PER_B, i // TILES_PER_B, 0, 0))
    return shape, spec


def _pre(x, y2, mods_prev, mods_cur, g, nb, h_dtype, chunk_major=False):
    t = x.shape[0]
    tok = pl.BlockSpec((TOK, D), lambda i: (i, 0))
    mod = pl.BlockSpec((1, 6, D), _mod_row(nb))
    vec = pl.BlockSpec((1, D), lambda i: (0, 0))
    h_shape, h_spec = _chunk_major(nb) if chunk_major else ((t, D), tok)
    if y2 is None:
        return x, pl.pallas_call(
            functools.partial(_pre_kernel, False),
            grid=(t // TOK,), in_specs=[tok, mod, vec], out_specs=h_spec,
            out_shape=jax.ShapeDtypeStruct(h_shape, h_dtype),
            compiler_params=_cparams(("parallel",)),
        )(x, mods_cur, g.reshape(1, D))
    return pl.pallas_call(
        functools.partial(_pre_kernel, True),
        grid=(t // TOK,), in_specs=[tok, tok, mod, mod, vec], out_specs=[tok, h_spec],
        out_shape=[jax.ShapeDtypeStruct((t, D), F32), jax.ShapeDtypeStruct(h_shape, h_dtype)],
        compiler_params=_cparams(("parallel",)),
    )(x, y2, mods_prev, mods_cur, g.reshape(1, D))


def _post_kernel(x_ref, y_ref, m_ref, g_ref, rwt_ref, rb_ref, xo_ref, h_ref, ridx_ref):
    x = x_ref[...] + m_ref[0, 2:3, :] * y_ref[...].reshape(x_ref.shape)
    xo_ref[...] = x
    h2 = _rms(x) * g_ref[...]
    h2 = h2 * (1.0 + m_ref[0, 4:5, :]) + m_ref[0, 3:4, :]
    h_ref[:, :D] = h2

    logits = _nt_dot(rwt_ref[...], h2, precision=lax.Precision.HIGHEST)
    scores = jax.nn.sigmoid(logits)
    sel = scores + rb_ref[...]
    row = lax.broadcasted_iota(jnp.int32, sel.shape, 0)
    per_group = N_EXPERTS // N_GROUPS
    best_val, best = None, None
    for g in range(N_GROUPS):
        r = [sel[per_group * g + j:per_group * g + j + 1, :] for j in range(per_group)]
        gs = None
        for i in range(per_group):
            for j in range(i + 1, per_group):
                gs = r[i] + r[j] if gs is None else jnp.maximum(gs, r[i] + r[j])
        if g == 0:
            best_val, best = gs, jnp.zeros(gs.shape, jnp.int32)
        else:
            upd = gs > best_val
            best = jnp.where(upd, g, best)
            best_val = jnp.where(upd, gs, best_val)
    masked = jnp.where((row // per_group) == best, sel, -jnp.inf)
    m1 = jnp.max(masked, axis=0, keepdims=True)
    i1 = jnp.min(jnp.where(masked == m1, row, N_EXPERTS), axis=0, keepdims=True)
    masked2 = jnp.where(row == i1, -jnp.inf, masked)
    m2 = jnp.max(masked2, axis=0, keepdims=True)
    i2 = jnp.min(jnp.where(masked2 == m2, row, N_EXPERTS), axis=0, keepdims=True)
    lo, hi = jnp.minimum(i1, i2), jnp.maximum(i1, i2)
    s_lo = jnp.sum(jnp.where(row == lo, scores, 0.0), axis=0, keepdims=True)
    s_hi = jnp.sum(jnp.where(row == hi, scores, 0.0), axis=0, keepdims=True)
    ridx_ref[...] = jnp.concatenate([lo, hi], axis=0)
    wts = jnp.concatenate([s_lo, s_hi, jnp.zeros((126, TOK), F32)], axis=0) / (s_lo + s_hi)
    h_ref[:, D:] = wts.T


def _post(x, y, mods, g, router_w, router_bias, nb):
    t = x.shape[0]
    tok = pl.BlockSpec((TOK, D), lambda i: (i, 0))
    return pl.pallas_call(
        _post_kernel,
        grid=(t // TOK,),
        in_specs=[tok, _chunk_major(nb)[1] if y.ndim == 4 else tok, pl.BlockSpec((1, 6, D), _mod_row(nb)),
                  pl.BlockSpec((1, D), lambda i: (0, 0)),
                  pl.BlockSpec((N_EXPERTS, D), lambda i: (0, 0)),
                  pl.BlockSpec((N_EXPERTS, 1), lambda i: (0, 0))],
        out_specs=[tok, pl.BlockSpec((TOK, MOE_ROW), lambda i: (i, 0)), pl.BlockSpec((2, TOK), lambda i: (0, i))],
        out_shape=[jax.ShapeDtypeStruct((t, D), F32), jax.ShapeDtypeStruct((t, MOE_ROW), F32),
                   jax.ShapeDtypeStruct((2, t), jnp.int32)],
        compiler_params=_cparams(("parallel",)),
    )(x, y, mods, g.reshape(1, D), router_w.T, router_bias.reshape(N_EXPERTS, 1))


def _final_kernel(x_ref, y_ref, m_ref, g_ref, o_ref):
    x = x_ref[...] + m_ref[0, 5:6, :] * y_ref[...]
    o_ref[...] = _rms(x) * g_ref[...]


def _final(x, y2, mods, g, nb):
    lat_tiles = SEQ // TOK
    src = lambda j: ((j // lat_tiles) * TILES_PER_B + CTX // TOK + j % lat_tiles, 0)
    tok_in = pl.BlockSpec((TOK, D), src)
    return pl.pallas_call(
        _final_kernel,
        grid=(nb * lat_tiles,),
        in_specs=[tok_in, tok_in, pl.BlockSpec((1, 6, D), lambda j: (j // lat_tiles, 0, 0)),
                  pl.BlockSpec((1, D), lambda j: (0, 0))],
        out_specs=pl.BlockSpec((TOK, D), lambda j: (j, 0)),
        out_shape=jax.ShapeDtypeStruct((nb * SEQ, D), F32),
        compiler_params=_cparams(("parallel",)),
    )(x, y2, mods, g.reshape(1, D))


def _linear_kernel(x_ref, w_ref, o_ref):
    o_ref[...] = _dot(x_ref[...].astype(BF16), w_ref[...]).astype(o_ref.dtype)


def _linear(x, w, out_dtype):
    t, k = x.shape
    n = w.shape[1]
    return pl.pallas_call(
        _linear_kernel,
        grid=(t // LIN_TM,),
        in_specs=[pl.BlockSpec((LIN_TM, k), lambda i: (i, 0)), pl.BlockSpec((k, n), lambda i: (0, 0))],
        out_specs=pl.BlockSpec((LIN_TM, n), lambda i: (i, 0)),
        out_shape=jax.ShapeDtypeStruct((t, n), out_dtype),
        compiler_params=_cparams(("parallel",)),
    )(x, w)


def _s5_operators(lam_re, lam_im, log_step, b_re, b_im, c_re, c_im):
    n = S5_CHUNK
    dt = jnp.exp(log_step)[None, :, :, None]
    k = jnp.arange(n + 1, dtype=F32)[:, None, None, None]
    mag = jnp.exp(k * lam_re[None] * dt)
    ang = k * lam_im[None] * dt
    pw_re, pw_im = mag * jnp.cos(ang), mag * jnp.sin(ang)
    a_re, a_im = pw_re[1], pw_im[1]
    den = lam_re * lam_re + lam_im * lam_im
    n_re, n_im = a_re - 1.0, a_im
    f_re = (n_re * lam_re + n_im * lam_im) / den
    f_im = (n_im * lam_re - n_re * lam_im) / den
    bt_re, bt_im = b_re.swapaxes(-1, -2), b_im.swapaxes(-1, -2)
    bb_re = f_re[:, :, None, :] * bt_re - f_im[:, :, None, :] * bt_im
    bb_im = f_re[:, :, None, :] * bt_im + f_im[:, :, None, :] * bt_re

    cp_re = c_re[None] * pw_re[:, :, :, None, :] - c_im[None] * pw_im[:, :, :, None, :]
    cp_im = c_re[None] * pw_im[:, :, :, None, :] + c_im[None] * pw_re[:, :, :, None, :]
    kk = (jnp.einsum('dgcp,kdgop->dgcko', bb_re, cp_re[:n]) - jnp.einsum('dgcp,kdgop->dgcko', bb_im, cp_im[:n]))
    wide = n * S5_GROUP
    lags = jnp.concatenate([kk[0].reshape(S5_GROUPS, S5_GROUP, wide),
                            kk[1, :, :, ::-1].reshape(S5_GROUPS, S5_GROUP, wide)], axis=1)

    def state_in(pw_sel, d):
        pr = pw_re[pw_sel, d].swapaxes(0, 1)[:, :, None, :]
        pi = pw_im[pw_sel, d].swapaxes(0, 1)[:, :, None, :]
        e_re = pr * bb_re[d][:, None] - pi * bb_im[d][:, None]
        e_im = pr * bb_im[d][:, None] + pi * bb_re[d][:, None]
        to_w = lambda e: jnp.pad(e.reshape(S5_GROUPS, wide, S5_STATE), ((0, 0), (0, 0), (0, 128 - S5_STATE)))
        return to_w(e_re), to_w(e_im)

    def state_out(pw_sel, d):
        ct_re = jnp.tile(c_re[d].swapaxes(-1, -2), (1, 1, n))
        ct_im = jnp.tile(c_im[d].swapaxes(-1, -2), (1, 1, n))
        pr = jnp.repeat(pw_re[pw_sel, d].transpose(1, 2, 0), S5_GROUP, axis=2)
        pi = jnp.repeat(pw_im[pw_sel, d].transpose(1, 2, 0), S5_GROUP, axis=2)
        to_w = lambda o: jnp.pad(o, ((0, 0), (0, 128 - S5_STATE), (0, 0)))
        return to_w(ct_re * pr - ct_im * pi), to_w(-(ct_re * pi + ct_im * pr))

    ar = jnp.arange(n)
    w1 = jnp.concatenate([*state_in(n - 1 - ar, 0), *state_in(ar, 1)], axis=2)
    w2 = jnp.concatenate([*state_out(ar + 1, 0), *state_out(n - ar, 1)], axis=1)
    pad = lambda a: jnp.pad(a, ((0, 0), (0, 128 - S5_STATE)))
    a16 = jnp.stack([pad(pw_re[n, 0]), pad(pw_im[n, 0]), pad(pw_re[n, 1]), pad(pw_im[n, 1])], axis=1)
    return lags, w1.astype(BF16), w2.astype(BF16), a16


def _s5_select():
    sel = np.zeros((S5_SLAB_GROUPS, S5_CHUNK * 128, S5_CHUNK * S5_GROUP), np.float32)
    s, c = np.meshgrid(np.arange(S5_CHUNK), np.arange(S5_GROUP), indexing="ij")
    for j in range(S5_SLAB_GROUPS):
        sel[j, s * 128 + S5_GROUP * j + c, s * S5_GROUP + c] = 1.0
    return jnp.asarray(sel, BF16)


def _s5_kernel(nb, h_ref, sel_ref, lag_ref, w1_ref, w2_ref, a_ref, y_ref, x2_scr, s_scr, x_scr, toep_scr):
    rows = S5_NCH * nb

    @pl.when(pl.program_id(1) == 0)
    def _():
        for s in range(S5_CHUNK):
            x2_scr[:, 128 * s:128 * (s + 1)] = h_ref[pl.ds(s, rows, stride=S5_CHUNK), :].astype(BF16)
        y_ref[...] = jnp.zeros(y_ref.shape, F32)

    wide = S5_CHUNK * S5_GROUP
    zeros = jnp.zeros((S5_GROUP, wide), F32)
    kf = jnp.concatenate([lag_ref[0, :S5_GROUP, :], zeros], axis=1)
    kb = jnp.concatenate([lag_ref[0, S5_GROUP:, :], zeros], axis=1)
    for s in range(S5_CHUNK):
        right, left = S5_GROUP * s, S5_GROUP * (S5_CHUNK - 1 - s)
        blk = ((pltpu.roll(kf, right, 1) if right else kf)[:, :wide]
               + (pltpu.roll(kb, 2 * wide - left, 1) if left else kb)[:, :wide])
        toep_scr[S5_GROUP * s:S5_GROUP * (s + 1), :] = blk.astype(BF16)

    u = _dot(x2_scr[...], sel_ref[0]).astype(BF16)
    s_scr[:, 0:wide] = _dot(u, toep_scr[...])
    s_scr[:, wide:] = _dot(u, w1_ref[0])

    def step(i, carry):
        cf = i
        cb = jnp.where(i < S5_CTX_CH, S5_CTX_CH - 1 - i, S5_NCH + S5_CTX_CH - 1 - i)
        rows_f = pl.ds(pl.multiple_of(cf * nb, nb), nb)
        rows_b = pl.ds(pl.multiple_of(cb * nb, nb), nb)
        fr, fi, br, bi = carry
        x_scr[rows_f, 0:128] = fr
        x_scr[rows_f, 128:256] = fi
        x_scr[rows_b, 256:384] = br
        x_scr[rows_b, 384:512] = bi
        ar, ai = a_ref[0, 0:1, :], a_ref[0, 1:2, :]
        nfr = ar * fr - ai * fi + s_scr[rows_f, 256:384]
        nfi = ar * fi + ai * fr + s_scr[rows_f, 384:512]
        ar, ai = a_ref[0, 2:3, :], a_ref[0, 3:4, :]
        nbr = ar * br - ai * bi + s_scr[rows_b, 512:640]
        nbi = ar * bi + ai * br + s_scr[rows_b, 640:768]
        return nfr, nfi, nbr, nbi

    zero = jnp.zeros((nb, 128), F32)
    lax.fori_loop(0, S5_NCH, step, (zero, zero, zero, zero))
    yg = s_scr[:, 0:256] + _dot(x_scr[...].astype(BF16), w2_ref[0])
    y_ref[0] += _nt_dot(yg.astype(BF16), sel_ref[0])


def _s5_scan(h_cm, lags, w1, w2, a16, nb):
    rows = S5_NCH * nb
    kc = S5_CHUNK * S5_GROUP
    n_slabs = D // 128
    per_group = lambda shape: pl.BlockSpec((1,) + shape, lambda q, j: (q * S5_SLAB_GROUPS + j, 0, 0))
    return pl.pallas_call(
        functools.partial(_s5_kernel, nb),
        grid=(n_slabs, S5_SLAB_GROUPS),
        in_specs=[pl.BlockSpec((nb * NTOK, 128), lambda q, j: (0, q)),
                  pl.BlockSpec((1, S5_CHUNK * 128, kc), lambda q, j: (j, 0, 0)),
                  per_group((2 * S5_GROUP, kc)), per_group((kc, 512)), per_group((512, kc)), per_group((4, 128))],
        out_specs=pl.BlockSpec((1, rows, S5_CHUNK * 128), lambda q, j: (q, 0, 0)),
        out_shape=jax.ShapeDtypeStruct((n_slabs, rows, S5_CHUNK * 128), F32),
        scratch_shapes=[pltpu.VMEM((rows, S5_CHUNK * 128), BF16), pltpu.VMEM((rows, 768), F32),
                        pltpu.VMEM((rows, 512), F32), pltpu.VMEM((kc, kc), BF16)],
        compiler_params=_cparams(("parallel", "arbitrary")),
    )(h_cm.reshape(nb * NTOK, D), _s5_select(), lags, w1, w2, a16)


def _gelu(x):
    return 0.5 * x * (1.0 + jnp.tanh(math.sqrt(2.0 / math.pi) * (x + 0.044715 * (x * x * x))))


def _glu_kernel(y_ref, h_ref, d_ref, w_ref, b_ref, o_ref, ys_scr):
    rows = TOK // S5_CHUNK
    for q in range(D // 128):
        for s in range(S5_CHUNK):
            ys_scr[q, pl.ds(s, rows, stride=S5_CHUNK), :] = y_ref[q, :, 128 * s:128 * (s + 1)]
    y = jnp.concatenate([ys_scr[q] for q in range(D // 128)], axis=1)
    z = _gelu(y + d_ref[...] * h_ref[...]).astype(BF16)
    r = _dot(z, w_ref[...]) + b_ref[...]
    o_ref[...] = r[:, :D] * jax.nn.sigmoid(r[:, D:])


def _glu(y, h, d_skip, w, b):
    t = h.shape[0]
    tok = pl.BlockSpec((TOK, D), lambda i: (i, 0))
    return pl.pallas_call(
        _glu_kernel,
        grid=(t // TOK,),
        in_specs=[pl.BlockSpec((D // 128, TOK // S5_CHUNK, S5_CHUNK * 128), lambda i: (0, i, 0)), tok,
                  pl.BlockSpec((1, D), lambda i: (0, 0)),
                  pl.BlockSpec((D, 2 * D), lambda i: (0, 0)), pl.BlockSpec((1, 2 * D), lambda i: (0, 0))],
        out_specs=tok,
        out_shape=jax.ShapeDtypeStruct((t, D), F32),
        scratch_shapes=[pltpu.VMEM((D // 128, TOK, 128), F32)],
        compiler_params=_cparams(("parallel",)),
    )(y, h, d_skip.reshape(1, D), w, b.reshape(1, 2 * D))


def _s5_mixer(h_cm, nb, lam_re, lam_im, log_step, b_re, b_im, c_re, c_im, d_skip, w_glu, b_glu):
    lags, w1, w2, a16 = _s5_operators(lam_re, lam_im, log_step, b_re, b_im, c_re, c_im)
    y = _s5_scan(h_cm, lags, w1, w2, a16, nb)
    y = _glu(y, h_cm.reshape(nb * NTOK, D), d_skip, w_glu.astype(BF16), b_glu)
    return y.reshape(h_cm.shape)


def _rope_tables():
    rows = SEQ // GRID_W
    row = jnp.repeat(jnp.arange(rows), GRID_W).astype(F32)
    col = jnp.tile(jnp.arange(GRID_W), rows).astype(F32)
    half = HEAD_DIM // 4
    inv_freq = ROPE_THETA ** (-jnp.arange(half, dtype=F32) / half)
    ang_r, ang_c = row[:, None] * inv_freq, col[:, None] * inv_freq
    cos = jnp.concatenate([jnp.cos(ang_r)] * 2 + [jnp.cos(ang_c)] * 2, axis=1)
    sin = jnp.concatenate([-jnp.sin(ang_r), jnp.sin(ang_r), -jnp.sin(ang_c), jnp.sin(ang_c)], axis=1)
    cos = jnp.concatenate([jnp.ones((CTX, HEAD_DIM), F32), cos], axis=0)
    sin = jnp.concatenate([jnp.zeros((CTX, HEAD_DIM), F32), sin], axis=0)
    return jnp.tile(cos, (1, 2)), jnp.tile(sin, (1, 2))


def _qk_prep_kernel(p_ref, cos_ref, sin_ref, qg_ref, kg_ref, q_ref, k_ref, v_ref):
    lane = lax.broadcasted_iota(jnp.int32, (TOK, 128), 1)
    low_head = lane < HEAD_DIM
    first_half = (lane % 32) < 16
    cos, sin = cos_ref[...], sin_ref[...]

    def norm_rope(x, gain):
        sq = x * x
        s_lo = jnp.sum(jnp.where(low_head, sq, 0.0), axis=1, keepdims=True)
        s_hi = jnp.sum(sq, axis=1, keepdims=True) - s_lo
        inv = lax.rsqrt(jnp.where(low_head, s_lo, s_hi) * (1.0 / HEAD_DIM) + EPS)
        xn = x * inv * gain
        partner = jnp.where(first_half, pltpu.roll(xn, 128 - 16, 1), pltpu.roll(xn, 16, 1))
        return xn * cos + partner * sin

    ones_col = (lax.broadcasted_iota(jnp.int32, (TOK, HEAD_DIM), 1) == 0).astype(BF16)
    for j in range(KV_COLS // 2 // 128):
        kt = norm_rope(p_ref[:, 128 * j:128 * (j + 1)], kg_ref[...]).astype(BF16)
        vt = p_ref[:, KV_COLS // 2 + 128 * j:KV_COLS // 2 + 128 * (j + 1)].astype(BF16)
        for hh in range(2):
            k_ref[0, 2 * j + hh] = kt[:, HEAD_DIM * hh:HEAD_DIM * (hh + 1)]
            v_ref[0, 2 * j + hh] = jnp.concatenate([vt[:, HEAD_DIM * hh:HEAD_DIM * (hh + 1)], ones_col], axis=1)
    for j in range(D // 128):
        qt = norm_rope(p_ref[:, KV_COLS + 128 * j:KV_COLS + 128 * (j + 1)], qg_ref[...])
        q_ref[:, 128 * j:128 * (j + 1)] = (qt * (HEAD_DIM ** -0.5 * math.log2(math.e))).astype(BF16)


def _qk_prep(proj, q_gain, k_gain, nb):
    t, ncol = proj.shape
    cos, sin = _rope_tables()
    tab = pl.BlockSpec((TOK, 128), lambda i: (i % TILES_PER_B, 0))
    gain = pl.BlockSpec((1, 128), lambda i: (0, 0))
    head_major = lambda width: pl.BlockSpec((1, N_KV_HEADS, TOK, width),
                                            lambda i: (i // TILES_PER_B, 0, i % TILES_PER_B, 0))
    hm_shape = lambda width: jax.ShapeDtypeStruct((nb, N_KV_HEADS, NTOK, width), BF16)
    return pl.pallas_call(
        _qk_prep_kernel,
        grid=(t // TOK,),
        in_specs=[pl.BlockSpec((TOK, ncol), lambda i: (i, 0)), tab, tab, gain, gain],
        out_specs=[pl.BlockSpec((TOK, D), lambda i: (i, 0)), head_major(HEAD_DIM), head_major(2 * HEAD_DIM)],
        out_shape=[jax.ShapeDtypeStruct((t, D), BF16), hm_shape(HEAD_DIM), hm_shape(2 * HEAD_DIM)],
        compiler_params=_cparams(("parallel",)),
    )(proj, cos, sin, jnp.tile(q_gain, 2).reshape(1, 128), jnp.tile(k_gain, 2).reshape(1, 128))


def _attn_kernel(q_ref, k_ref, v_ref, o_ref):
    def attend(n_keys):
        outs = []
        for kh in range(N_KV_HEADS):
            qs = jnp.concatenate(
                [q_ref[0, :, HEAD_DIM * (Q_PER_KV * kh + g):HEAD_DIM * (Q_PER_KV * kh + g + 1)]
                 for g in range(Q_PER_KV)], axis=0)
            s = _nt_dot(qs, k_ref[0, kh, :n_keys, :])
            p = jnp.exp2(s - jnp.max(s, axis=1, keepdims=True))
            ov = _dot(p.astype(BF16), v_ref[0, kh, :n_keys, :])
            o = ov[:, :HEAD_DIM] / ov[:, HEAD_DIM:HEAD_DIM + 1]
            outs += [o[ATT_TQ * g:ATT_TQ * (g + 1)] for g in range(Q_PER_KV)]
        o_ref[0] = jnp.concatenate(outs, axis=1).astype(BF16)

    is_ctx = pl.program_id(1) < CTX // ATT_TQ

    @pl.when(is_ctx)
    def _():
        attend(CTX)

    @pl.when(jnp.logical_not(is_ctx))
    def _():
        attend(NTOK)


def _attention(q, k, v, nb):
    kv = lambda width: pl.BlockSpec((1, N_KV_HEADS, NTOK, width), lambda b, i: (b, 0, 0, 0))
    qo = pl.BlockSpec((1, ATT_TQ, D), lambda b, i: (b, i, 0))
    return pl.pallas_call(
        _attn_kernel,
        grid=(nb, NTOK // ATT_TQ),
        in_specs=[qo, kv(HEAD_DIM), kv(2 * HEAD_DIM)], out_specs=qo,
        out_shape=jax.ShapeDtypeStruct((nb, NTOK, D), BF16),
        compiler_params=_cparams(("parallel", "parallel")),
    )(q.reshape(nb, NTOK, D), k, v)


def _gqa_mixer(h, nb, w_kvq, q_gain, k_gain, w_o):
    proj = _linear(h, w_kvq.astype(BF16), F32)
    q, k, v = _qk_prep(proj, q_gain, k_gain, nb)
    o = _attention(q, k, v, nb)
    return _linear(o.reshape(nb * NTOK, D), w_o.astype(BF16), F32)


def _split3(x):
    hi = x.astype(BF16).astype(F32)
    r = x - hi
    mid = r.astype(BF16).astype(F32)
    lo = (r - mid).astype(BF16).astype(F32)
    return hi, mid, lo


def _log_sigmoid(x):
    return jnp.minimum(x, 0.0) - jnp.log1p(jnp.exp(-jnp.abs(x)))


def _mlstm_kernel(h_ref, k_ref, v_ref, q_ref, wg_ref, wgt_ref, bc_ref, br_ref, o_ref,
                  col_scr, row_scr, st_scr, cl_scr, c0_scr, mloc_scr, m0_scr):
    hb = h_ref[0]
    g_col = _dot(hb, wg_ref[0]) + bc_ref[0]
    g_row = _nt_dot(wgt_ref[0], hb) + br_ref[0]
    ii = lax.broadcasted_iota(jnp.int32, (ML_CHUNK, ML_CHUNK), 0)
    jj = lax.broadcasted_iota(jnp.int32, (ML_CHUNK, ML_CHUNK), 1)
    low = ii >= jj
    upp = ii <= jj
    tri_l, tri_u = low.astype(BF16), upp.astype(BF16)
    tri_lu_rows = jnp.concatenate([tri_l, tri_u], axis=0)
    tri_ul_cols = jnp.concatenate([tri_u, tri_l], axis=1)
    lane_c = lax.broadcasted_iota(jnp.int32, (ML_CHUNK, 8), 1)
    sub_r = lax.broadcasted_iota(jnp.int32, (8, ML_CHUNK), 0)
    for c in range(ML_NCH):
        sl = slice(ML_CHUNK * c, ML_CHUNK * (c + 1))
        gc = g_col[sl]
        parts = _split3(jnp.where(lane_c % 2 == 1, _log_sigmoid(gc), 0.0))
        cs = _dot(tri_lu_rows, jnp.concatenate(parts, axis=1).astype(BF16))
        cs = cs[:, 0:8] + cs[:, 8:16] + cs[:, 16:24]
        cum = jnp.where((lane_c // 2) % 2 == 1, cs[ML_CHUNK:], cs[:ML_CHUNK])
        col_scr[sl, :] = jnp.where(lane_c % 2 == 1, cum, gc)
        gr = g_row[:, sl]
        parts = _split3(jnp.where(sub_r % 2 == 1, _log_sigmoid(gr), 0.0))
        rs = _dot(jnp.concatenate(parts, axis=0).astype(BF16), tri_ul_cols)
        rs = rs[0:8] + rs[8:16] + rs[16:24]
        cum = jnp.where((sub_r // 2) % 2 == 1, rs[:, ML_CHUNK:], rs[:, :ML_CHUNK])
        row_scr[c] = jnp.where(sub_r % 2 == 1, cum, gr)

    ones_col = (lax.broadcasted_iota(jnp.int32, (ML_CHUNK, ML_DV), 1) == 0).astype(BF16)

    def chunk_rows(c):
        return pl.ds(pl.multiple_of(c * ML_CHUNK, ML_CHUNK), ML_CHUNK)

    def gates(c, idx):
        rows, rr = chunk_rows(c), row_scr[c]
        ig_c, cum_c = col_scr[rows, 2 * idx:2 * idx + 1], col_scr[rows, 2 * idx + 1:2 * idx + 2]
        ig_r, cum_r = rr[2 * idx:2 * idx + 1, :], rr[2 * idx + 1:2 * idx + 2, :]
        g = cum_r[:, ML_CHUNK - 1:ML_CHUNK] if idx % 2 == 0 else cum_r[:, 0:1]
        return ig_c, cum_c, ig_r, cum_r, g

    def keys_values(c, hl):
        rows = chunk_rows(c)
        ks = k_ref[0, rows, ML_DQK * hl:ML_DQK * (hl + 1)] * (ML_DQK ** -0.5)
        v_aug = jnp.concatenate([v_ref[0, rows, ML_DV * hl:ML_DV * (hl + 1)], ones_col], axis=1)
        return ks, v_aug

    scalar_tile = lambda m: jnp.broadcast_to(m, (8, 128))

    def local_state(c, carry):
        for hl in range(2):
            ks, v_aug = keys_values(c, hl)
            ke = []
            for d in range(2):
                ig_c, cum_c, _, _, g = gates(c, 2 * hl + d)
                w = g - cum_c + ig_c
                m_loc = jnp.max(w, axis=0, keepdims=True)
                mloc_scr[c, 2 * hl + d] = scalar_tile(m_loc)
                ke.append(ks.astype(F32) * jnp.exp(w - m_loc))
            ke = jnp.concatenate(ke, axis=1).astype(BF16)
            cl_scr[hl, c] = lax.dot_general(ke, v_aug, (((0,), (0,)), ((), ())), preferred_element_type=F32)
        return carry

    lax.fori_loop(0, ML_NCH, local_state, 0)

    st_scr[...] = jnp.zeros(st_scr.shape, F32)

    def recur(i, ms):
        new_ms = []
        for hl in range(2):
            for d in range(2):
                idx = 2 * hl + d
                if d == 0:
                    c = i
                else:
                    c = jnp.where(i < ML_CTX_CH, ML_CTX_CH - 1 - i, ML_NCH + ML_CTX_CH - 1 - i)
                g = gates(c, idx)[4]
                m0, c_aug = ms[idx], st_scr[idx]
                c0_scr[idx, c] = c_aug.astype(BF16)
                m0_scr[c, idx] = scalar_tile(m0)
                m_loc = mloc_scr[c, idx, 0:1, 0:1]
                m_new = jnp.maximum(g + m0, m_loc)
                st_scr[idx] = (jnp.exp(g + m0 - m_new) * c_aug
                               + jnp.exp(m_loc - m_new) * cl_scr[hl, c, ML_DQK * d:ML_DQK * (d + 1), :])
                new_ms.append(m_new)
        return tuple(new_ms)

    lax.fori_loop(0, ML_NCH, recur, tuple(jnp.zeros((1, 1), F32) for _ in range(4)))

    def outputs(c, carry):
        rows = chunk_rows(c)
        for hl in range(2):
            ks, v_aug = keys_values(c, hl)
            q = q_ref[0, rows, ML_DQK * hl:ML_DQK * (hl + 1)]
            raw = _nt_dot(q, ks)
            qc = _dot(q, jnp.concatenate([c0_scr[2 * hl, c], c0_scr[2 * hl + 1, c]], axis=1))
            s, a_inter, m_row = [], [], []
            for d in range(2):
                idx = 2 * hl + d
                ig_c, cum_c, ig_r, cum_r, _ = gates(c, idx)
                log_d = jnp.where(low if d == 0 else upp, cum_c - cum_r + ig_r, -jnp.inf)
                log_inter = cum_c + m0_scr[c, idx, 0:1, 0:1]
                m_row.append(jnp.maximum(log_inter, jnp.max(log_d, axis=1, keepdims=True)))
                s.append(raw * jnp.exp(log_d - m_row[d]))
                a_inter.append(jnp.exp(log_inter - m_row[d]))
            sv = _dot(jnp.concatenate(s, axis=0).astype(BF16), v_aug)
            h = None
            for d in range(2):
                nd = a_inter[d] * qc[:, 2 * ML_DV * d:2 * ML_DV * (d + 1)] + sv[ML_CHUNK * d:ML_CHUNK * (d + 1)]
                den = jnp.maximum(jnp.abs(nd[:, ML_DV:ML_DV + 1]), jnp.exp(-m_row[d]))
                h = nd[:, :ML_DV] / den if h is None else h + nd[:, :ML_DV] / den
            o_ref[0, rows, ML_DV * hl:ML_DV * (hl + 1)] = h
        return carry

    lax.fori_loop(0, ML_NCH, outputs, 0)


def _mlstm_scan(h, proj, wg, wgt, b_col, b_row, nb):
    pairs = ML_HEADS // 2
    k_blk = pl.BlockSpec((1, NTOK, 2 * ML_DQK), lambda b, p: (b, 0, p))
    v_blk = pl.BlockSpec((1, NTOK, 2 * ML_DV), lambda b, p: (b, 0, ML_QK // (2 * ML_DV) + p))
    q_blk = pl.BlockSpec((1, NTOK, 2 * ML_DQK), lambda b, p: (b, 0, (ML_QK + D) // (2 * ML_DQK) + p))
    per_pair = lambda shape: pl.BlockSpec((1,) + shape, lambda b, p: (p, 0, 0))
    proj3 = proj.reshape(nb, NTOK, proj.shape[-1])
    return pl.pallas_call(
        _mlstm_kernel,
        grid=(nb, pairs),
        in_specs=[pl.BlockSpec((1, NTOK, D), lambda b, p: (b, 0, 0)), k_blk, v_blk, q_blk,
                  per_pair((D, 8)), per_pair((8, D)), per_pair((1, 8)), per_pair((8, 1))],
        out_specs=pl.BlockSpec((1, NTOK, 2 * ML_DV), lambda b, p: (b, 0, p)),
        out_shape=jax.ShapeDtypeStruct((nb, NTOK, D), F32),
        scratch_shapes=[pltpu.VMEM((NTOK, 8), F32), pltpu.VMEM((ML_NCH, 8, ML_CHUNK), F32),
                        pltpu.VMEM((4, ML_DQK, 2 * ML_DV), F32),
                        pltpu.VMEM((2, ML_NCH, 2 * ML_DQK, 2 * ML_DV), F32),
                        pltpu.VMEM((4, ML_NCH, ML_DQK, 2 * ML_DV), BF16),
                        pltpu.VMEM((ML_NCH, 4, 8, 128), F32), pltpu.VMEM((ML_NCH, 4, 8, 128), F32)],
        compiler_params=_cparams(("parallel", "parallel")),
    )(h.reshape(nb, NTOK, D), proj3, proj3, proj3, wg, wgt, b_col, b_row)


def _ml_finish_kernel(hh_ref, o_ref, g_ref, w_ref, y_ref):
    parts = [_rms(hh_ref[:, ML_DV * i:ML_DV * (i + 1)]) for i in range(ML_HEADS)]
    hn = jnp.concatenate(parts, axis=1) * g_ref[...]
    z = (hn * jax.nn.sigmoid(o_ref[...].astype(F32))).astype(BF16)
    y_ref[...] = _dot(z, w_ref[...])


def _ml_finish(hh, proj, norm_g, w_out):
    t = hh.shape[0]
    tok = pl.BlockSpec((LIN_TM, D), lambda i: (i, 0))
    return pl.pallas_call(
        _ml_finish_kernel,
        grid=(t // LIN_TM,),
        in_specs=[tok, pl.BlockSpec((LIN_TM, D), lambda i: (i, (ML_QK + D + ML_QK) // D)),
                  pl.BlockSpec((1, D), lambda i: (0, 0)), pl.BlockSpec((D, D), lambda i: (0, 0))],
        out_specs=tok,
        out_shape=jax.ShapeDtypeStruct((t, D), F32),
        compiler_params=_cparams(("parallel",)),
    )(hh, proj, norm_g.reshape(1, D), w_out)


def _ml_project_kernel(x_ref, w_ref, o_ref, w_scr):
    n_state, n_gate = ML_QK + D, 4 * ML_HEADS

    @pl.when(pl.program_id(0) == 0)
    def _():
        w_scr[:, :n_state] = w_ref[:, :n_state].astype(BF16)
        w_scr[:, n_state:] = w_ref[:, n_state + n_gate:].astype(BF16)

    o_ref[...] = _dot(x_ref[...], w_scr[...]).astype(o_ref.dtype)


def _ml_project(h, w_in):
    t = h.shape[0]
    n = w_in.shape[1] - 4 * ML_HEADS
    return pl.pallas_call(
        _ml_project_kernel,
        grid=(t // LIN_TM,),
        in_specs=[pl.BlockSpec((LIN_TM, D), lambda i: (i, 0)), pl.BlockSpec(w_in.shape, lambda i: (0, 0))],
        out_specs=pl.BlockSpec((LIN_TM, n), lambda i: (i, 0)),
        out_shape=jax.ShapeDtypeStruct((t, n), BF16),
        scratch_shapes=[pltpu.VMEM((D, n), BF16)],
        compiler_params=_cparams(("arbitrary",)),
    )(h, w_in)


def _mlstm_mixer(h, nb, w_in, b_gate, norm_g, w_out):
    n_state = ML_QK + D
    n_gate = 4 * ML_HEADS
    pairs = ML_HEADS // 2
    wg = w_in[:, n_state:n_state + n_gate].reshape(D, 2, 2, pairs, 2).transpose(3, 0, 4, 1, 2)
    wg = wg.reshape(pairs, D, 8).astype(BF16)
    bg = b_gate.reshape(2, 2, pairs, 2).transpose(2, 3, 0, 1).reshape(pairs, 8)
    proj = _ml_project(h, w_in)
    hh = _mlstm_scan(h, proj, wg, wg.transpose(0, 2, 1), bg[:, None, :], bg[:, :, None], nb)
    return _ml_finish(hh.reshape(nb * NTOK, D), proj, norm_g, w_out.astype(BF16))


PAIRS = tuple((a, b) for a in range(4) for b in range(a + 1, 4))
N_CAT = N_GROUPS * len(PAIRS)


def _route_tables(ridx, t):
    n_tiles = t // MOE_TM + N_CAT
    n_rows = n_tiles * MOE_TM
    lo, hi = ridx[0], ridx[1]
    a, b = lo % 4, hi % 4
    cat = (lo // 4) * len(PAIRS) + (a * (7 - a)) // 2 + (b - a - 1)
    onehot = (cat[:, None] == jnp.arange(N_CAT)[None, :]).astype(jnp.int32)
    rank = jnp.take_along_axis(jnp.cumsum(onehot, axis=0) - onehot, cat[:, None], axis=1)[:, 0]
    counts = jnp.sum(onehot, axis=0)
    padded = ((counts + MOE_TM - 1) // MOE_TM) * MOE_TM
    ends = jnp.cumsum(padded)
    pos = (ends - padded)[cat] + rank
    tok_ids = jnp.arange(t, dtype=jnp.int32)
    src = jnp.zeros((n_rows,), jnp.int32).at[pos].set(tok_ids)
    n_used = ends[-1] // MOE_TM
    tile_start = jnp.minimum(jnp.arange(n_tiles), n_used - 1) * MOE_TM
    tile_cat = jnp.sum((ends[None, :] <= tile_start[:, None]).astype(jnp.int32), axis=1)
    n_valid = jnp.clip((ends - padded + counts)[tile_cat] - tile_start, 0, MOE_TM)
    pa = jnp.array([p[0] for p in PAIRS], jnp.int32)[tile_cat % len(PAIRS)]
    pb = jnp.array([p[1] for p in PAIRS], jnp.int32)[tile_cat % len(PAIRS)]
    base = (tile_cat // len(PAIRS)) * 4
    i32 = lambda a: a.astype(jnp.int32)
    return i32(base + pa), i32(base + pb), i32(n_used).reshape(1), i32(n_valid), src


def _moe_kernel(ea_ref, eb_ref, nu_ref, nv_ref, tok_ref,
                h_hbm, wga_ref, wua_ref, wda_ref, wgb_ref, wub_ref, wdb_ref, y_hbm,
                xbuf, ybuf, bin_scr, gsem, ssem):
    i = pl.program_id(0)
    n_used = nu_ref[0]

    def by_eights(row_fn):
        def body(c, carry):
            for u in range(8):
                row_fn(c * 8 + u, carry)
            return carry
        return body

    def issue_gather(tile, sl):
        def row(r, carry):
            tok = tok_ref[tile * MOE_TM + r]
            pltpu.make_async_copy(h_hbm.at[pl.ds(tok, 1)], xbuf.at[sl, pl.ds(r, 1)], gsem.at[sl]).start()
            return carry
        lax.fori_loop(0, MOE_TM // 8, by_eights(row), 0)

    def wait_gather(sl):
        pltpu.make_async_copy(h_hbm.at[pl.ds(0, MOE_TM)], xbuf.at[sl], gsem.at[sl]).wait()

    def issue_scatter(tile, sl):
        def to_hbm(r, carry):
            tok = tok_ref[tile * MOE_TM + r]
            pltpu.make_async_copy(ybuf.at[sl, pl.ds(r, 1)], y_hbm.at[pl.ds(tok, 1)], ssem.at[sl]).start()
            return carry

        def to_bin(r, carry):
            pltpu.make_async_copy(ybuf.at[sl, pl.ds(r, 1)], bin_scr.at[sl, pl.ds(r, 1)], ssem.at[sl]).start()
            return carry
        nv = nv_ref[tile]
        nv_down = lax.shift_right_logical(nv, 3)
        nv_up = lax.shift_right_logical(nv + 7, 3)
        lax.fori_loop(0, nv_down, by_eights(to_hbm), 0)
        lax.fori_loop(nv_down * 8, nv, to_hbm, 0)
        lax.fori_loop(nv, nv_up * 8, to_bin, 0)
        lax.fori_loop(nv_up, MOE_TM // 8, by_eights(to_bin), 0)

    def wait_scatter(sl):
        pltpu.make_async_copy(ybuf.at[sl], y_hbm.at[pl.ds(0, MOE_TM)], ssem.at[sl]).wait()

    @pl.when(i == 0)
    def _():
        issue_gather(0, 0)

    def tile(slot):
        wait_gather(slot)

        @pl.when(i >= 2)
        def _():
            wait_scatter(slot)

        nxt = jnp.minimum(i + 1, n_used - 1) * MOE_TM
        for r in range(MOE_TM):
            pltpu.make_async_copy(h_hbm.at[pl.ds(tok_ref[nxt + r], 1)], xbuf.at[1 - slot, pl.ds(r, 1)],
                                  gsem.at[1 - slot]).start()

        x = xbuf[slot, :, :D].astype(BF16)
        w = xbuf[slot, :, D:]

        def expert(wg_ref, wu_ref, wd_ref):
            a = _dot(x, wg_ref[0, 0].astype(BF16))
            hid = (a * jax.nn.sigmoid(a) * _dot(x, wu_ref[0, 0].astype(BF16))).astype(BF16)
            return _dot(hid, wd_ref[0, 0].astype(BF16))

        ybuf[slot] = (w[:, 0:1] * expert(wga_ref, wua_ref, wda_ref)
                      + w[:, 1:2] * expert(wgb_ref, wub_ref, wdb_ref))
        issue_scatter(i, slot)

    for parity in range(2):
        pl.when(jnp.logical_and(i < n_used, i % 2 == parity))(functools.partial(tile, parity))

    @pl.when(i == pl.num_programs(0) - 1)
    def _():
        wait_scatter(0)
        wait_scatter(1)
        wait_gather(n_used % 2)


def _moe(h2, ridx, w_gate, w_up, w_down, layer):
    t = h2.shape[0]
    ea, eb, n_used, n_valid, src = _route_tables(ridx, t)
    n_tiles = ea.shape[0]
    assert t // MOE_TM >= 2
    wspec = lambda which, shape: pl.BlockSpec(
        (1, 1) + shape, (lambda i, ea, eb, nu, nv, tok: (layer, ea[i], 0, 0)) if which == 0
        else (lambda i, ea, eb, nu, nv, tok: (layer, eb[i], 0, 0)))
    wg, wu, wd = w_gate, w_up, w_down
    return pl.pallas_call(
        _moe_kernel,
        grid_spec=pltpu.PrefetchScalarGridSpec(
            num_scalar_prefetch=5,
            grid=(n_tiles,),
            in_specs=[pl.BlockSpec(memory_space=pl.ANY),
                      wspec(0, (D, D_FF)), wspec(0, (D, D_FF)), wspec(0, (D_FF, D)),
                      wspec(1, (D, D_FF)), wspec(1, (D, D_FF)), wspec(1, (D_FF, D))],
            out_specs=pl.BlockSpec(memory_space=pl.ANY),
            scratch_shapes=[pltpu.VMEM((2, MOE_TM, MOE_ROW), F32), pltpu.VMEM((2, MOE_TM, D), F32),
                            pltpu.VMEM((2, MOE_TM, D), F32),
                            pltpu.SemaphoreType.DMA((2,)), pltpu.SemaphoreType.DMA((2,))]),
        out_shape=jax.ShapeDtypeStruct((t, D), F32),
        compiler_params=_cparams(("arbitrary",)),
    )(ea, eb, n_used, n_valid, src, h2, wg, wu, wd, wg, wu, wd)


def kernel(x, c, ctx, c_ctx, ada_w, ada_b, norm1_g, norm2_g, final_g, s5_lambda_re, s5_lambda_im, s5_log_step, s5_b_re, s5_b_im, s5_c_re, s5_c_im, s5_d, s5_w_glu, s5_b_glu, attn_w_kvq, attn_q_gain, attn_k_gain, attn_w_o, ml_w_in, ml_b_gate, ml_norm_g, ml_w_out, router_w, router_bias, moe_w_gate, moe_w_up, moe_w_down):
    nb = x.shape[0]
    depth = ada_w.shape[0]
    assert x.shape[1:] == (SEQ, D) and ctx.shape[1:] == (CTX, D) and nb + 1 <= 16
    xs = jnp.concatenate([ctx, x], axis=1).reshape(nb * NTOK, D)
    cond_in = jnp.zeros((16, D), F32).at[:nb].set(c).at[nb].set(c_ctx)
    mods = _ada(cond_in, ada_w, ada_b).reshape(depth, 16, 6, D)

    y2 = None
    for layer in range(depth):
        kind, j = layer % N_MIXERS, layer // N_MIXERS
        xs, h = _pre(xs, y2, mods[layer - 1] if layer else None, mods[layer], norm1_g[layer], nb,
                     F32 if kind == 0 else BF16, chunk_major=kind == 0)
        if kind == 0:
            y = _s5_mixer(h, nb, s5_lambda_re[j], s5_lambda_im[j], s5_log_step[j], s5_b_re[j], s5_b_im[j],
                          s5_c_re[j], s5_c_im[j], s5_d[j], s5_w_glu[j], s5_b_glu[j])
        elif kind == 1:
            y = _gqa_mixer(h, nb, attn_w_kvq[j], attn_q_gain[j], attn_k_gain[j], attn_w_o[j])
        else:
            y = _mlstm_mixer(h, nb, ml_w_in[j], ml_b_gate[j], ml_norm_g[j], ml_w_out[j])
        xs, h2, ridx = _post(xs, y, mods[layer], norm2_g[layer], router_w, router_bias, nb)
        y2 = _moe(h2, ridx, moe_w_gate, moe_w_up, moe_w_down, layer)
    out = _final(xs, y2, mods[depth - 1], final_g, nb)
    return out.reshape(nb, SEQ, D)
```

```python
import functools
import math

import jax
import jax.numpy as jnp
import numpy as np
from jax import lax
from jax.experimental import pallas as pl
from jax.experimental.pallas import tpu as pltpu

F32 = jnp.float32
BF16 = jnp.bfloat16

D = 1024
CTX = 256
SEQ = 2048
NTOK = CTX + SEQ
GRID_W = 64
EPS = 1e-6
N_MIXERS = 3

TOK = 256
TILES_PER_B = NTOK // TOK
LIN_TM = 512

S5_GROUP = 16
S5_GROUPS = D // S5_GROUP
S5_STATE = 64
S5_CHUNK = 16
S5_NCH = NTOK // S5_CHUNK
S5_CTX_CH = CTX // S5_CHUNK
S5_SLAB_GROUPS = 128 // S5_GROUP

HEAD_DIM = 64
N_Q_HEADS = 16
N_KV_HEADS = 4
Q_PER_KV = 4
KV_COLS = 2 * N_KV_HEADS * HEAD_DIM
ROPE_THETA = 10000.0
ATT_TQ = 128

ML_HEADS = 8
ML_DQK = 64
ML_DV = 128
ML_QK = 512
ML_CHUNK = 128
ML_NCH = NTOK // ML_CHUNK
ML_CTX_CH = CTX // ML_CHUNK

N_EXPERTS = 16
N_GROUPS = 4
D_FF = 512
MOE_TM = 256
MOE_ROW = D + 128

VMEM_LIMIT = 56 * 1024 * 1024


def _cparams(sem):
    return pltpu.CompilerParams(dimension_semantics=sem, vmem_limit_bytes=VMEM_LIMIT)


def _nt_dot(a, b, **kw):
    return lax.dot_general(a, b, (((1,), (1,)), ((), ())), preferred_element_type=F32, **kw)


def _dot(a, b):
    return jnp.dot(a, b, preferred_element_type=F32)


def _rms(x):
    return x * lax.rsqrt(jnp.mean(x * x, axis=-1, keepdims=True) + EPS)


def _ada_kernel(c_ref, w_ref, b_ref, o_ref):
    c = c_ref[...]
    cond = (c * jax.nn.sigmoid(c)).astype(BF16)
    o_ref[0] = _dot(cond, w_ref[0].astype(BF16)) + b_ref[0]


def _ada(cond_in, ada_w, ada_b):
    depth, _, n = ada_w.shape
    tn = 1536
    rows = cond_in.shape[0]
    return pl.pallas_call(
        _ada_kernel,
        grid=(depth, n // tn),
        in_specs=[pl.BlockSpec((rows, D), lambda l, j: (0, 0)),
                  pl.BlockSpec((1, D, tn), lambda l, j: (l, 0, j)),
                  pl.BlockSpec((1, 1, tn), lambda l, j: (l, 0, j))],
        out_specs=pl.BlockSpec((1, rows, tn), lambda l, j: (l, 0, j)),
        out_shape=jax.ShapeDtypeStruct((depth, rows, n), F32),
        compiler_params=_cparams(("parallel", "parallel")),
    )(cond_in, ada_w, ada_b.reshape(depth, 1, n))


def _mod_row(nb):
    return lambda i: (jnp.where(i % TILES_PER_B == 0, nb, i // TILES_PER_B), 0, 0)


def _pre_kernel(has_res, *refs):
    if has_res:
        x_ref, y_ref, mprev_ref, mcur_ref, g_ref, xo_ref, h_ref = refs
        x = x_ref[...] + mprev_ref[0, 5:6, :] * y_ref[...]
        xo_ref[...] = x
    else:
        x_ref, mcur_ref, g_ref, h_ref = refs
        x = x_ref[...]
    h = _rms(x) * g_ref[...]
    h = h * (1.0 + mcur_ref[0, 1:2, :]) + mcur_ref[0, 0:1, :]
    h_ref[...] = h.astype(h_ref.dtype).reshape(h_ref.shape)


def _chunk_major(nb):
    shape = (S5_NCH, nb, S5_CHUNK, D)
    spec = pl.BlockSpec((TOK // S5_CHUNK, 1, S5_CHUNK, D), lambda i: (i % TILES_PER_B, i // TILES_PER_B, 0, 0))
    return shape, spec


def _pre(x, y2, mods_prev, mods_cur, g, nb, h_dtype, chunk_major=False):
    t = x.shape[0]
    tok = pl.BlockSpec((TOK, D), lambda i: (i, 0))
    mod = pl.BlockSpec((1, 6, D), _mod_row(nb))
    vec = pl.BlockSpec((1, D), lambda i: (0, 0))
    h_shape, h_spec = _chunk_major(nb) if chunk_major else ((t, D), tok)
    if y2 is None:
        return x, pl.pallas_call(
            functools.partial(_pre_kernel, False),
            grid=(t // TOK,), in_specs=[tok, mod, vec], out_specs=h_spec,
            out_shape=jax.ShapeDtypeStruct(h_shape, h_dtype),
            compiler_params=_cparams(("parallel",)),
        )(x, mods_cur, g.reshape(1, D))
    return pl.pallas_call(
        functools.partial(_pre_kernel, True),
        grid=(t // TOK,), in_specs=[tok, tok, mod, mod, vec], out_specs=[tok, h_spec],
        out_shape=[jax.ShapeDtypeStruct((t, D), F32), jax.ShapeDtypeStruct(h_shape, h_dtype)],
        compiler_params=_cparams(("parallel",)),
    )(x, y2, mods_prev, mods_cur, g.reshape(1, D))


def _post_kernel(x_ref, y_ref, m_ref, g_ref, rwt_ref, rb_ref, xo_ref, h_ref, ridx_ref):
    x = x_ref[...] + m_ref[0, 2:3, :] * y_ref[...].reshape(x_ref.shape)
    xo_ref[...] = x
    h2 = _rms(x) * g_ref[...]
    h2 = h2 * (1.0 + m_ref[0, 4:5, :]) + m_ref[0, 3:4, :]
    h_ref[:, :D] = h2

    logits = _nt_dot(rwt_ref[...], h2, precision=lax.Precision.HIGHEST)
    scores = jax.nn.sigmoid(logits)
    sel = scores + rb_ref[...]
    row = lax.broadcasted_iota(jnp.int32, sel.shape, 0)
    per_group = N_EXPERTS // N_GROUPS
    best_val, best = None, None
    for g in range(N_GROUPS):
        r = [sel[per_group * g + j:per_group * g + j + 1, :] for j in range(per_group)]
        gs = None
        for i in range(per_group):
            for j in range(i + 1, per_group):
                gs = r[i] + r[j] if gs is None else jnp.maximum(gs, r[i] + r[j])
        if g == 0:
            best_val, best = gs, jnp.zeros(gs.shape, jnp.int32)
        else:
            upd = gs > best_val
            best = jnp.where(upd, g, best)
            best_val = jnp.where(upd, gs, best_val)
    masked = jnp.where((row // per_group) == best, sel, -jnp.inf)
    m1 = jnp.max(masked, axis=0, keepdims=True)
    i1 = jnp.min(jnp.where(masked == m1, row, N_EXPERTS), axis=0, keepdims=True)
    masked2 = jnp.where(row == i1, -jnp.inf, masked)
    m2 = jnp.max(masked2, axis=0, keepdims=True)
    i2 = jnp.min(jnp.where(masked2 == m2, row, N_EXPERTS), axis=0, keepdims=True)
    lo, hi = jnp.minimum(i1, i2), jnp.maximum(i1, i2)
    s_lo = jnp.sum(jnp.where(row == lo, scores, 0.0), axis=0, keepdims=True)
    s_hi = jnp.sum(jnp.where(row == hi, scores, 0.0), axis=0, keepdims=True)
    ridx_ref[...] = jnp.concatenate([lo, hi], axis=0)
    wts = jnp.concatenate([s_lo, s_hi, jnp.zeros((126, TOK), F32)], axis=0) / (s_lo + s_hi)
    h_ref[:, D:] = wts.T


def _post(x, y, mods, g, router_w, router_bias, nb):
    t = x.shape[0]
    tok = pl.BlockSpec((TOK, D), lambda i: (i, 0))
    return pl.pallas_call(
        _post_kernel,
        grid=(t // TOK,),
        in_specs=[tok, _chunk_major(nb)[1] if y.ndim == 4 else tok, pl.BlockSpec((1, 6, D), _mod_row(nb)),
                  pl.BlockSpec((1, D), lambda i: (0, 0)),
                  pl.BlockSpec((N_EXPERTS, D), lambda i: (0, 0)),
                  pl.BlockSpec((N_EXPERTS, 1), lambda i: (0, 0))],
        out_specs=[tok, pl.BlockSpec((TOK, MOE_ROW), lambda i: (i, 0)), pl.BlockSpec((2, TOK), lambda i: (0, i))],
        out_shape=[jax.ShapeDtypeStruct((t, D), F32), jax.ShapeDtypeStruct((t, MOE_ROW), F32),
                   jax.ShapeDtypeStruct((2, t), jnp.int32)],
        compiler_params=_cparams(("parallel",)),
    )(x, y, mods, g.reshape(1, D), router_w.T, router_bias.reshape(N_EXPERTS, 1))


def _final_kernel(x_ref, y_ref, m_ref, g_ref, o_ref):
    x = x_ref[...] + m_ref[0, 5:6, :] * y_ref[...]
    o_ref[...] = _rms(x) * g_ref[...]


def _final(x, y2, mods, g, nb):
    lat_tiles = SEQ // TOK
    src = lambda j: ((j // lat_tiles) * TILES_PER_B + CTX // TOK + j % lat_tiles, 0)
    tok_in = pl.BlockSpec((TOK, D), src)
    return pl.pallas_call(
        _final_kernel,
        grid=(nb * lat_tiles,),
        in_specs=[tok_in, tok_in, pl.BlockSpec((1, 6, D), lambda j: (j // lat_tiles, 0, 0)),
                  pl.BlockSpec((1, D), lambda j: (0, 0))],
        out_specs=pl.BlockSpec((TOK, D), lambda j: (j, 0)),
        out_shape=jax.ShapeDtypeStruct((nb * SEQ, D), F32),
        compiler_params=_cparams(("parallel",)),
    )(x, y2, mods, g.reshape(1, D))


def _linear_kernel(x_ref, w_ref, o_ref):
    o_ref[...] = _dot(x_ref[...].astype(BF16), w_ref[...]).astype(o_ref.dtype)


def _linear(x, w, out_dtype):
    t, k = x.shape
    n = w.shape[1]
    return pl.pallas_call(
        _linear_kernel,
        grid=(t // LIN_TM,),
        in_specs=[pl.BlockSpec((LIN_TM, k), lambda i: (i, 0)), pl.BlockSpec((k, n), lambda i: (0, 0))],
        out_specs=pl.BlockSpec((LIN_TM, n), lambda i: (i, 0)),
        out_shape=jax.ShapeDtypeStruct((t, n), out_dtype),
        compiler_params=_cparams(("parallel",)),
    )(x, w)


def _s5_operators(lam_re, lam_im, log_step, b_re, b_im, c_re, c_im):
    n = S5_CHUNK
    dt = jnp.exp(log_step)[None, :, :, None]
    k = jnp.arange(n + 1, dtype=F32)[:, None, None, None]
    mag = jnp.exp(k * lam_re[None] * dt)
    ang = k * lam_im[None] * dt
    pw_re, pw_im = mag * jnp.cos(ang), mag * jnp.sin(ang)
    a_re, a_im = pw_re[1], pw_im[1]
    den = lam_re * lam_re + lam_im * lam_im
    n_re, n_im = a_re - 1.0, a_im
    f_re = (n_re * lam_re + n_im * lam_im) / den
    f_im = (n_im * lam_re - n_re * lam_im) / den
    bt_re, bt_im = b_re.swapaxes(-1, -2), b_im.swapaxes(-1, -2)
    bb_re = f_re[:, :, None, :] * bt_re - f_im[:, :, None, :] * bt_im
    bb_im = f_re[:, :, None, :] * bt_im + f_im[:, :, None, :] * bt_re

    cp_re = c_re[None] * pw_re[:, :, :, None, :] - c_im[None] * pw_im[:, :, :, None, :]
    cp_im = c_re[None] * pw_im[:, :, :, None, :] + c_im[None] * pw_re[:, :, :, None, :]
    kk = (jnp.einsum('dgcp,kdgop->dgcko', bb_re, cp_re[:n]) - jnp.einsum('dgcp,kdgop->dgcko', bb_im, cp_im[:n]))
    wide = n * S5_GROUP
    lags = jnp.concatenate([kk[0].reshape(S5_GROUPS, S5_GROUP, wide),
                            kk[1, :, :, ::-1].reshape(S5_GROUPS, S5_GROUP, wide)], axis=1)

    def state_in(pw_sel, d):
        pr = pw_re[pw_sel, d].swapaxes(0, 1)[:, :, None, :]
        pi = pw_im[pw_sel, d].swapaxes(0, 1)[:, :, None, :]
        e_re = pr * bb_re[d][:, None] - pi * bb_im[d][:, None]
        e_im = pr * bb_im[d][:, None] + pi * bb_re[d][:, None]
        to_w = lambda e: jnp.pad(e.reshape(S5_GROUPS, wide, S5_STATE), ((0, 0), (0, 0), (0, 128 - S5_STATE)))
        return to_w(e_re), to_w(e_im)

    def state_out(pw_sel, d):
        ct_re = jnp.tile(c_re[d].swapaxes(-1, -2), (1, 1, n))
        ct_im = jnp.tile(c_im[d].swapaxes(-1, -2), (1, 1, n))
        pr = jnp.repeat(pw_re[pw_sel, d].transpose(1, 2, 0), S5_GROUP, axis=2)
        pi = jnp.repeat(pw_im[pw_sel, d].transpose(1, 2, 0), S5_GROUP, axis=2)
        to_w = lambda o: jnp.pad(o, ((0, 0), (0, 128 - S5_STATE), (0, 0)))
        return to_w(ct_re * pr - ct_im * pi), to_w(-(ct_re * pi + ct_im * pr))

    ar = jnp.arange(n)
    w1 = jnp.concatenate([*state_in(n - 1 - ar, 0), *state_in(ar, 1)], axis=2)
    w2 = jnp.concatenate([*state_out(ar + 1, 0), *state_out(n - ar, 1)], axis=1)
    pad = lambda a: jnp.pad(a, ((0, 0), (0, 128 - S5_STATE)))
    a16 = jnp.stack([pad(pw_re[n, 0]), pad(pw_im[n, 0]), pad(pw_re[n, 1]), pad(pw_im[n, 1])], axis=1)
    return lags, w1.astype(BF16), w2.astype(BF16), a16


def _s5_select():
    sel = np.zeros((S5_SLAB_GROUPS, S5_CHUNK * 128, S5_CHUNK * S5_GROUP), np.float32)
    s, c = np.meshgrid(np.arange(S5_CHUNK), np.arange(S5_GROUP), indexing="ij")
    for j in range(S5_SLAB_GROUPS):
        sel[j, s * 128 + S5_GROUP * j + c, s * S5_GROUP + c] = 1.0
    return jnp.asarray(sel, BF16)


def _s5_kernel(nb, h_ref, sel_ref, lag_ref, w1_ref, w2_ref, a_ref, y_ref, x2_scr, s_scr, x_scr, toep_scr):
    rows = S5_NCH * nb

    @pl.when(pl.program_id(1) == 0)
    def _():
        for s in range(S5_CHUNK):
            x2_scr[:, 128 * s:128 * (s + 1)] = h_ref[pl.ds(s, rows, stride=S5_CHUNK), :].astype(BF16)
        y_ref[...] = jnp.zeros(y_ref.shape, F32)

    wide = S5_CHUNK * S5_GROUP
    zeros = jnp.zeros((S5_GROUP, wide), F32)
    kf = jnp.concatenate([lag_ref[0, :S5_GROUP, :], zeros], axis=1)
    kb = jnp.concatenate([lag_ref[0, S5_GROUP:, :], zeros], axis=1)
    for s in range(S5_CHUNK):
        right, left = S5_GROUP * s, S5_GROUP * (S5_CHUNK - 1 - s)
        blk = ((pltpu.roll(kf, right, 1) if right else kf)[:, :wide]
               + (pltpu.roll(kb, 2 * wide - left, 1) if left else kb)[:, :wide])
        toep_scr[S5_GROUP * s:S5_GROUP * (s + 1), :] = blk.astype(BF16)

    u = _dot(x2_scr[...], sel_ref[0]).astype(BF16)
    s_scr[:, 0:wide] = _dot(u, toep_scr[...])
    s_scr[:, wide:] = _dot(u, w1_ref[0])

    def step(i, carry):
        cf = i
        cb = jnp.where(i < S5_CTX_CH, S5_CTX_CH - 1 - i, S5_NCH + S5_CTX_CH - 1 - i)
        rows_f = pl.ds(pl.multiple_of(cf * nb, nb), nb)
        rows_b = pl.ds(pl.multiple_of(cb * nb, nb), nb)
        fr, fi, br, bi = carry
        x_scr[rows_f, 0:128] = fr
        x_scr[rows_f, 128:256] = fi
        x_scr[rows_b, 256:384] = br
        x_scr[rows_b, 384:512] = bi
        ar, ai = a_ref[0, 0:1, :], a_ref[0, 1:2, :]
        nfr = ar * fr - ai * fi + s_scr[rows_f, 256:384]
        nfi = ar * fi + ai * fr + s_scr[rows_f, 384:512]
        ar, ai = a_ref[0, 2:3, :], a_ref[0, 3:4, :]
        nbr = ar * br - ai * bi + s_scr[rows_b, 512:640]
        nbi = ar * bi + ai * br + s_scr[rows_b, 640:768]
        return nfr, nfi, nbr, nbi

    zero = jnp.zeros((nb, 128), F32)
    lax.fori_loop(0, S5_NCH, step, (zero, zero, zero, zero))
    yg = s_scr[:, 0:256] + _dot(x_scr[...].astype(BF16), w2_ref[0])
    y_ref[0] += _nt_dot(yg.astype(BF16), sel_ref[0])


def _s5_scan(h_cm, lags, w1, w2, a16, nb):
    rows = S5_NCH * nb
    kc = S5_CHUNK * S5_GROUP
    n_slabs = D // 128
    per_group = lambda shape: pl.BlockSpec((1,) + shape, lambda q, j: (q * S5_SLAB_GROUPS + j, 0, 0))
    return pl.pallas_call(
        functools.partial(_s5_kernel, nb),
        grid=(n_slabs, S5_SLAB_GROUPS),
        in_specs=[pl.BlockSpec((nb * NTOK, 128), lambda q, j: (0, q)),
                  pl.BlockSpec((1, S5_CHUNK * 128, kc), lambda q, j: (j, 0, 0)),
                  per_group((2 * S5_GROUP, kc)), per_group((kc, 512)), per_group((512, kc)), per_group((4, 128))],
        out_specs=pl.BlockSpec((1, rows, S5_CHUNK * 128), lambda q, j: (q, 0, 0)),
        out_shape=jax.ShapeDtypeStruct((n_slabs, rows, S5_CHUNK * 128), F32),
        scratch_shapes=[pltpu.VMEM((rows, S5_CHUNK * 128), BF16), pltpu.VMEM((rows, 768), F32),
                        pltpu.VMEM((rows, 512), F32), pltpu.VMEM((kc, kc), BF16)],
        compiler_params=_cparams(("parallel", "arbitrary")),
    )(h_cm.reshape(nb * NTOK, D), _s5_select(), lags, w1, w2, a16)


def _gelu(x):
    return 0.5 * x * (1.0 + jnp.tanh(math.sqrt(2.0 / math.pi) * (x + 0.044715 * (x * x * x))))


def _glu_kernel(y_ref, h_ref, d_ref, w_ref, b_ref, o_ref, ys_scr):
    rows = TOK // S5_CHUNK
    for q in range(D // 128):
        for s in range(S5_CHUNK):
            ys_scr[q, pl.ds(s, rows, stride=S5_CHUNK), :] = y_ref[q, :, 128 * s:128 * (s + 1)]
    y = jnp.concatenate([ys_scr[q] for q in range(D // 128)], axis=1)
    z = _gelu(y + d_ref[...] * h_ref[...]).astype(BF16)
    r = _dot(z, w_ref[...]) + b_ref[...]
    o_ref[...] = r[:, :D] * jax.nn.sigmoid(r[:, D:])


def _glu(y, h, d_skip, w, b):
    t = h.shape[0]
    tok = pl.BlockSpec((TOK, D), lambda i: (i, 0))
    return pl.pallas_call(
        _glu_kernel,
        grid=(t // TOK,),
        in_specs=[pl.BlockSpec((D // 128, TOK // S5_CHUNK, S5_CHUNK * 128), lambda i: (0, i, 0)), tok,
                  pl.BlockSpec((1, D), lambda i: (0, 0)),
                  pl.BlockSpec((D, 2 * D), lambda i: (0, 0)), pl.BlockSpec((1, 2 * D), lambda i: (0, 0))],
        out_specs=tok,
        out_shape=jax.ShapeDtypeStruct((t, D), F32),
        scratch_shapes=[pltpu.VMEM((D // 128, TOK, 128), F32)],
        compiler_params=_cparams(("parallel",)),
    )(y, h, d_skip.reshape(1, D), w, b.reshape(1, 2 * D))


def _s5_mixer(h_cm, nb, lam_re, lam_im, log_step, b_re, b_im, c_re, c_im, d_skip, w_glu, b_glu):
    lags, w1, w2, a16 = _s5_operators(lam_re, lam_im, log_step, b_re, b_im, c_re, c_im)
    y = _s5_scan(h_cm, lags, w1, w2, a16, nb)
    y = _glu(y, h_cm.reshape(nb * NTOK, D), d_skip, w_glu.astype(BF16), b_glu)
    return y.reshape(h_cm.shape)


def _rope_tables():
    rows = SEQ // GRID_W
    row = jnp.repeat(jnp.arange(rows), GRID_W).astype(F32)
    col = jnp.tile(jnp.arange(GRID_W), rows).astype(F32)
    half = HEAD_DIM // 4
    inv_freq = ROPE_THETA ** (-jnp.arange(half, dtype=F32) / half)
    ang_r, ang_c = row[:, None] * inv_freq, col[:, None] * inv_freq
    cos = jnp.concatenate([jnp.cos(ang_r)] * 2 + [jnp.cos(ang_c)] * 2, axis=1)
    sin = jnp.concatenate([-jnp.sin(ang_r), jnp.sin(ang_r), -jnp.sin(ang_c), jnp.sin(ang_c)], axis=1)
    cos = jnp.concatenate([jnp.ones((CTX, HEAD_DIM), F32), cos], axis=0)
    sin = jnp.concatenate([jnp.zeros((CTX, HEAD_DIM), F32), sin], axis=0)
    return jnp.tile(cos, (1, 2)), jnp.tile(sin, (1, 2))


def _qk_prep_kernel(p_ref, cos_ref, sin_ref, qg_ref, kg_ref, q_ref, k_ref, v_ref):
    lane = lax.broadcasted_iota(jnp.int32, (TOK, 128), 1)
    low_head = lane < HEAD_DIM
    first_half = (lane % 32) < 16
    cos, sin = cos_ref[...], sin_ref[...]

    def norm_rope(x, gain):
        sq = x * x
        s_lo = jnp.sum(jnp.where(low_head, sq, 0.0), axis=1, keepdims=True)
        s_hi = jnp.sum(sq, axis=1, keepdims=True) - s_lo
        inv = lax.rsqrt(jnp.where(low_head, s_lo, s_hi) * (1.0 / HEAD_DIM) + EPS)
        xn = x * inv * gain
        partner = jnp.where(first_half, pltpu.roll(xn, 128 - 16, 1), pltpu.roll(xn, 16, 1))
        return xn * cos + partner * sin

    ones_col = (lax.broadcasted_iota(jnp.int32, (TOK, HEAD_DIM), 1) == 0).astype(BF16)
    for j in range(KV_COLS // 2 // 128):
        kt = norm_rope(p_ref[:, 128 * j:128 * (j + 1)], kg_ref[...]).astype(BF16)
        vt = p_ref[:, KV_COLS // 2 + 128 * j:KV_COLS // 2 + 128 * (j + 1)].astype(BF16)
        for hh in range(2):
            k_ref[0, 2 * j + hh] = kt[:, HEAD_DIM * hh:HEAD_DIM * (hh + 1)]
            v_ref[0, 2 * j + hh] = jnp.concatenate([vt[:, HEAD_DIM * hh:HEAD_DIM * (hh + 1)], ones_col], axis=1)
    for j in range(D // 128):
        qt = norm_rope(p_ref[:, KV_COLS + 128 * j:KV_COLS + 128 * (j + 1)], qg_ref[...])
        q_ref[:, 128 * j:128 * (j + 1)] = (qt * (HEAD_DIM ** -0.5 * math.log2(math.e))).astype(BF16)


def _qk_prep(proj, q_gain, k_gain, nb):
    t, ncol = proj.shape
    cos, sin = _rope_tables()
    tab = pl.BlockSpec((TOK, 128), lambda i: (i % TILES_PER_B, 0))
    gain = pl.BlockSpec((1, 128), lambda i: (0, 0))
    head_major = lambda width: pl.BlockSpec((1, N_KV_HEADS, TOK, width),
                                            lambda i: (i // TILES_PER_B, 0, i % TILES_PER_B, 0))
    hm_shape = lambda width: jax.ShapeDtypeStruct((nb, N_KV_HEADS, NTOK, width), BF16)
    return pl.pallas_call(
        _qk_prep_kernel,
        grid=(t // TOK,),
        in_specs=[pl.BlockSpec((TOK, ncol), lambda i: (i, 0)), tab, tab, gain, gain],
        out_specs=[pl.BlockSpec((TOK, D), lambda i: (i, 0)), head_major(HEAD_DIM), head_major(2 * HEAD_DIM)],
        out_shape=[jax.ShapeDtypeStruct((t, D), BF16), hm_shape(HEAD_DIM), hm_shape(2 * HEAD_DIM)],
        compiler_params=_cparams(("parallel",)),
    )(proj, cos, sin, jnp.tile(q_gain, 2).reshape(1, 128), jnp.tile(k_gain, 2).reshape(1, 128))


def _attn_kernel(q_ref, k_ref, v_ref, o_ref):
    def attend(n_keys):
        outs = []
        for kh in range(N_KV_HEADS):
            qs = jnp.concatenate(
                [q_ref[0, :, HEAD_DIM * (Q_PER_KV * kh + g):HEAD_DIM * (Q_PER_KV * kh + g + 1)]
                 for g in range(Q_PER_KV)], axis=0)
            s = _nt_dot(qs, k_ref[0, kh, :n_keys, :])
            p = jnp.exp2(s - jnp.max(s, axis=1, keepdims=True))
            ov = _dot(p.astype(BF16), v_ref[0, kh, :n_keys, :])
            o = ov[:, :HEAD_DIM] / ov[:, HEAD_DIM:HEAD_DIM + 1]
            outs += [o[ATT_TQ * g:ATT_TQ * (g + 1)] for g in range(Q_PER_KV)]
        o_ref[0] = jnp.concatenate(outs, axis=1).astype(BF16)

    is_ctx = pl.program_id(1) < CTX // ATT_TQ

    @pl.when(is_ctx)
    def _():
        attend(CTX)

    @pl.when(jnp.logical_not(is_ctx))
    def _():
        attend(NTOK)


def _attention(q, k, v, nb):
    kv = lambda width: pl.BlockSpec((1, N_KV_HEADS, NTOK, width), lambda b, i: (b, 0, 0, 0))
    qo = pl.BlockSpec((1, ATT_TQ, D), lambda b, i: (b, i, 0))
    return pl.pallas_call(
        _attn_kernel,
        grid=(nb, NTOK // ATT_TQ),
        in_specs=[qo, kv(HEAD_DIM), kv(2 * HEAD_DIM)], out_specs=qo,
        out_shape=jax.ShapeDtypeStruct((nb, NTOK, D), BF16),
        compiler_params=_cparams(("parallel", "parallel")),
    )(q.reshape(nb, NTOK, D), k, v)


def _gqa_mixer(h, nb, w_kvq, q_gain, k_gain, w_o):
    proj = _linear(h, w_kvq.astype(BF16), F32)
    q, k, v = _qk_prep(proj, q_gain, k_gain, nb)
    o = _attention(q, k, v, nb)
    return _linear(o.reshape(nb * NTOK, D), w_o.astype(BF16), F32)


def _split3(x):
    hi = x.astype(BF16).astype(F32)
    r = x - hi
    mid = r.astype(BF16).astype(F32)
    lo = (r - mid).astype(BF16).astype(F32)
    return hi, mid, lo


def _log_sigmoid(x):
    return jnp.minimum(x, 0.0) - jnp.log1p(jnp.exp(-jnp.abs(x)))


def _mlstm_kernel(h_ref, k_ref, v_ref, q_ref, wg_ref, wgt_ref, bc_ref, br_ref, o_ref,
                  col_scr, row_scr, st_scr, cl_scr, c0_scr, mloc_scr, m0_scr):
    hb = h_ref[0]
    g_col = _dot(hb, wg_ref[0]) + bc_ref[0]
    g_row = _nt_dot(wgt_ref[0], hb) + br_ref[0]
    ii = lax.broadcasted_iota(jnp.int32, (ML_CHUNK, ML_CHUNK), 0)
    jj = lax.broadcasted_iota(jnp.int32, (ML_CHUNK, ML_CHUNK), 1)
    low = ii >= jj
    upp = ii <= jj
    tri_l, tri_u = low.astype(BF16), upp.astype(BF16)
    tri_lu_rows = jnp.concatenate([tri_l, tri_u], axis=0)
    tri_ul_cols = jnp.concatenate([tri_u, tri_l], axis=1)
    lane_c = lax.broadcasted_iota(jnp.int32, (ML_CHUNK, 8), 1)
    sub_r = lax.broadcasted_iota(jnp.int32, (8, ML_CHUNK), 0)
    for c in range(ML_NCH):
        sl = slice(ML_CHUNK * c, ML_CHUNK * (c + 1))
        gc = g_col[sl]
        parts = _split3(jnp.where(lane_c % 2 == 1, _log_sigmoid(gc), 0.0))
        cs = _dot(tri_lu_rows, jnp.concatenate(parts, axis=1).astype(BF16))
        cs = cs[:, 0:8] + cs[:, 8:16] + cs[:, 16:24]
        cum = jnp.where((lane_c // 2) % 2 == 1, cs[ML_CHUNK:], cs[:ML_CHUNK])
        col_scr[sl, :] = jnp.where(lane_c % 2 == 1, cum, gc)
        gr = g_row[:, sl]
        parts = _split3(jnp.where(sub_r % 2 == 1, _log_sigmoid(gr), 0.0))
        rs = _dot(jnp.concatenate(parts, axis=0).astype(BF16), tri_ul_cols)
        rs = rs[0:8] + rs[8:16] + rs[16:24]
        cum = jnp.where((sub_r // 2) % 2 == 1, rs[:, ML_CHUNK:], rs[:, :ML_CHUNK])
        row_scr[c] = jnp.where(sub_r % 2 == 1, cum, gr)

    ones_col = (lax.broadcasted_iota(jnp.int32, (ML_CHUNK, ML_DV), 1) == 0).astype(BF16)

    def chunk_rows(c):
        return pl.ds(pl.multiple_of(c * ML_CHUNK, ML_CHUNK), ML_CHUNK)

    def gates(c, idx):
        rows, rr = chunk_rows(c), row_scr[c]
        ig_c, cum_c = col_scr[rows, 2 * idx:2 * idx + 1], col_scr[rows, 2 * idx + 1:2 * idx + 2]
        ig_r, cum_r = rr[2 * idx:2 * idx + 1, :], rr[2 * idx + 1:2 * idx + 2, :]
        g = cum_r[:, ML_CHUNK - 1:ML_CHUNK] if idx % 2 == 0 else cum_r[:, 0:1]
        return ig_c, cum_c, ig_r, cum_r, g

    def keys_values(c, hl):
        rows = chunk_rows(c)
        ks = k_ref[0, rows, ML_DQK * hl:ML_DQK * (hl + 1)] * (ML_DQK ** -0.5)
        v_aug = jnp.concatenate([v_ref[0, rows, ML_DV * hl:ML_DV * (hl + 1)], ones_col], axis=1)
        return ks, v_aug

    scalar_tile = lambda m: jnp.broadcast_to(m, (8, 128))

    def local_state(c, carry):
        for hl in range(2):
            ks, v_aug = keys_values(c, hl)
            ke = []
            for d in range(2):
                ig_c, cum_c, _, _, g = gates(c, 2 * hl + d)
                w = g - cum_c + ig_c
                m_loc = jnp.max(w, axis=0, keepdims=True)
                mloc_scr[c, 2 * hl + d] = scalar_tile(m_loc)
                ke.append(ks.astype(F32) * jnp.exp(w - m_loc))
            ke = jnp.concatenate(ke, axis=1).astype(BF16)
            cl_scr[hl, c] = lax.dot_general(ke, v_aug, (((0,), (0,)), ((), ())), preferred_element_type=F32)
        return carry

    lax.fori_loop(0, ML_NCH, local_state, 0)

    st_scr[...] = jnp.zeros(st_scr.shape, F32)

    def recur(i, ms):
        new_ms = []
        for hl in range(2):
            for d in range(2):
                idx = 2 * hl + d
                if d == 0:
                    c = i
                else:
                    c = jnp.where(i < ML_CTX_CH, ML_CTX_CH - 1 - i, ML_NCH + ML_CTX_CH - 1 - i)
                g = gates(c, idx)[4]
                m0, c_aug = ms[idx], st_scr[idx]
                c0_scr[idx, c] = c_aug.astype(BF16)
                m0_scr[c, idx] = scalar_tile(m0)
                m_loc = mloc_scr[c, idx, 0:1, 0:1]
                m_new = jnp.maximum(g + m0, m_loc)
                st_scr[idx] = (jnp.exp(g + m0 - m_new) * c_aug
                               + jnp.exp(m_loc - m_new) * cl_scr[hl, c, ML_DQK * d:ML_DQK * (d + 1), :])
                new_ms.append(m_new)
        return tuple(new_ms)

    lax.fori_loop(0, ML_NCH, recur, tuple(jnp.zeros((1, 1), F32) for _ in range(4)))

    def outputs(c, carry):
        rows = chunk_rows(c)
        for hl in range(2):
            ks, v_aug = keys_values(c, hl)
            q = q_ref[0, rows, ML_DQK * hl:ML_DQK * (hl + 1)]
            raw = _nt_dot(q, ks)
            qc = _dot(q, jnp.concatenate([c0_scr[2 * hl, c], c0_scr[2 * hl + 1, c]], axis=1))
            s, a_inter, m_row = [], [], []
            for d in range(2):
                idx = 2 * hl + d
                ig_c, cum_c, ig_r, cum_r, _ = gates(c, idx)
                log_d = jnp.where(low if d == 0 else upp, cum_c - cum_r + ig_r, -jnp.inf)
                log_inter = cum_c + m0_scr[c, idx, 0:1, 0:1]
                m_row.append(jnp.maximum(log_inter, jnp.max(log_d, axis=1, keepdims=True)))
                s.append(raw * jnp.exp(log_d - m_row[d]))
                a_inter.append(jnp.exp(log_inter - m_row[d]))
            sv = _dot(jnp.concatenate(s, axis=0).astype(BF16), v_aug)
            h = None
            for d in range(2):
                nd = a_inter[d] * qc[:, 2 * ML_DV * d:2 * ML_DV * (d + 1)] + sv[ML_CHUNK * d:ML_CHUNK * (d + 1)]
                den = jnp.maximum(jnp.abs(nd[:, ML_DV:ML_DV + 1]), jnp.exp(-m_row[d]))
                h = nd[:, :ML_DV] / den if h is None else h + nd[:, :ML_DV] / den
            o_ref[0, rows, ML_DV * hl:ML_DV * (hl + 1)] = h
        return carry

    lax.fori_loop(0, ML_NCH, outputs, 0)


def _mlstm_scan(h, proj, wg, wgt, b_col, b_row, nb):
    pairs = ML_HEADS // 2
    k_blk = pl.BlockSpec((1, NTOK, 2 * ML_DQK), lambda b, p: (b, 0, p))
    v_blk = pl.BlockSpec((1, NTOK, 2 * ML_DV), lambda b, p: (b, 0, ML_QK // (2 * ML_DV) + p))
    q_blk = pl.BlockSpec((1, NTOK, 2 * ML_DQK), lambda b, p: (b, 0, (ML_QK + D) // (2 * ML_DQK) + p))
    per_pair = lambda shape: pl.BlockSpec((1,) + shape, lambda b, p: (p, 0, 0))
    proj3 = proj.reshape(nb, NTOK, proj.shape[-1])
    return pl.pallas_call(
        _mlstm_kernel,
        grid=(nb, pairs),
        in_specs=[pl.BlockSpec((1, NTOK, D), lambda b, p: (b, 0, 0)), k_blk, v_blk, q_blk,
                  per_pair((D, 8)), per_pair((8, D)), per_pair((1, 8)), per_pair((8, 1))],
        out_specs=pl.BlockSpec((1, NTOK, 2 * ML_DV), lambda b, p: (b, 0, p)),
        out_shape=jax.ShapeDtypeStruct((nb, NTOK, D), F32),
        scratch_shapes=[pltpu.VMEM((NTOK, 8), F32), pltpu.VMEM((ML_NCH, 8, ML_CHUNK), F32),
                        pltpu.VMEM((4, ML_DQK, 2 * ML_DV), F32),
                        pltpu.VMEM((2, ML_NCH, 2 * ML_DQK, 2 * ML_DV), F32),
                        pltpu.VMEM((4, ML_NCH, ML_DQK, 2 * ML_DV), BF16),
                        pltpu.VMEM((ML_NCH, 4, 8, 128), F32), pltpu.VMEM((ML_NCH, 4, 8, 128), F32)],
        compiler_params=_cparams(("parallel", "parallel")),
    )(h.reshape(nb, NTOK, D), proj3, proj3, proj3, wg, wgt, b_col, b_row)


def _ml_finish_kernel(hh_ref, o_ref, g_ref, w_ref, y_ref):
    parts = [_rms(hh_ref[:, ML_DV * i:ML_DV * (i + 1)]) for i in range(ML_HEADS)]
    hn = jnp.concatenate(parts, axis=1) * g_ref[...]
    z = (hn * jax.nn.sigmoid(o_ref[...].astype(F32))).astype(BF16)
    y_ref[...] = _dot(z, w_ref[...])


def _ml_finish(hh, proj, norm_g, w_out):
    t = hh.shape[0]
    tok = pl.BlockSpec((LIN_TM, D), lambda i: (i, 0))
    return pl.pallas_call(
        _ml_finish_kernel,
        grid=(t // LIN_TM,),
        in_specs=[tok, pl.BlockSpec((LIN_TM, D), lambda i: (i, (ML_QK + D + ML_QK) // D)),
                  pl.BlockSpec((1, D), lambda i: (0, 0)), pl.BlockSpec((D, D), lambda i: (0, 0))],
        out_specs=tok,
        out_shape=jax.ShapeDtypeStruct((t, D), F32),
        compiler_params=_cparams(("parallel",)),
    )(hh, proj, norm_g.reshape(1, D), w_out)


def _ml_project_kernel(x_ref, w_ref, o_ref, w_scr):
    n_state, n_gate = ML_QK + D, 4 * ML_HEADS

    @pl.when(pl.program_id(0) == 0)
    def _():
        w_scr[:, :n_state] = w_ref[:, :n_state].astype(BF16)
        w_scr[:, n_state:] = w_ref[:, n_state + n_gate:].astype(BF16)

    o_ref[...] = _dot(x_ref[...], w_scr[...]).astype(o_ref.dtype)


def _ml_project(h, w_in):
    t = h.shape[0]
    n = w_in.shape[1] - 4 * ML_HEADS
    return pl.pallas_call(
        _ml_project_kernel,
        grid=(t // LIN_TM,),
        in_specs=[pl.BlockSpec((LIN_TM, D), lambda i: (i, 0)), pl.BlockSpec(w_in.shape, lambda i: (0, 0))],
        out_specs=pl.BlockSpec((LIN_TM, n), lambda i: (i, 0)),
        out_shape=jax.ShapeDtypeStruct((t, n), BF16),
        scratch_shapes=[pltpu.VMEM((D, n), BF16)],
        compiler_params=_cparams(("arbitrary",)),
    )(h, w_in)


def _mlstm_mixer(h, nb, w_in, b_gate, norm_g, w_out):
    n_state = ML_QK + D
    n_gate = 4 * ML_HEADS
    pairs = ML_HEADS // 2
    wg = w_in[:, n_state:n_state + n_gate].reshape(D, 2, 2, pairs, 2).transpose(3, 0, 4, 1, 2)
    wg = wg.reshape(pairs, D, 8).astype(BF16)
    bg = b_gate.reshape(2, 2, pairs, 2).transpose(2, 3, 0, 1).reshape(pairs, 8)
    proj = _ml_project(h, w_in)
    hh = _mlstm_scan(h, proj, wg, wg.transpose(0, 2, 1), bg[:, None, :], bg[:, :, None], nb)
    return _ml_finish(hh.reshape(nb * NTOK, D), proj, norm_g, w_out.astype(BF16))


PAIRS = tuple((a, b) for a in range(4) for b in range(a + 1, 4))
N_CAT = N_GROUPS * len(PAIRS)


def _route_tables(ridx, t):
    n_tiles = t // MOE_TM + N_CAT
    n_rows = n_tiles * MOE_TM
    lo, hi = ridx[0], ridx[1]
    a, b = lo % 4, hi % 4
    cat = (lo // 4) * len(PAIRS) + (a * (7 - a)) // 2 + (b - a - 1)
    onehot = (cat[:, None] == jnp.arange(N_CAT)[None, :]).astype(jnp.int32)
    rank = jnp.take_along_axis(jnp.cumsum(onehot, axis=0) - onehot, cat[:, None], axis=1)[:, 0]
    counts = jnp.sum(onehot, axis=0)
    padded = ((counts + MOE_TM - 1) // MOE_TM) * MOE_TM
    ends = jnp.cumsum(padded)
    pos = (ends - padded)[cat] + rank
    tok_ids = jnp.arange(t, dtype=jnp.int32)
    src = jnp.zeros((n_rows,), jnp.int32).at[pos].set(tok_ids)
    n_used = ends[-1] // MOE_TM
    tile_start = jnp.minimum(jnp.arange(n_tiles), n_used - 1) * MOE_TM
    tile_cat = jnp.sum((ends[None, :] <= tile_start[:, None]).astype(jnp.int32), axis=1)
    n_valid = jnp.clip((ends - padded + counts)[tile_cat] - tile_start, 0, MOE_TM)
    pa = jnp.array([p[0] for p in PAIRS], jnp.int32)[tile_cat % len(PAIRS)]
    pb = jnp.array([p[1] for p in PAIRS], jnp.int32)[tile_cat % len(PAIRS)]
    base = (tile_cat // len(PAIRS)) * 4
    i32 = lambda a: a.astype(jnp.int32)
    return i32(base + pa), i32(base + pb), i32(n_used).reshape(1), i32(n_valid), src


def _moe_kernel(ea_ref, eb_ref, nu_ref, nv_ref, tok_ref,
                h_hbm, wga_ref, wua_ref, wda_ref, wgb_ref, wub_ref, wdb_ref, y_hbm,
                xbuf, ybuf, bin_scr, gsem, ssem):
    i = pl.program_id(0)
    n_used = nu_ref[0]

    def by_eights(row_fn, **kw):
        def body(c, carry):
            for u in range(8):
                row_fn(c * 8 + u, carry, **({k: v[u] for k, v in kw.items()}))
            return carry
        return body

    def issue_gather(tile, sl):
        def row(r, carry):
            tok = tok_ref[tile * MOE_TM + r]
            pltpu.make_async_copy(h_hbm.at[pl.ds(tok, 1)], xbuf.at[sl, pl.ds(r, 1)], gsem.at[sl]).start()
            return carry
        lax.fori_loop(0, MOE_TM // 8, by_eights(row), 0)

    def wait_gather(sl):
        pltpu.make_async_copy(h_hbm.at[pl.ds(0, MOE_TM)], xbuf.at[sl], gsem.at[sl]).wait()

    def issue_scatter(tile, sl):
        def to_hbm(r, carry, priority=0):
            tok = tok_ref[tile * MOE_TM + r]
            pltpu.make_async_copy(ybuf.at[sl, pl.ds(r, 1)], y_hbm.at[pl.ds(tok, 1)],
                                  ssem.at[sl]).start(priority=priority)
            return carry

        def to_bin(r, carry):
            pltpu.make_async_copy(ybuf.at[sl, pl.ds(r, 1)], bin_scr.at[sl, pl.ds(r, 1)], ssem.at[sl]).start()
            return carry
        nv = nv_ref[tile]
        nv_down = lax.shift_right_logical(nv, 3)
        nv_up = lax.shift_right_logical(nv + 7, 3)
        lax.fori_loop(0, nv_down, by_eights(to_hbm, priority=(0, 1) * 4), 0)
        lax.fori_loop(nv_down * 8, nv, to_hbm, 0)
        lax.fori_loop(nv, nv_up * 8, to_bin, 0)
        lax.fori_loop(nv_up, MOE_TM // 8, by_eights(to_bin), 0)

    def wait_scatter(sl):
        pltpu.make_async_copy(ybuf.at[sl], y_hbm.at[pl.ds(0, MOE_TM)], ssem.at[sl]).wait()

    @pl.when(i == 0)
    def _():
        issue_gather(0, 0)

    def tile(slot):
        wait_gather(slot)

        @pl.when(i >= 2)
        def _():
            wait_scatter(slot)

        nxt = jnp.minimum(i + 1, n_used - 1) * MOE_TM
        for r in range(MOE_TM):
            pltpu.make_async_copy(h_hbm.at[pl.ds(tok_ref[nxt + r], 1)], xbuf.at[1 - slot, pl.ds(r, 1)],
                                  gsem.at[1 - slot]).start()

        x = xbuf[slot, :, :D].astype(BF16)
        w = xbuf[slot, :, D:]

        def expert(wg_ref, wu_ref, wd_ref):
            a = _dot(x, wg_ref[0, 0].astype(BF16))
            hid = (a * jax.nn.sigmoid(a) * _dot(x, wu_ref[0, 0].astype(BF16))).astype(BF16)
            return _dot(hid, wd_ref[0, 0].astype(BF16))

        ybuf[slot] = (w[:, 0:1] * expert(wga_ref, wua_ref, wda_ref)
                      + w[:, 1:2] * expert(wgb_ref, wub_ref, wdb_ref))
        issue_scatter(i, slot)

    for parity in range(2):
        pl.when(jnp.logical_and(i < n_used, i % 2 == parity))(functools.partial(tile, parity))

    @pl.when(i == pl.num_programs(0) - 1)
    def _():
        wait_scatter(0)
        wait_scatter(1)
        wait_gather(n_used % 2)


def _moe(h2, ridx, w_gate, w_up, w_down, layer):
    t = h2.shape[0]
    ea, eb, n_used, n_valid, src = _route_tables(ridx, t)
    n_tiles = ea.shape[0]
    assert t // MOE_TM >= 2
    wspec = lambda which, shape: pl.BlockSpec(
        (1, 1) + shape, (lambda i, ea, eb, nu, nv, tok: (layer, ea[i], 0, 0)) if which == 0
        else (lambda i, ea, eb, nu, nv, tok: (layer, eb[i], 0, 0)))
    wg, wu, wd = w_gate, w_up, w_down
    return pl.pallas_call(
        _moe_kernel,
        grid_spec=pltpu.PrefetchScalarGridSpec(
            num_scalar_prefetch=5,
            grid=(n_tiles,),
            in_specs=[pl.BlockSpec(memory_space=pl.ANY),
                      wspec(0, (D, D_FF)), wspec(0, (D, D_FF)), wspec(0, (D_FF, D)),
                      wspec(1, (D, D_FF)), wspec(1, (D, D_FF)), wspec(1, (D_FF, D))],
            out_specs=pl.BlockSpec(memory_space=pl.ANY),
            scratch_shapes=[pltpu.VMEM((2, MOE_TM, MOE_ROW), F32), pltpu.VMEM((2, MOE_TM, D), F32),
                            pltpu.VMEM((2, MOE_TM, D), F32),
                            pltpu.SemaphoreType.DMA((2,)), pltpu.SemaphoreType.DMA((2,))]),
        out_shape=jax.ShapeDtypeStruct((t, D), F32),
        compiler_params=_cparams(("arbitrary",)),
    )(ea, eb, n_used, n_valid, src, h2, wg, wu, wd, wg, wu, wd)


def kernel(x, c, ctx, c_ctx, ada_w, ada_b, norm1_g, norm2_g, final_g, s5_lambda_re, s5_lambda_im, s5_log_step, s5_b_re, s5_b_im, s5_c_re, s5_c_im, s5_d, s5_w_glu, s5_b_glu, attn_w_kvq, attn_q_gain, attn_k_gain, attn_w_o, ml_w_in, ml_b_gate, ml_norm_g, ml_w_out, router_w, router_bias, moe_w_gate, moe_w_up, moe_w_down):
    nb = x.shape[0]
    depth = ada_w.shape[0]
    assert x.shape[1:] == (SEQ, D) and ctx.shape[1:] == (CTX, D) and nb + 1 <= 16
    xs = jnp.concatenate([ctx, x], axis=1).reshape(nb * NTOK, D)
    cond_in = jnp.zeros((16, D), F32).at[:nb].set(c).at[nb].set(c_ctx)
    mods = _ada(cond_in, ada_w, ada_b).reshape(depth, 16, 6, D)

    y2 = None
    for layer in range(depth):
        kind, j = layer % N_MIXERS, layer // N_MIXERS
        xs, h = _pre(xs, y2, mods[layer - 1] if layer else None, mods[layer], norm1_g[layer], nb,
                     F32 if kind == 0 else BF16, chunk_major=kind == 0)
        if kind == 0:
            y = _s5_mixer(h, nb, s5_lambda_re[j], s5_lambda_im[j], s5_log_step[j], s5_b_re[j], s5_b_im[j],
                          s5_c_re[j], s5_c_im[j], s5_d[j], s5_w_glu[j], s5_b_glu[j])
        elif kind == 1:
            y = _gqa_mixer(h, nb, attn_w_kvq[j], attn_q_gain[j], attn_k_gain[j], attn_w_o[j])
        else:
            y = _mlstm_mixer(h, nb, ml_w_in[j], ml_b_gate[j], ml_norm_g[j], ml_w_out[j])
        xs, h2, ridx = _post(xs, y, mods[layer], norm2_g[layer], router_w, router_bias, nb)
        y2 = _moe(h2, ridx, moe_w_gate, moe_w_up, moe_w_down, layer)
    out = _final(xs, y2, mods[depth - 1], final_g, nb)
    return out.reshape(nb, SEQ, D)
```
